```python
import math, functools
import jax
import jax.numpy as jnp
from jax import lax
import numpy as np

D_MODEL = 1024
BATCH = 32
SEQ = 256
DEPTH = 4
DEC_BATCH = 4
DEC_SEQ = 1024
PAST_LEN = 256

GRID_W = 64
N_DIR = 2
D_MIX = D_MODEL
GROUP_W = D_MIX // 4
HEAD_DIM = 64
N_HEADS_G = GROUP_W // HEAD_DIM
D_FF = 4 * D_MODEL
MOD_CH = 6
NORM_EPS = 1e-6
LB_FLOOR = 1e-30
RW_LORA_W = 16
RW_LORA_A = 16
RW_LORA_G = 32
RW_GN_EPS = 64e-5
RW_COLS = 3 * GROUP_W + N_DIR * RW_LORA_W + N_DIR * RW_LORA_A + RW_LORA_G
RW_SPLITS = [GROUP_W, 2 * GROUP_W, 3 * GROUP_W, 3 * GROUP_W + N_DIR * RW_LORA_W,
             3 * GROUP_W + N_DIR * RW_LORA_W + N_DIR * RW_LORA_A]
HG_CHUNK = 16
HG_COLS = 5 * GROUP_W
SSD_GROUPS = 2
SSD_STATE = 128
SSD_CHUNK = 64
SS_CONV_CH = GROUP_W + 2 * SSD_GROUPS * SSD_STATE
SS_COLS = GROUP_W + SS_CONV_CH + N_DIR * N_HEADS_G
LRU_C = 8.0
LR_COLS = 2 * GROUP_W
CONV_W = 4
CONV_PAD_L = 2
CONV_PAD_R = 1
P_IN = RW_COLS + HG_COLS + SS_COLS + LR_COLS

kernel_name = 'hybrid_bidir_recurrent_diffusion_step'


def rms_norm(x, g):
    xf = x.astype(jnp.float32)
    y = xf * lax.rsqrt(jnp.mean(xf * xf, axis=-1, keepdims=True) + NORM_EPS)
    return (y * g.astype(jnp.float32)).astype(x.dtype)


def flip_if(z, d):
    return jnp.flip(z, axis=1) if d == 1 else z


def seq_centred_shift(p):
    half = p.shape[-1] // 2
    prev = jnp.pad(p[:, :-1, :half], ((0, 0), (1, 0), (0, 0)))
    nxt = jnp.pad(p[:, 1:, half:], ((0, 0), (0, 1), (0, 0)))
    return jnp.concatenate([prev, nxt], axis=-1)


def grid_q_shift(p, rows):
    b, t, ch = p.shape
    g = p.reshape(b, rows, GRID_W, ch)
    q = ch // 4
    left = jnp.pad(g[:, :, :-1, :q], ((0, 0), (0, 0), (1, 0), (0, 0)))
    right = jnp.pad(g[:, :, 1:, q:2 * q], ((0, 0), (0, 0), (0, 1), (0, 0)))
    up = jnp.pad(g[:, :-1, :, 2 * q:3 * q], ((0, 0), (1, 0), (0, 0), (0, 0)))
    down = jnp.pad(g[:, 1:, :, 3 * q:], ((0, 0), (0, 1), (0, 0), (0, 0)))
    return jnp.concatenate([left, right, up, down], axis=-1).reshape(b, t, ch)


def centred_dwconv(x, w, b):
    t = x.shape[1]
    xp = jnp.pad(x, ((0, 0), (CONV_PAD_L, CONV_PAD_R), (0, 0)))
    y = b + xp[:, 0:t] * w[0]
    for j in range(1, CONV_W):
        y = y + xp[:, j:j + t] * w[j]
    return y


def modulation(cond, w, b):
    m = jax.nn.silu(cond) @ w + b
    return m.reshape(cond.shape[0], MOD_CH, D_MODEL)


def rwkv7_scan(r, decay, k, v, kk, a, s0):
    def step(s, inp):
        r_t, w_t, k_t, v_t, kk_t, a_t = inp
        sa = jnp.einsum('bhvk,bhk->bhv', s, -kk_t)
        s = s * w_t[:, :, None, :] + sa[..., None] * (kk_t * a_t)[:, :, None, :] + v_t[..., None] * k_t[:, :, None, :]
        return s, jnp.einsum('bhvk,bhk->bhv', s, r_t)
    xs = tuple(jnp.moveaxis(z, 1, 0) for z in (r, decay, k, v, kk, a))
    s_fin, y = lax.scan(step, s0, xs)
    return jnp.moveaxis(y, 0, 1), s_fin


def rwkv7_mix(p, lw, shift_fn, s0):
    bsz, t, _ = p.shape
    heads = lambda z: z.reshape(bsz, t, N_HEADS_G, HEAD_DIM)
    p = (p + (shift_fn(p) - p) * lw['rw_mu']).astype(jnp.float32)
    r, k, v, wd, ad, gd = jnp.split(p, RW_SPLITS, axis=-1)
    g = jax.nn.sigmoid(gd) @ lw['rw_g2']
    kk = heads(k * lw['rw_kk'])
    kk = kk / jnp.maximum(jnp.sqrt(jnp.sum(kk * kk, axis=-1, keepdims=True)), 1e-12)
    rh, vh = heads(r), heads(v)
    ys, ks, finals = [], [], []
    for d in range(N_DIR):
        w_lora = jnp.tanh(wd[..., d * RW_LORA_W:(d + 1) * RW_LORA_W]) @ lw['rw_w2'][d]
        w_log = -jax.nn.softplus(-(lw['rw_w0'][d] + w_lora)) - 0.5
        decay = jnp.exp(-jnp.exp(w_log))
        a = jax.nn.sigmoid(lw['rw_a0'][d] + ad[..., d * RW_LORA_A:(d + 1) * RW_LORA_A] @ lw['rw_a2'][d])
        k_d = k * (1.0 + (a - 1.0) * lw['rw_ka'])
        seqs = [flip_if(z, d) for z in (rh, heads(decay), heads(k_d), vh, kk, heads(a))]
        y_d, s_d = rwkv7_scan(*seqs, s0[:, d].astype(jnp.float32))
        ys.append(flip_if(y_d, d))
        ks.append(k_d)
        finals.append(s_d)
    y = ys[0] + ys[1]
    mu = jnp.mean(y, axis=-1, keepdims=True)
    var = jnp.mean(jnp.square(y - mu), axis=-1, keepdims=True)
    y = ((y - mu) * lax.rsqrt(var + RW_GN_EPS)).reshape(bsz, t, GROUP_W) * lw['rw_ln_w'] + lw['rw_ln_b']
    bonus = jnp.sum(rh * heads(ks[0] + ks[1]) * lw['rw_rk'], axis=-1, keepdims=True) * vh
    out = (y + bonus.reshape(bsz, t, GROUP_W)) * g
    return out, jnp.stack(finals, axis=1)


def gla_chunk_scan(q, k, v, logf, s0):
    bsz, t, h, dk = q.shape
    dv = v.shape[-1]
    nc = t // HG_CHUNK
    q, k, logf = (z.reshape(bsz, nc, HG_CHUNK, h, dk) for z in (q, k, logf))
    v = v.reshape(bsz, nc, HG_CHUNK, h, dv)
    b = jnp.cumsum(logf, axis=2)
    tri = jnp.tril(jnp.ones((HG_CHUNK, HG_CHUNK), dtype=bool))[None, None, :, :, None, None]
    diff = b[:, :, :, None] - b[:, :, None, :]
    rel = jnp.where(tri, jnp.exp(jnp.where(tri, diff, 0.0)), 0.0)
    att = jnp.einsum('bnthk,bntshk,bnshk->bnhts', q, rel, k)
    o_intra = jnp.einsum('bnhts,bnshv->bnthv', att, v)
    b_last = b[:, :, -1]
    q_dec = q * jnp.exp(b)
    k_dec = k * jnp.exp(b_last[:, :, None] - b)
    ds = jnp.einsum('bnshk,bnshv->bnhkv', k_dec, v)

    def step(s, inp):
        dec, d_s = inp
        return dec[..., None] * s + d_s, s
    s_fin, s_prev = lax.scan(step, s0, (jnp.moveaxis(jnp.exp(b_last), 1, 0), jnp.moveaxis(ds, 1, 0)))
    o_inter = jnp.einsum('bnthk,nbhkv->bnthv', q_dec, s_prev)
    return (o_intra + o_inter).reshape(bsz, t, h, dv), s_fin


def hgrn2_mix(p, lw, lb, s0):
    bsz, t, _ = p.shape
    heads = lambda z: z.reshape(bsz, t, N_HEADS_G, HEAD_DIM)
    p = p.astype(jnp.float32)
    q, i, f_fw, f_bw, g = jnp.split(p, 5, axis=-1)
    qh, ih = heads(jax.nn.silu(q)), heads(i)
    ys, finals = [], []
    for d, f_raw in enumerate((f_fw, f_bw)):
        lb_d = lb[d]
        logf = jnp.logaddexp(jnp.log(jnp.maximum(lb_d, LB_FLOOR)), jnp.log1p(-lb_d) + jax.nn.log_sigmoid(f_raw))
        k_d = -jnp.expm1(logf)
        seqs = [flip_if(z, d) for z in (qh, heads(k_d), ih, heads(logf))]
        y_d, s_d = gla_chunk_scan(*seqs, s0[:, d].astype(jnp.float32))
        ys.append(flip_if(y_d, d))
        finals.append(s_d)
    o = (ys[0] + ys[1]).reshape(bsz, t, GROUP_W)
    out = rms_norm(o, lw['hg_norm']) * jax.nn.silu(g)
    return out, jnp.stack(finals, axis=1)


def ssd_chunk_scan(x, dt, a_neg, bm, cm, s0):
    bsz, t, h, pd = x.shape
    nc = t // SSD_CHUNK
    rep = h // SSD_GROUPS
    bh = jnp.repeat(bm, rep, axis=2).reshape(bsz, nc, SSD_CHUNK, h, SSD_STATE)
    ch = jnp.repeat(cm, rep, axis=2).reshape(bsz, nc, SSD_CHUNK, h, SSD_STATE)
    x = x.reshape(bsz, nc, SSD_CHUNK, h, pd)
    dt = dt.reshape(bsz, nc, SSD_CHUNK, h)
    acum = jnp.cumsum(dt * a_neg, axis=2)
    tri = jnp.tril(jnp.ones((SSD_CHUNK, SSD_CHUNK), dtype=bool))[None, None, :, :, None]
    seg = acum[:, :, :, None, :] - acum[:, :, None, :, :]
    decay_ts = jnp.where(tri, jnp.exp(jnp.where(tri, seg, 0.0)), 0.0)
    xdt = x * dt[..., None]
    cb = jnp.einsum('bzthn,bzshn->bztsh', ch, bh)
    y_diag = jnp.einsum('bztsh,bzshp->bzthp', cb * decay_ts, xdt)
    a_last = acum[:, :, -1]
    ds = jnp.einsum('bzshn,bzsh,bzshp->bzhpn', bh, jnp.exp(a_last[:, :, None] - acum), xdt)

    def step(s, inp):
        dec, d_s = inp
        return dec[..., None, None] * s + d_s, s
    s_fin, s_prev = lax.scan(step, s0, (jnp.moveaxis(jnp.exp(a_last), 1, 0), jnp.moveaxis(ds, 1, 0)))
    y_off = jnp.einsum('bzthn,zbhpn->bzthp', ch, s_prev) * jnp.exp(acum)[..., None]
    return (y_diag + y_off).reshape(bsz, t, h, pd), s_fin


def ssd_mix(p, lw, s0):
    bsz, t, _ = p.shape
    z, xbc, dtr = jnp.split(p, [GROUP_W, GROUP_W + SS_CONV_CH], axis=-1)
    xbc = jax.nn.silu(centred_dwconv(xbc, lw['ss_conv_w'], lw['ss_conv_b'])).astype(jnp.float32)
    xs, bm, cm = jnp.split(xbc, [GROUP_W, GROUP_W + SSD_GROUPS * SSD_STATE], axis=-1)
    xh = xs.reshape(bsz, t, N_HEADS_G, HEAD_DIM)
    bm = bm.reshape(bsz, t, SSD_GROUPS, SSD_STATE)
    cm = cm.reshape(bsz, t, SSD_GROUPS, SSD_STATE)
    ys, finals = [], []
    for d in range(N_DIR):
        dt = jax.nn.softplus(dtr[..., d * N_HEADS_G:(d + 1) * N_HEADS_G].astype(jnp.float32) + lw['ss_dt_bias'][d])
        a_neg = -jnp.exp(lw['ss_A_log'][d].astype(jnp.float32))
        y_d, s_d = ssd_chunk_scan(flip_if(xh, d), flip_if(dt, d), a_neg, flip_if(bm, d), flip_if(cm, d),
                                  s0[:, d].astype(jnp.float32))
        ys.append(flip_if(y_d, d))
        finals.append(s_d)
    y = ys[0] + ys[1] + lw['ss_D'][:, None] * xh
    out = rms_norm(y.reshape(bsz, t, GROUP_W) * jax.nn.silu(z.astype(jnp.float32)), lw['ss_norm'])
    return out, jnp.stack(finals, axis=1)


def diag_linear_scan(a, b, h0):
    b = b.at[:, 0].add(a[:, 0] * h0)

    def comb(l, r):
        return (l[0] * r[0], r[0] * l[1] + r[1])
    _, h = lax.associative_scan(comb, (a, b), axis=1)
    return h, h[:, -1]


def rglru_mix(p, lw, s0):
    bsz, t, _ = p.shape
    ybr, xbr = jnp.split(p, 2, axis=-1)
    xc = centred_dwconv(xbr, lw['lr_conv_w'], lw['lr_conv_b']).astype(jnp.float32)
    xh = xc.reshape(bsz, t, N_HEADS_G, HEAD_DIM)
    hs, finals = [], []
    for d in range(N_DIR):
        r = jax.nn.sigmoid(jnp.einsum('bthi,hij->bthj', xh, lw['lr_wa'][d]).reshape(bsz, t, GROUP_W) + lw['lr_ba'][d])
        i = jax.nn.sigmoid(jnp.einsum('bthi,hij->bthj', xh, lw['lr_wx'][d]).reshape(bsz, t, GROUP_W) + lw['lr_bx'][d])
        log_a = -LRU_C * r * jax.nn.softplus(-lw['lr_lam'][d].astype(jnp.float32))
        a = jnp.exp(log_a)
        b = jnp.sqrt(jnp.maximum(-jnp.expm1(2.0 * log_a), 0.0)) * (i * xc)
        h_d, s_d = diag_linear_scan(flip_if(a, d), flip_if(b, d), s0[:, d].astype(jnp.float32))
        hs.append(flip_if(h_d, d))
        finals.append(s_d)
    out = (hs[0] + hs[1]) * jax.nn.gelu(ybr.astype(jnp.float32))
    return out, jnp.stack(finals, axis=1)


def trunk_layer(x, mod, lw, lb, shift_fn, states):
    sh1, sc1, g1, sh2, sc2, g2 = (mod[:, i, None, :] for i in range(MOD_CH))
    h = rms_norm(x, lw['g_pre1']) * (1 + sc1) + sh1
    proj = h @ lw['w_in']
    p_rw, p_hg, p_ss, p_lr = jnp.split(proj, [RW_COLS, RW_COLS + HG_COLS, RW_COLS + HG_COLS + SS_COLS], axis=-1)
    s_rw, s_hg, s_ss, s_lr = states
    y_rw, f_rw = rwkv7_mix(p_rw, lw, shift_fn, s_rw)
    y_hg, f_hg = hgrn2_mix(p_hg, lw, lb, s_hg)
    y_ss, f_ss = ssd_mix(p_ss, lw, s_ss)
    y_lr, f_lr = rglru_mix(p_lr, lw, s_lr)
    mix = jnp.concatenate([y_rw, y_hg, y_ss, y_lr], axis=-1).astype(x.dtype) @ lw['w_out']
    x = x + g1 * rms_norm(mix, lw['g_post1'])
    h = rms_norm(x, lw['g_pre2']) * (1 + sc2) + sh2
    ff = jnp.square(jax.nn.relu(h @ lw['w1'])) @ lw['w2']
    x = x + g2 * rms_norm(ff, lw['g_post2'])
    return x, (f_rw, f_hg, f_ss, f_lr)


def setup_inputs(seed: int = 0) -> dict:
    key = jax.random.key(seed)
    ks = iter(jax.random.split(key, 64))
    f32 = jnp.float32

    def nrm(shape, scale=1.0):
        return scale * jax.random.normal(next(ks), shape, f32)

    def uni(shape, lo, hi):
        return jax.random.uniform(next(ks), shape, f32, lo, hi)
    H = N_HEADS_G
    dt0 = jnp.exp(uni((DEPTH, N_DIR, H), math.log(1e-3), math.log(1e-1)))
    lam_u = uni((DEPTH, N_DIR, GROUP_W), 0.9, 0.999) ** (1.0 / LRU_C)
    return {
        'x_prompt': nrm((BATCH, SEQ, D_MODEL)),
        'x_sample': nrm((DEC_BATCH, DEC_SEQ, D_MODEL)),
        'state_rwkv': nrm((DEC_BATCH, DEPTH, N_DIR, H, HEAD_DIM, HEAD_DIM), 0.5),
        'state_hgrn': nrm((DEC_BATCH, DEPTH, N_DIR, H, HEAD_DIM, HEAD_DIM), 0.5),
        'state_ssd': nrm((DEC_BATCH, DEPTH, N_DIR, H, HEAD_DIM, SSD_STATE), 0.5),
        'state_lru': nrm((DEC_BATCH, DEPTH, N_DIR, GROUP_W), 0.5),
        'c': nrm((DEC_BATCH, D_MODEL)),
        'c_ctx': nrm((D_MODEL,)),
        'mod_w': nrm((DEPTH, D_MODEL, MOD_CH * D_MODEL), 0.5 * D_MODEL ** -0.5),
        'mod_b': nrm((DEPTH, MOD_CH * D_MODEL), 0.01),
        'norm_g': 1.0 + nrm((DEPTH, 4, D_MODEL), 0.02),
        'w_in': nrm((DEPTH, D_MODEL, P_IN), D_MODEL ** -0.5),
        'w_out': nrm((DEPTH, D_MIX, D_MODEL), D_MIX ** -0.5),
        'rw_mu': uni((DEPTH, RW_COLS), 0.0, 1.0),
        'rw_w0': uni((DEPTH, N_DIR, GROUP_W), -4.0, -1.0),
        'rw_w2': nrm((DEPTH, N_DIR, RW_LORA_W, GROUP_W), 0.1),
        'rw_a0': nrm((DEPTH, N_DIR, GROUP_W), 0.1),
        'rw_a2': nrm((DEPTH, N_DIR, RW_LORA_A, GROUP_W), 0.1),
        'rw_g2': nrm((DEPTH, RW_LORA_G, GROUP_W), RW_LORA_G ** -0.5),
        'rw_kk': 0.85 + nrm((DEPTH, GROUP_W), 0.02),
        'rw_ka': 1.0 + nrm((DEPTH, GROUP_W), 0.02),
        'rw_rk': nrm((DEPTH, H, HEAD_DIM), 0.1),
        'rw_ln_w': 1.0 + nrm((DEPTH, GROUP_W), 0.02),
        'rw_ln_b': nrm((DEPTH, GROUP_W), 0.01),
        'hg_lb': 1.0 + nrm((DEPTH, N_DIR, GROUP_W), 0.1),
        'hg_norm': 1.0 + nrm((DEPTH, GROUP_W), 0.02),
        'ss_conv_w': nrm((DEPTH, CONV_W, SS_CONV_CH), CONV_W ** -0.5),
        'ss_conv_b': nrm((DEPTH, SS_CONV_CH), 0.01),
        'ss_dt_bias': dt0 + jnp.log(-jnp.expm1(-dt0)),
        'ss_A_log': jnp.log(uni((DEPTH, N_DIR, H), 1.0, 16.0)),
        'ss_D': 1.0 + nrm((DEPTH, H), 0.02),
        'ss_norm': 1.0 + nrm((DEPTH, GROUP_W), 0.02),
        'lr_conv_w': nrm((DEPTH, CONV_W, GROUP_W), CONV_W ** -0.5),
        'lr_conv_b': nrm((DEPTH, GROUP_W), 0.01),
        'lr_wa': nrm((DEPTH, N_DIR, H, HEAD_DIM, HEAD_DIM), HEAD_DIM ** -0.5),
        'lr_ba': nrm((DEPTH, N_DIR, GROUP_W), 0.01),
        'lr_wx': nrm((DEPTH, N_DIR, H, HEAD_DIM, HEAD_DIM), HEAD_DIM ** -0.5),
        'lr_bx': nrm((DEPTH, N_DIR, GROUP_W), 0.01),
        'lr_lam': jnp.log(lam_u) - jnp.log1p(-lam_u),
        'mlp_w1': nrm((DEPTH, D_MODEL, D_FF), D_MODEL ** -0.5),
        'mlp_w2': nrm((DEPTH, D_FF, D_MODEL), D_FF ** -0.5),
    }


def reference(x_prompt, x_sample, state_rwkv, state_hgrn, state_ssd, state_lru, c, c_ctx,
              mod_w, mod_b, norm_g, w_in, w_out,
              rw_mu, rw_w0, rw_w2, rw_a0, rw_a2, rw_g2, rw_kk, rw_ka, rw_rk, rw_ln_w, rw_ln_b,
              hg_lb, hg_norm,
              ss_conv_w, ss_conv_b, ss_dt_bias, ss_A_log, ss_D, ss_norm,
              lr_conv_w, lr_conv_b, lr_wa, lr_ba, lr_wx, lr_bx, lr_lam,
              mlp_w1, mlp_w2):
    f32 = jnp.float32
    lb_soft = jax.nn.softmax(hg_lb.astype(f32), axis=0)
    lower_bounds = jnp.cumsum(lb_soft, axis=0) - lb_soft[0]
    rows = x_sample.shape[1] // GRID_W
    latent_shift = functools.partial(grid_q_shift, rows=rows)
    bp = x_prompt.shape[0]
    zero_states = (jnp.zeros((bp, N_DIR, N_HEADS_G, HEAD_DIM, HEAD_DIM), f32),
                   jnp.zeros((bp, N_DIR, N_HEADS_G, HEAD_DIM, HEAD_DIM), f32),
                   jnp.zeros((bp, N_DIR, N_HEADS_G, HEAD_DIM, SSD_STATE), f32),
                   jnp.zeros((bp, N_DIR, GROUP_W), f32))
    xp, xs = x_prompt, x_sample
    new_rw, new_hg, new_ss, new_lr = [], [], [], []
    for l in range(DEPTH):
        lw = {
            'w_in': w_in[l], 'w_out': w_out[l],
            'g_pre1': norm_g[l, 0], 'g_post1': norm_g[l, 1], 'g_pre2': norm_g[l, 2], 'g_post2': norm_g[l, 3],
            'w1': mlp_w1[l], 'w2': mlp_w2[l],
            'rw_mu': rw_mu[l], 'rw_w0': rw_w0[l], 'rw_w2': rw_w2[l], 'rw_a0': rw_a0[l], 'rw_a2': rw_a2[l],
            'rw_g2': rw_g2[l], 'rw_kk': rw_kk[l], 'rw_ka': rw_ka[l], 'rw_rk': rw_rk[l],
            'rw_ln_w': rw_ln_w[l], 'rw_ln_b': rw_ln_b[l],
            'hg_norm': hg_norm[l],
            'ss_conv_w': ss_conv_w[l], 'ss_conv_b': ss_conv_b[l], 'ss_dt_bias': ss_dt_bias[l],
            'ss_A_log': ss_A_log[l], 'ss_D': ss_D[l], 'ss_norm': ss_norm[l],
            'lr_conv_w': lr_conv_w[l], 'lr_conv_b': lr_conv_b[l], 'lr_wa': lr_wa[l], 'lr_ba': lr_ba[l],
            'lr_wx': lr_wx[l], 'lr_bx': lr_bx[l], 'lr_lam': lr_lam[l],
        }
        mod_ctx = modulation(c_ctx[None, :], mod_w[l], mod_b[l])
        mod_lat = modulation(c, mod_w[l], mod_b[l])
        xp, (f_rw, f_hg, f_ss, f_lr) = trunk_layer(xp, mod_ctx, lw, lower_bounds[l], seq_centred_shift, zero_states)
        new_rw.append(f_rw)
        new_hg.append(f_hg)
        new_ss.append(f_ss)
        new_lr.append(f_lr)
        cached = (state_rwkv[:, l], state_hgrn[:, l], state_ssd[:, l], state_lru[:, l])
        xs, _ = trunk_layer(xs, mod_lat, lw, lower_bounds[l], latent_shift, cached)
    new_state_rwkv = jnp.stack(new_rw, axis=1).astype(x_prompt.dtype)
    new_state_hgrn = jnp.stack(new_hg, axis=1).astype(x_prompt.dtype)
    new_state_ssd = jnp.stack(new_ss, axis=1).astype(x_prompt.dtype)
    new_state_lru = jnp.stack(new_lr, axis=1).astype(x_prompt.dtype)
    return (xp, xs, new_state_rwkv, new_state_hgrn, new_state_ssd, new_state_lru)
```

```python
import functools

import jax
import jax.numpy as jnp
from jax import lax
from jax.experimental import pallas as pl
from jax.experimental.pallas import tpu as pltpu

F32 = jnp.float32
BF16 = jnp.bfloat16

D_MODEL = 1024
DEPTH = 4
MOD_CH = 6
GROUP_W = 256
HEAD_DIM = 64
N_HEADS = 4
N_PAIRS = 2
PAIR_W = 128
D_FF = 4096
GRID_W = 64
NORM_EPS = 1e-6
LB_FLOOR = 1e-30
RW_GN_EPS = 64e-5
LRU_C = 8.0
SSD_STATE = 128
CONV_TAPS = (-2, -1, 0, 1)

RW_COLS = 864
RW_PAD = 896
RW_SHIFT_SPLIT = RW_COLS // 4
HG_COLS = 1280
SS_COLS = 1032
SS_PAD = 1152
LR_COLS = 512
RW_CHUNK = 64
HG_BLOCK = 16
SSD_CHUNK = 256
SCAN_BLOCK = 8

TOK_TILE = 512
VMEM_LIMIT = 56 * 1024 * 1024


def _cparams(sem):
    return pltpu.CompilerParams(dimension_semantics=sem, vmem_limit_bytes=VMEM_LIMIT)


def _mm(a, b):
    return jnp.dot(a.astype(BF16), b.astype(BF16), preferred_element_type=F32)


def _mm_nt(a, b):
    return lax.dot_general(a.astype(BF16), b.astype(BF16), (((1,), (1,)), ((), ())),
                           preferred_element_type=F32)


def _mm_tn(a, b):
    return lax.dot_general(a.astype(BF16), b.astype(BF16), (((0,), (0,)), ((), ())),
                           preferred_element_type=F32)


def _mm_split(a, b_bf16):
    hi = a.astype(BF16)
    lo = (a - hi.astype(F32)).astype(BF16)
    return (jnp.dot(hi, b_bf16, preferred_element_type=F32)
            + jnp.dot(lo, b_bf16, preferred_element_type=F32))


def _row_iota(shape):
    return lax.broadcasted_iota(jnp.int32, shape, 0)


def _lane_iota(shape):
    return lax.broadcasted_iota(jnp.int32, shape, 1)


def _shift_rows(x, d):
    n = x.shape[0]
    s = (-d) % n
    if s == 0:
        return x
    return pltpu.roll(x, s, 0)


def _seg_cumsum(x, seg, rev):
    r = _row_iota(x.shape) % seg
    k = 1
    while k < seg:
        if rev:
            x = x + jnp.where(r < seg - k, _shift_rows(x, k), 0.0)
        else:
            x = x + jnp.where(r >= k, _shift_rows(x, -k), 0.0)
        k *= 2
    return x


def _sigmoid(x):
    return jax.nn.sigmoid(x)


def _silu(x):
    return x * _sigmoid(x)


def _softplus(x):
    return jnp.maximum(x, 0.0) + jnp.log(1.0 + jnp.exp(-jnp.abs(x)))


def _gelu_tanh(x):
    c = 0.7978845608028654
    return 0.5 * x * (1.0 + jnp.tanh(c * (x + 0.044715 * (x * x * x))))


def _rms(x, g):
    return x * lax.rsqrt(jnp.mean(x * x, axis=-1, keepdims=True) + NORM_EPS) * g


def _conv_rows(x, w_ref, b_ref):
    t = x.shape[0]
    rows = _row_iota(x.shape)
    y = b_ref[...] + jnp.zeros_like(x)
    for j, d in enumerate(CONV_TAPS):
        valid = (rows + d >= 0) & (rows + d < t)
        y = y + jnp.where(valid, _shift_rows(x, d), 0.0) * w_ref[j:j + 1, :]
    return y


def _mod_kernel(c_ref, w_ref, b_ref, o_ref):
    c = c_ref[...]
    o_ref[...] = _mm(_silu(c), w_ref[...]) + b_ref[...]


def _mod_call(cond8, mod_w, mod_b):
    n_col = MOD_CH
    return pl.pallas_call(
        _mod_kernel,
        grid=(DEPTH, n_col),
        in_specs=[
            pl.BlockSpec((8, D_MODEL), lambda l, j: (0, 0)),
            pl.BlockSpec((None, D_MODEL, D_MODEL), lambda l, j: (l, 0, j)),
            pl.BlockSpec((None, 1, D_MODEL), lambda l, j: (l, 0, j)),
        ],
        out_specs=pl.BlockSpec((None, 8, D_MODEL), lambda l, j: (l, 0, j)),
        out_shape=jax.ShapeDtypeStruct((DEPTH, 8, MOD_CH * D_MODEL), F32),
        compiler_params=_cparams(("arbitrary", "arbitrary")),
        name="modulation",
    )(cond8, mod_w, mod_b.reshape(DEPTH, 1, MOD_CH * D_MODEL))


def _inproj_kernel(x_ref, mod_ref, g_ref, wrw, whg, wss, wlr, orw, ohg, oss, olr):
    x = x_ref[...]
    m = mod_ref[...]
    h = _rms(x, g_ref[...]) * (1.0 + m[1:2, :]) + m[0:1, :]
    hb = h.astype(BF16)
    orw[...] = jnp.dot(hb, wrw[...], preferred_element_type=F32)
    ohg[...] = jnp.dot(hb, whg[...], preferred_element_type=F32)
    oss[...] = jnp.dot(hb, wss[...], preferred_element_type=F32)
    olr[...] = jnp.dot(hb, wlr[...], preferred_element_type=F32)


def _inproj_call(x, mod_l, g, wrw, whg, wss, wlr, n_ctx_tiles, lat_tiles_per_seq):
    n_tok = x.shape[0]
    n_tiles = n_tok // TOK_TILE

    def seq_of(i):
        return jnp.where(i < n_ctx_tiles, 0, 1 + (i - n_ctx_tiles) // lat_tiles_per_seq)

    full = lambda a: pl.BlockSpec(a.shape, lambda i: (0,) * a.ndim)
    widths = (RW_PAD, HG_COLS, SS_PAD, LR_COLS)
    return pl.pallas_call(
        _inproj_kernel,
        grid=(n_tiles,),
        in_specs=[
            pl.BlockSpec((TOK_TILE, D_MODEL), lambda i: (i, 0)),
            pl.BlockSpec((None, MOD_CH, D_MODEL), lambda i: (seq_of(i), 0, 0)),
            full(g), full(wrw), full(whg), full(wss), full(wlr),
        ],
        out_specs=[pl.BlockSpec((TOK_TILE, w), lambda i: (i, 0)) for w in widths],
        out_shape=[jax.ShapeDtypeStruct((n_tok, w), F32) for w in widths],
        compiler_params=_cparams(("arbitrary",)),
        name="in_proj",
    )(x, mod_l, g, wrw, whg, wss, wlr)


def _outmlp_kernel(n_ctx_tiles, x_ref, c0, c1, c2, c3, l0, l1, l2, l3, mod_ref, g_ref,
                   wout, w1, w2, o_ref):
    is_ctx = pl.program_id(0) < n_ctx_tiles
    mix = jnp.concatenate(
        [jnp.where(is_ctx, c[...], l[...]) for c, l in ((c0, l0), (c1, l1), (c2, l2), (c3, l3))],
        axis=-1).astype(BF16)
    m = mod_ref[...]
    g = g_ref[...]
    x = x_ref[...]
    t = jnp.dot(mix, wout[...], preferred_element_type=F32)
    x1 = x + m[2:3, :] * _rms(t, g[1:2, :])
    h2 = (_rms(x1, g[2:3, :]) * (1.0 + m[4:5, :]) + m[3:4, :]).astype(BF16)
    acc = jnp.zeros_like(x)
    for f in range(D_FF // D_MODEL):
        u = jnp.dot(h2, w1[:, f * D_MODEL:(f + 1) * D_MODEL], preferred_element_type=F32)
        u = jnp.square(jnp.maximum(u, 0.0)).astype(BF16)
        acc = acc + jnp.dot(u, w2[f * D_MODEL:(f + 1) * D_MODEL, :], preferred_element_type=F32)
    o_ref[...] = x1 + m[5:6, :] * _rms(acc, g[3:4, :])


def _outmlp_call(x, ys_ctx, ys_lat, mod_l, g4, wout, w1, w2, n_ctx_tiles, lat_tiles_per_seq):
    n_tok = x.shape[0]
    n_tiles = n_tok // TOK_TILE
    n_lat_tiles = n_tiles - n_ctx_tiles

    def seq_of(i):
        return jnp.where(i < n_ctx_tiles, 0, 1 + (i - n_ctx_tiles) // lat_tiles_per_seq)

    ctx_spec = pl.BlockSpec((TOK_TILE, GROUP_W), lambda i: (jnp.minimum(i, n_ctx_tiles - 1), 0))
    lat_spec = pl.BlockSpec((TOK_TILE, GROUP_W),
                            lambda i: (jnp.clip(i - n_ctx_tiles, 0, n_lat_tiles - 1), 0))
    const = lambda a: pl.BlockSpec(a.shape, lambda i: (0,) * a.ndim, pipeline_mode=pl.Buffered(1))
    return pl.pallas_call(
        functools.partial(_outmlp_kernel, n_ctx_tiles),
        grid=(n_tiles,),
        in_specs=[pl.BlockSpec((TOK_TILE, D_MODEL), lambda i: (i, 0))]
        + [ctx_spec] * 4 + [lat_spec] * 4
        + [pl.BlockSpec((None, MOD_CH, D_MODEL), lambda i: (seq_of(i), 0, 0)),
           const(g4), const(wout), const(w1), const(w2)],
        out_specs=pl.BlockSpec((TOK_TILE, D_MODEL), lambda i: (i, 0)),
        out_shape=jax.ShapeDtypeStruct((n_tok, D_MODEL), F32),
        compiler_params=_cparams(("arbitrary",)),
        name="out_mlp",
    )(x, *ys_ctx, *ys_lat, mod_l, g4, wout, w1, w2)


def _lru_kernel(T, p_ref, cw_ref, cb_ref, wcat_ref, bcat_ref, lam_ref, h0_ref, y_ref, fin_ref,
                ac_ref, bc_ref, h_ref):
    ybr = p_ref[:, 0:GROUP_W]
    xbr = p_ref[:, GROUP_W:2 * GROUP_W]
    xc = _conv_rows(xbr, cw_ref, cb_ref)
    gates = _sigmoid(_mm(xc, wcat_ref[...]) + bcat_ref[...])
    sp = _softplus(-lam_ref[...])
    rows = _row_iota((T, GROUP_W)) % SCAN_BLOCK
    n_blk = T // SCAN_BLOCK
    total = jnp.zeros((T, GROUP_W), F32)
    for d in range(2):
        r = gates[:, (2 * d) * GROUP_W:(2 * d + 1) * GROUP_W]
        ig = gates[:, (2 * d + 1) * GROUP_W:(2 * d + 2) * GROUP_W]
        log_a = (-LRU_C) * r * sp[d:d + 1, :]
        a = jnp.exp(log_a)
        b = jnp.sqrt(jnp.maximum(1.0 - a * a, 0.0)) * (ig * xc)
        k = 1
        while k < SCAN_BLOCK:
            if d == 0:
                ok = rows >= k
                a_s, b_s = _shift_rows(a, -k), _shift_rows(b, -k)
            else:
                ok = rows < SCAN_BLOCK - k
                a_s, b_s = _shift_rows(a, k), _shift_rows(b, k)
            b = jnp.where(ok, a * b_s + b, b)
            a = jnp.where(ok, a * a_s, a)
            k *= 2
        ac_ref[...] = a
        bc_ref[...] = b
        edge = SCAN_BLOCK - 1 if d == 0 else 0

        def body(j, carry, d=d, edge=edge):
            jb = j if d == 0 else n_blk - 1 - j
            r0 = pl.multiple_of(jb * SCAN_BLOCK, SCAN_BLOCK)
            hb = ac_ref[pl.ds(r0, SCAN_BLOCK), :] * carry + bc_ref[pl.ds(r0, SCAN_BLOCK), :]
            h_ref[pl.ds(r0, SCAN_BLOCK), :] = hb
            return jnp.broadcast_to(hb[edge:edge + 1, :], (SCAN_BLOCK, GROUP_W))

        carry0 = jnp.broadcast_to(h0_ref[d:d + 1, :], (SCAN_BLOCK, GROUP_W))
        last = lax.fori_loop(0, n_blk, body, carry0)
        fin_ref[d:d + 1, :] = last[0:1, :]
        total = total + h_ref[...]
    y_ref[...] = total * _gelu_tanh(ybr)


def _lru_call(p, T, n_seq, blk0, cw, cb, wcat, bcat, lam, h0):
    const = lambda a: pl.BlockSpec(a.shape, lambda i: (0,) * a.ndim)
    return pl.pallas_call(
        functools.partial(_lru_kernel, T),
        grid=(n_seq,),
        in_specs=[pl.BlockSpec((T, LR_COLS), lambda i: (blk0 + i, 0)),
                  const(cw), const(cb), const(wcat), const(bcat), const(lam),
                  pl.BlockSpec((None, 2, GROUP_W), lambda i: (i, 0, 0))],
        out_specs=[pl.BlockSpec((T, GROUP_W), lambda i: (i, 0)),
                   pl.BlockSpec((None, 2, GROUP_W), lambda i: (i, 0, 0))],
        out_shape=[jax.ShapeDtypeStruct((n_seq * T, GROUP_W), F32),
                   jax.ShapeDtypeStruct((n_seq, 2, GROUP_W), F32)],
        scratch_shapes=[pltpu.VMEM((T, GROUP_W), F32)] * 3,
        compiler_params=_cparams(("arbitrary",)),
        name="rglru_T%d" % T,
    )(p, cw, cb, wcat, bcat, lam, h0)


def _ssd_kernel(T, has_init, p_ref, cw_ref, cb_ref, dtb_ref, aneg_ref, dexp_ref, nrm_ref, s0_ref,
                y_ref, fin_ref):
    Q = SSD_CHUNK
    nq = T // Q
    z = p_ref[:, 0:GROUP_W]
    xbc = _silu(_conv_rows(p_ref[:, GROUP_W:4 * GROUP_W], cw_ref, cb_ref))
    xs = xbc[:, 0:GROUP_W]
    bm = xbc[:, GROUP_W:2 * GROUP_W]
    cm = xbc[:, 2 * GROUP_W:3 * GROUP_W]
    dt = _softplus(p_ref[:, 4 * GROUP_W:4 * GROUP_W + PAIR_W] + dtb_ref[...])
    da = dt * aneg_ref[...]
    acf = _seg_cumsum(da, Q, rev=False)
    acb = _seg_cumsum(da, Q, rev=True)
    ti = _row_iota((Q, Q))
    si = _lane_iota((Q, Q))
    low = si <= ti
    upp = si >= ti
    lane_lo = _lane_iota((Q, PAIR_W)) < HEAD_DIM
    row_lo = _row_iota((PAIR_W, SSD_STATE)) < HEAD_DIM

    def col(a, j):
        return a[:, j:j + 1]

    ydiag, ds, dec, ecol = [], [], [], []
    for q in range(nq):
        sl = slice(q * Q, (q + 1) * Q)
        acf_q, acb_q, dt_q = acf[sl], acb[sl], dt[sl]
        acf_t, acb_t, dt_t = acf_q.T, acb_q.T, dt_q.T
        y_q, ds_q, dec_q, ec_q = [], [], [], []
        for pr in range(N_PAIRS):
            ls = slice(pr * PAIR_W, (pr + 1) * PAIR_W)
            bg, cg, xp = bm[sl, ls], cm[sl, ls], xs[sl, ls]
            g = _mm_nt(cg, bg)
            outs = []
            wf_cols, wb_cols, ef_cols, eb_cols, decf, decb = [], [], [], [], [], []
            for hh in range(2):
                h = 2 * pr + hh
                hb_ = N_HEADS + h
                lf = jnp.exp(jnp.minimum(col(acf_q, h) - acf_t[h:h + 1, :], 0.0)) * dt_t[h:h + 1, :]
                lb = jnp.exp(jnp.minimum(col(acb_q, hb_) - acb_t[hb_:hb_ + 1, :], 0.0)) \
                    * dt_t[hb_:hb_ + 1, :]
                mh = g * (jnp.where(low, lf, 0.0) + jnp.where(upp, lb, 0.0))
                outs.append(_mm(mh, xp))
                af_last = acf_q[Q - 1:Q, h:h + 1]
                ab_last = acb_q[0:1, hb_:hb_ + 1]
                wf_cols.append(jnp.exp(af_last - col(acf_q, h)) * col(dt_q, h))
                wb_cols.append(jnp.exp(ab_last - col(acb_q, hb_)) * col(dt_q, hb_))
                ef_cols.append(jnp.exp(col(acf_q, h)))
                eb_cols.append(jnp.exp(col(acb_q, hb_)))
                decf.append(jnp.exp(af_last))
                decb.append(jnp.exp(ab_last))
            y_q.append(jnp.where(lane_lo, outs[0], outs[1]))
            wf = jnp.where(lane_lo, wf_cols[0], wf_cols[1])
            wb = jnp.where(lane_lo, wb_cols[0], wb_cols[1])
            ds_q.append((_mm_tn(xp * wf, bg), _mm_tn(xp * wb, bg)))
            dec_q.append((jnp.where(row_lo, decf[0], decf[1]), jnp.where(row_lo, decb[0], decb[1])))
            ec_q.append((jnp.where(lane_lo, ef_cols[0], ef_cols[1]),
                         jnp.where(lane_lo, eb_cols[0], eb_cols[1])))
        ydiag.append(y_q)
        ds.append(ds_q)
        dec.append(dec_q)
        ecol.append(ec_q)

    prev = [[[None, None] for _ in range(N_PAIRS)] for _ in range(nq)]
    for pr in range(N_PAIRS):
        for d in range(2):
            s = s0_ref[d, pr] if has_init else jnp.zeros((PAIR_W, SSD_STATE), F32)
            order = range(nq) if d == 0 else range(nq - 1, -1, -1)
            for q in order:
                prev[q][pr][d] = s
                s = dec[q][pr][d] * s + ds[q][pr][d]
            fin_ref[d, pr] = s

    ys = []
    for q in range(nq):
        sl = slice(q * Q, (q + 1) * Q)
        parts = []
        for pr in range(N_PAIRS):
            ls = slice(pr * PAIR_W, (pr + 1) * PAIR_W)
            yp = ydiag[q][pr]
            if has_init or nq > 1:
                cg = cm[sl, ls]
                for d in range(2):
                    yp = yp + _mm_nt(cg, prev[q][pr][d]) * ecol[q][pr][d]
            parts.append(yp)
        ys.append(jnp.concatenate(parts, axis=-1))
    y = jnp.concatenate(ys, axis=0) if nq > 1 else ys[0]
    y = y + dexp_ref[...] * xs
    y_ref[...] = _rms(y * _silu(z), nrm_ref[...])


def _ssd_call(p, T, n_seq, blk0, has_init, cw, cb, dtb, aneg, dexp, nrm, s0):
    const = lambda a: pl.BlockSpec(a.shape, lambda i: (0,) * a.ndim)
    st_spec = pl.BlockSpec((None, 2, N_PAIRS, PAIR_W, SSD_STATE), lambda i: (i, 0, 0, 0, 0))
    return pl.pallas_call(
        functools.partial(_ssd_kernel, T, has_init),
        grid=(n_seq,),
        in_specs=[pl.BlockSpec((T, SS_PAD), lambda i: (blk0 + i, 0)),
                  const(cw), const(cb), const(dtb), const(aneg), const(dexp), const(nrm), st_spec],
        out_specs=[pl.BlockSpec((T, GROUP_W), lambda i: (i, 0)), st_spec],
        out_shape=[jax.ShapeDtypeStruct((n_seq * T, GROUP_W), F32),
                   jax.ShapeDtypeStruct((n_seq, 2, N_PAIRS, PAIR_W, SSD_STATE), F32)],
        compiler_params=_cparams(("arbitrary",)),
        name="ssd_T%d" % T,
    )(p, cw, cb, dtb, aneg, dexp, nrm, s0)


def _hgrn_kernel(T, p_ref, la_ref, l1_ref, lbd_ref, nrm_ref, bd_ref, s0_ref, y_ref, fin_ref,
                 q_sc, v_sc, k_sc, b_sc, o_sc, st_sc):
    NB = HG_BLOCK
    n_blk = T // NB
    q_sc[...] = _silu(p_ref[:, 0:GROUP_W])
    v_sc[...] = p_ref[:, GROUP_W:2 * GROUP_W]
    for d in range(2):
        x = p_ref[:, (2 + d) * GROUP_W:(3 + d) * GROUP_W]
        e1 = jnp.exp(-jnp.abs(x))
        lsig = jnp.minimum(x, 0.0) - jnp.log(1.0 + e1)
        a_ = la_ref[d:d + 1, :] + jnp.zeros_like(x)
        b_ = l1_ref[d:d + 1, :] + lsig
        logf = jnp.maximum(a_, b_) + jnp.log(1.0 + jnp.exp(-jnp.abs(a_ - b_)))
        k_sc[d] = jnp.exp(l1_ref[d:d + 1, :]) * (jnp.where(x >= 0.0, e1, 1.0) / (1.0 + e1)) \
            - lbd_ref[d:d + 1, :]
        b_sc[d] = _seg_cumsum(logf, NB, rev=(d == 1))
    for i in range(2 * N_PAIRS):
        st_sc[i] = s0_ref[i // N_PAIRS, i % N_PAIRS]
    bd = bd_ref[...]
    rows16 = _row_iota((NB, GROUP_W))
    bdm = (_row_iota((PAIR_W, PAIR_W)) < HEAD_DIM) == (_lane_iota((PAIR_W, PAIR_W)) < HEAD_DIM)

    def body(j, carry):
        for d in range(2):
            jb = j if d == 0 else n_blk - 1 - j
            r0 = pl.multiple_of(jb * NB, NB)
            qb = q_sc[pl.ds(r0, NB), :]
            vb = v_sc[pl.ds(r0, NB), :]
            kb = k_sc[d, pl.ds(r0, NB), :]
            bb = b_sc[d, pl.ds(r0, NB), :]
            parts = []
            for s in range(NB):
                es = jnp.exp(jnp.minimum(bb - bb[s:s + 1, :], 0.0))
                keep = (rows16 >= s) if d == 0 else (rows16 <= s)
                parts.append(jnp.where(keep, qb * es * kb[s:s + 1, :], 0.0))
            att = _mm(jnp.concatenate(parts, axis=0), bd)
            o = jnp.zeros((NB, GROUP_W), F32)
            for s in range(NB):
                o = o + att[s * NB:(s + 1) * NB, :] * vb[s:s + 1, :]
            bl = bb[NB - 1:NB, :] if d == 0 else bb[0:1, :]
            qd = qb * jnp.exp(bb)
            kh = kb * jnp.exp(bl - bb)
            dec = jnp.exp(bl)
            inter = []
            for pr in range(N_PAIRS):
                ls = slice(pr * PAIR_W, (pr + 1) * PAIR_W)
                st = st_sc[d * N_PAIRS + pr]
                inter.append(_mm_nt(qd[:, ls], st))
                st_sc[d * N_PAIRS + pr] = st * dec[:, ls] + jnp.where(
                    bdm, _mm_tn(vb[:, ls], kh[:, ls]), 0.0)
            o_sc[d, pl.ds(r0, NB), :] = o + jnp.concatenate(inter, axis=-1)
        return carry

    lax.fori_loop(0, n_blk, body, 0)
    for i in range(2 * N_PAIRS):
        fin_ref[i // N_PAIRS, i % N_PAIRS] = st_sc[i]
    g = p_ref[:, 4 * GROUP_W:5 * GROUP_W]
    y_ref[...] = _rms(o_sc[0] + o_sc[1], nrm_ref[...]) * _silu(g)


def _hgrn_call(p, T, n_seq, blk0, la, l1, lbd, nrm, bd, s0):
    const = lambda a: pl.BlockSpec(a.shape, lambda i: (0,) * a.ndim)
    st_spec = pl.BlockSpec((None, 2, N_PAIRS, PAIR_W, PAIR_W), lambda i: (i, 0, 0, 0, 0))
    tw = pltpu.VMEM((T, GROUP_W), F32)
    tw2 = pltpu.VMEM((2, T, GROUP_W), F32)
    return pl.pallas_call(
        functools.partial(_hgrn_kernel, T),
        grid=(n_seq,),
        in_specs=[pl.BlockSpec((T, HG_COLS), lambda i: (blk0 + i, 0)),
                  const(la), const(l1), const(lbd), const(nrm), const(bd), st_spec],
        out_specs=[pl.BlockSpec((T, GROUP_W), lambda i: (i, 0)), st_spec],
        out_shape=[jax.ShapeDtypeStruct((n_seq * T, GROUP_W), F32),
                   jax.ShapeDtypeStruct((n_seq, 2, N_PAIRS, PAIR_W, PAIR_W), F32)],
        scratch_shapes=[tw, tw, tw2, tw2, tw2,
                        pltpu.VMEM((2 * N_PAIRS, PAIR_W, PAIR_W), F32)],
        compiler_params=_cparams(("arbitrary",)),
        name="hgrn2_T%d" % T,
    )(p, la, l1, lbd, nrm, bd, s0)


def _rwkv_kernel(T, lat, p_ref, mu_ref, sm_ref, w0_ref, a0_ref, wl_ref, kk_ref, ka_ref, rk_ref,
                 lnw_ref, lnb_ref, bd_ref, s0_ref, y_ref, fin_ref,
                 xa_sc, xr_sc, yb_sc, yk_sc, c_sc, v_sc, yo_sc, st_sc):
    C = RW_CHUNK
    n_chunk = T // C
    P = p_ref[...]
    rowi = _row_iota(P.shape)
    if lat:
        gcol = rowi % GRID_W
        shifts = [(-1, gcol >= 1), (1, gcol <= GRID_W - 2), (-GRID_W, rowi >= GRID_W),
                  (GRID_W, rowi < T - GRID_W)]
    else:
        shifts = [(-1, rowi >= 1), (1, rowi <= T - 2)]
    sh = jnp.zeros_like(P)
    for j, (dlt, valid) in enumerate(shifts):
        sh = jnp.where(valid & (sm_ref[j:j + 1, :] > 0.5), _shift_rows(P, dlt), sh)
    P = P + (sh - P) * mu_ref[...]
    r = P[:, 0:GROUP_W]
    k = P[:, GROUP_W:2 * GROUP_W]
    v = P[:, 2 * GROUP_W:3 * GROUP_W]
    lo = P[:, 3 * GROUP_W:3 * GROUP_W + PAIR_W]
    ln = _lane_iota(lo.shape)
    act = jnp.where(ln < 32, jnp.tanh(lo), jnp.where(ln < 64, lo, jnp.where(ln < 96, _sigmoid(lo), 0.0)))
    lora = _mm(act, wl_ref[...])
    gate = lora[:, 4 * GROUP_W:5 * GROUP_W]
    bd = bd_ref[...]
    kkv = k * kk_ref[...]
    kkn = kkv / jnp.maximum(jnp.sqrt(_mm_split(kkv * kkv, bd)), 1e-12)
    v_sc[...] = v
    ksum = jnp.zeros_like(k)
    for d in range(2):
        w_lora = lora[:, d * GROUP_W:(d + 1) * GROUP_W]
        a_lora = lora[:, (2 + d) * GROUP_W:(3 + d) * GROUP_W]
        w_log = -_softplus(-(w0_ref[d:d + 1, :] + w_lora)) - 0.5
        logw = -jnp.exp(w_log)
        a_sig = _sigmoid(a0_ref[d:d + 1, :] + a_lora)
        k_d = k * (1.0 + (a_sig - 1.0) * ka_ref[...])
        ksum = ksum + k_d
        c = _seg_cumsum(logw, C, rev=(d == 1))
        einv = jnp.exp(-c)
        xa_sc[d] = (-kkn) * jnp.exp(c - logw)
        xr_sc[d] = r * jnp.exp(c)
        yb_sc[d] = (kkn * a_sig) * einv
        yk_sc[d] = k_d * einv
        c_sc[d] = c
    for i in range(2 * N_PAIRS):
        st_sc[i] = s0_ref[i // N_PAIRS, i % N_PAIRS]

    lane_lo = _lane_iota((C, PAIR_W)) < HEAD_DIM
    tt = _row_iota((PAIR_W, PAIR_W)) % C
    ss = _lane_iota((PAIR_W, PAIR_W)) % C
    tt2 = _row_iota((PAIR_W, 2 * PAIR_W)) % C
    ss2 = _lane_iota((PAIR_W, 2 * PAIR_W)) % C
    n_fac = C.bit_length() - 1

    def stack2(x):
        return jnp.concatenate([jnp.where(lane_lo, x, 0.0), jnp.where(lane_lo, 0.0, x)], axis=0)

    def body(j, carry):
        for d in range(2):
            jc = j if d == 0 else n_chunk - 1 - j
            r0 = pl.multiple_of(jc * C, C)
            strict = (ss < tt) if d == 0 else (ss > tt)
            incl2 = (ss2 <= tt2) if d == 0 else (ss2 >= tt2)
            for pr in range(N_PAIRS):
                ls = slice(pr * PAIR_W, (pr + 1) * PAIR_W)
                at = xa_sc[d, pl.ds(r0, C), ls]
                rt = xr_sc[d, pl.ds(r0, C), ls]
                bt = yb_sc[d, pl.ds(r0, C), ls]
                kt = yk_sc[d, pl.ds(r0, C), ls]
                vt = v_sc[pl.ds(r0, C), ls]
                cc = c_sc[d, pl.ds(r0, C), ls]
                pc = jnp.exp(cc[C - 1:C, :] if d == 0 else cc[0:1, :])
                lhs = jnp.concatenate([stack2(at), stack2(rt)], axis=0)
                rhs_t = jnp.concatenate([stack2(bt), stack2(kt)], axis=0)
                vst = stack2(vt)
                sbd = st_sc[d * N_PAIRS + pr]
                gg = _mm_nt(lhs, rhs_t)
                qm = jnp.where(strict, gg[0:PAIR_W, 0:PAIR_W], 0.0)
                mak = jnp.where(strict, gg[0:PAIR_W, PAIR_W:2 * PAIR_W], 0.0)
                ncat = jnp.where(incl2, gg[PAIR_W:2 * PAIR_W, :], 0.0)
                uu = _mm_nt(lhs, sbd)
                xs = uu[0:PAIR_W, :] + _mm(mak, vst)
                qb16 = qm.astype(BF16)
                for i in range(n_fac):
                    xh = xs.astype(BF16)
                    xl = (xs - xh.astype(F32)).astype(BF16)
                    corr = jnp.dot(qb16, xl, preferred_element_type=F32)
                    if i < n_fac - 1:
                        zz = jnp.dot(qb16, jnp.concatenate([xh, qb16], axis=1),
                                     preferred_element_type=F32)
                        xs = xs + zz[:, 0:PAIR_W] + corr
                        qb16 = zz[:, PAIR_W:2 * PAIR_W].astype(BF16)
                    else:
                        xs = xs + jnp.dot(qb16, xh, preferred_element_type=F32) + corr
                sv = jnp.concatenate([xs, vst], axis=0)
                yst = uu[PAIR_W:2 * PAIR_W, :] + _mm(ncat, sv)
                yo_sc[d, pl.ds(r0, C), ls] = yst[0:C, :] + yst[C:2 * C, :]
                st_sc[d * N_PAIRS + pr] = sbd * pc + _mm_tn(sv, rhs_t * pc)
        return carry

    lax.fori_loop(0, n_chunk, body, 0)
    for i in range(2 * N_PAIRS):
        fin_ref[i // N_PAIRS, i % N_PAIRS] = st_sc[i]
    y = yo_sc[0] + yo_sc[1]
    inv_n = 1.0 / HEAD_DIM
    mu = _mm_split(y, bd) * inv_n
    yc = y - mu
    var = _mm_split(yc * yc, bd) * inv_n
    yn = yc * lax.rsqrt(var + RW_GN_EPS) * lnw_ref[...] + lnb_ref[...]
    bonus = _mm_split(r * ksum * rk_ref[...], bd) * v
    y_ref[...] = (yn + bonus) * gate


def _rwkv_call(p, T, n_seq, blk0, lat, mu, sm, w0, a0, wl, kk, ka, rk, lnw, lnb, bd, s0):
    const = lambda a: pl.BlockSpec(a.shape, lambda i: (0,) * a.ndim)
    st_spec = pl.BlockSpec((None, 2, N_PAIRS, PAIR_W, PAIR_W), lambda i: (i, 0, 0, 0, 0))
    tw = pltpu.VMEM((T, GROUP_W), F32)
    tw2 = pltpu.VMEM((2, T, GROUP_W), F32)
    return pl.pallas_call(
        functools.partial(_rwkv_kernel, T, lat),
        grid=(n_seq,),
        in_specs=[pl.BlockSpec((T, RW_PAD), lambda i: (blk0 + i, 0)),
                  const(mu), const(sm), const(w0), const(a0), const(wl), const(kk), const(ka),
                  const(rk), const(lnw), const(lnb), const(bd), st_spec],
        out_specs=[pl.BlockSpec((T, GROUP_W), lambda i: (i, 0)), st_spec],
        out_shape=[jax.ShapeDtypeStruct((n_seq * T, GROUP_W), F32),
                   jax.ShapeDtypeStruct((n_seq, 2, N_PAIRS, PAIR_W, PAIR_W), F32)],
        scratch_shapes=[tw2, tw2, tw2, tw2, tw2, tw, tw2,
                        pltpu.VMEM((2 * N_PAIRS, PAIR_W, PAIR_W), F32)],
        compiler_params=_cparams(("arbitrary",)),
        name="rwkv7_T%d" % T,
    )(p, mu, sm, w0, a0, wl, kk, ka, rk, lnw, lnb, bd, s0)


def _pair_blockdiag(s):
    b = s.shape[0]
    s = s.reshape(b, 2, N_PAIRS, 2, HEAD_DIM, HEAD_DIM)
    z = jnp.zeros_like(s[:, :, :, 0])
    top = jnp.concatenate([s[:, :, :, 0], z], axis=-1)
    bot = jnp.concatenate([z, s[:, :, :, 1]], axis=-1)
    return jnp.concatenate([top, bot], axis=-2)


def _pair_diag_blocks(s):
    b = s.shape[0]
    h0 = s[:, :, :, 0:HEAD_DIM, 0:HEAD_DIM]
    h1 = s[:, :, :, HEAD_DIM:, HEAD_DIM:]
    return jnp.stack([h0, h1], axis=3).reshape(b, 2, N_HEADS, HEAD_DIM, HEAD_DIM)


def _head_blockdiag(w):
    out = jnp.zeros((GROUP_W, GROUP_W), w.dtype)
    for h in range(N_HEADS):
        out = out.at[h * HEAD_DIM:(h + 1) * HEAD_DIM, h * HEAD_DIM:(h + 1) * HEAD_DIM].set(w[h])
    return out


def _pad_cols(a, n):
    return jnp.pad(a, ((0, 0),) * (a.ndim - 1) + ((0, n - a.shape[-1]),))


def kernel(x_prompt, x_sample, state_rwkv, state_hgrn, state_ssd, state_lru, c, c_ctx, mod_w, mod_b, norm_g, w_in, w_out, rw_mu, rw_w0, rw_w2, rw_a0, rw_a2, rw_g2, rw_kk, rw_ka, rw_rk, rw_ln_w, rw_ln_b, hg_lb, hg_norm, ss_conv_w, ss_conv_b, ss_dt_bias, ss_A_log, ss_D, ss_norm, lr_conv_w, lr_conv_b, lr_wa, lr_ba, lr_wx, lr_bx, lr_lam, mlp_w1, mlp_w2):
    bp, t_ctx, _ = x_prompt.shape
    bl, t_lat, _ = x_sample.shape
    n_ctx_tok = bp * t_ctx
    n_ctx_tiles = n_ctx_tok // TOK_TILE
    lat_tiles_per_seq = t_lat // TOK_TILE
    lat_blk0 = n_ctx_tok // t_lat
    assert n_ctx_tok % t_lat == 0 and t_lat % TOK_TILE == 0 and n_ctx_tok % TOK_TILE == 0
    assert t_lat // GRID_W * GRID_W == t_lat and 1 + bl <= 8

    x = jnp.concatenate([x_prompt.reshape(n_ctx_tok, D_MODEL), x_sample.reshape(bl * t_lat, D_MODEL)], axis=0)
    cond8 = jnp.zeros((8, D_MODEL), F32).at[0].set(c_ctx).at[1:1 + bl].set(c)
    mod = _mod_call(cond8, mod_w, mod_b).reshape(DEPTH, 8, MOD_CH, D_MODEL)

    lb_soft = jax.nn.softmax(hg_lb.astype(F32), axis=0)
    lower = jnp.cumsum(lb_soft, axis=0) - lb_soft[0]
    lb_floor = jnp.maximum(lower, LB_FLOOR)
    hg_la = jnp.log(lb_floor)
    hg_l1 = jnp.log1p(-lower)
    hg_lbd = lb_floor - lower

    lane = jnp.arange(GROUP_W)
    bd_ones = (lane[:, None] // HEAD_DIM == lane[None, :] // HEAD_DIM).astype(BF16)
    col = jnp.arange(RW_PAD)
    quarter = col // RW_SHIFT_SPLIT
    real = col < RW_COLS
    sm_lat = jnp.stack([(quarter == i) & real for i in range(4)]).astype(F32)
    sm_ctx = jnp.stack([(quarter < 2) & real, (quarter >= 2) & real,
                        jnp.zeros_like(real), jnp.zeros_like(real)]).astype(F32)

    zeros_pair = jnp.zeros((bp, 2, N_PAIRS, PAIR_W, PAIR_W), F32)
    zeros_ssd = jnp.zeros((bp, 2, N_PAIRS, PAIR_W, SSD_STATE), F32)
    zeros_lru = jnp.zeros((bp, 2, GROUP_W), F32)

    new_rw, new_hg, new_ss, new_lr = [], [], [], []
    for l in range(DEPTH):
        o1 = RW_COLS
        o2 = o1 + HG_COLS
        o3 = o2 + SS_COLS
        wl_in = w_in[l].astype(BF16)
        w_rw = _pad_cols(wl_in[:, :o1], RW_PAD)
        w_hg = wl_in[:, o1:o2]
        w_ss = _pad_cols(wl_in[:, o2:o3], SS_PAD)
        w_lr = wl_in[:, o3:]
        g4 = norm_g[l]
        p_rw, p_hg, p_ss, p_lr = _inproj_call(x, mod[l], g4[0:1], w_rw, w_hg, w_ss, w_lr,
                                              n_ctx_tiles, lat_tiles_per_seq)

        wl = jnp.zeros((PAIR_W, 5 * GROUP_W), F32)
        wl = wl.at[0:16, 0:GROUP_W].set(rw_w2[l, 0]).at[16:32, GROUP_W:2 * GROUP_W].set(rw_w2[l, 1])
        wl = wl.at[32:48, 2 * GROUP_W:3 * GROUP_W].set(rw_a2[l, 0])
        wl = wl.at[48:64, 3 * GROUP_W:4 * GROUP_W].set(rw_a2[l, 1])
        wl = wl.at[64:96, 4 * GROUP_W:5 * GROUP_W].set(rw_g2[l]).astype(BF16)
        rw_args = (_pad_cols(rw_mu[l][None, :], RW_PAD),)
        rw_tail = (rw_w0[l], rw_a0[l], wl, rw_kk[l][None, :], rw_ka[l][None, :],
                   rw_rk[l].reshape(1, GROUP_W), rw_ln_w[l][None, :], rw_ln_b[l][None, :], bd_ones)
        y_rw_c, f_rw = _rwkv_call(p_rw, t_ctx, bp, 0, False, *rw_args, sm_ctx, *rw_tail, zeros_pair)
        y_rw_l, _ = _rwkv_call(p_rw, t_lat, bl, lat_blk0, True, *rw_args, sm_lat, *rw_tail,
                               _pair_blockdiag(state_rwkv[:, l]))

        hg_args = (hg_la[l], hg_l1[l], hg_lbd[l], hg_norm[l][None, :], bd_ones)
        y_hg_c, f_hg = _hgrn_call(p_hg, t_ctx, bp, 0, *hg_args, zeros_pair)
        y_hg_l, _ = _hgrn_call(p_hg, t_lat, bl, lat_blk0, *hg_args,
                               _pair_blockdiag(jnp.swapaxes(state_hgrn[:, l], -1, -2)))

        dtb = _pad_cols(ss_dt_bias[l].reshape(1, 2 * N_HEADS), PAIR_W)
        aneg = _pad_cols(-jnp.exp(ss_A_log[l].astype(F32)).reshape(1, 2 * N_HEADS), PAIR_W)
        dexp = jnp.repeat(ss_D[l], HEAD_DIM)[None, :]
        ss_args = (ss_conv_w[l], ss_conv_b[l][None, :], dtb, aneg, dexp, ss_norm[l][None, :])
        y_ss_c, f_ss = _ssd_call(p_ss, t_ctx, bp, 0, False, *ss_args, zeros_ssd)
        y_ss_l, _ = _ssd_call(p_ss, t_lat, bl, lat_blk0, True, *ss_args,
                              state_ssd[:, l].reshape(bl, 2, N_PAIRS, PAIR_W, SSD_STATE))

        wcat = jnp.concatenate([_head_blockdiag(lr_wa[l, 0]), _head_blockdiag(lr_wx[l, 0]),
                                _head_blockdiag(lr_wa[l, 1]), _head_blockdiag(lr_wx[l, 1])],
                               axis=1).astype(BF16)
        bcat = jnp.concatenate([lr_ba[l, 0], lr_bx[l, 0], lr_ba[l, 1], lr_bx[l, 1]])[None, :]
        lr_args = (lr_conv_w[l], lr_conv_b[l][None, :], wcat, bcat, lr_lam[l])
        y_lr_c, f_lr = _lru_call(p_lr, t_ctx, bp, 0, *lr_args, zeros_lru)
        y_lr_l, _ = _lru_call(p_lr, t_lat, bl, lat_blk0, *lr_args, state_lru[:, l])

        x = _outmlp_call(x, (y_rw_c, y_hg_c, y_ss_c, y_lr_c), (y_rw_l, y_hg_l, y_ss_l, y_lr_l),
                         mod[l], g4, w_out[l].astype(BF16), mlp_w1[l].astype(BF16),
                         mlp_w2[l].astype(BF16), n_ctx_tiles, lat_tiles_per_seq)

        new_rw.append(_pair_diag_blocks(f_rw))
        new_hg.append(jnp.swapaxes(_pair_diag_blocks(f_hg), -1, -2))
        new_ss.append(f_ss.reshape(bp, 2, N_HEADS, HEAD_DIM, SSD_STATE))
        new_lr.append(f_lr)

    y_prompt = x[:n_ctx_tok].reshape(bp, t_ctx, D_MODEL)
    y_sample = x[n_ctx_tok:].reshape(bl, t_lat, D_MODEL)
    return (y_prompt, y_sample, jnp.stack(new_rw, axis=1), jnp.stack(new_hg, axis=1),
            jnp.stack(new_ss, axis=1), jnp.stack(new_lr, axis=1))
```

```python
import functools

import jax
import jax.numpy as jnp
from jax import lax
from jax.experimental import pallas as pl
from jax.experimental.pallas import tpu as pltpu

F32 = jnp.float32
BF16 = jnp.bfloat16

D_MODEL = 1024
DEPTH = 4
MOD_CH = 6
GROUP_W = 256
HEAD_DIM = 64
N_HEADS = 4
N_PAIRS = 2
PAIR_W = 128
D_FF = 4096
GRID_W = 64
NORM_EPS = 1e-6
LB_FLOOR = 1e-30
RW_GN_EPS = 64e-5
LRU_C = 8.0
SSD_STATE = 128
CONV_TAPS = (-2, -1, 0, 1)

RW_COLS = 864
RW_PAD = 896
RW_SHIFT_SPLIT = RW_COLS // 4
HG_COLS = 1280
SS_COLS = 1032
SS_PAD = 1152
LR_COLS = 512
RW_CHUNK = 64
RW_CTX_SEQS = 4
RW_LAT_SEQS = 1
HG_BLOCK = 16
SSD_CHUNK = 256
SCAN_BLOCK = 8

TOK_TILE = 512
VMEM_LIMIT = 56 * 1024 * 1024


def _cparams(sem):
    return pltpu.CompilerParams(dimension_semantics=sem, vmem_limit_bytes=VMEM_LIMIT)


def _mm(a, b):
    return jnp.dot(a.astype(BF16), b.astype(BF16), preferred_element_type=F32)


def _mm_nt(a, b):
    return lax.dot_general(a.astype(BF16), b.astype(BF16), (((1,), (1,)), ((), ())),
                           preferred_element_type=F32)


def _mm_tn(a, b):
    return lax.dot_general(a.astype(BF16), b.astype(BF16), (((0,), (0,)), ((), ())),
                           preferred_element_type=F32)


def _mm_split(a, b_bf16):
    hi = a.astype(BF16)
    lo = (a - hi.astype(F32)).astype(BF16)
    return (jnp.dot(hi, b_bf16, preferred_element_type=F32)
            + jnp.dot(lo, b_bf16, preferred_element_type=F32))


def _row_iota(shape):
    return lax.broadcasted_iota(jnp.int32, shape, 0)


def _lane_iota(shape):
    return lax.broadcasted_iota(jnp.int32, shape, 1)


def _shift_rows(x, d):
    n = x.shape[0]
    s = (-d) % n
    if s == 0:
        return x
    return pltpu.roll(x, s, 0)


def _seg_cumsum(x, seg, rev):
    r = _row_iota(x.shape) % seg
    k = 1
    while k < seg:
        if rev:
            x = x + jnp.where(r < seg - k, _shift_rows(x, k), 0.0)
        else:
            x = x + jnp.where(r >= k, _shift_rows(x, -k), 0.0)
        k *= 2
    return x


def _sigmoid(x):
    return jax.nn.sigmoid(x)


def _silu(x):
    return x * _sigmoid(x)


def _softplus(x):
    return jnp.maximum(x, 0.0) + jnp.log(1.0 + jnp.exp(-jnp.abs(x)))


def _gelu_tanh(x):
    c = 0.7978845608028654
    return 0.5 * x * (1.0 + jnp.tanh(c * (x + 0.044715 * (x * x * x))))


def _rms(x, g):
    return x * lax.rsqrt(jnp.mean(x * x, axis=-1, keepdims=True) + NORM_EPS) * g


def _conv_rows(x, w_ref, b_ref):
    t = x.shape[0]
    rows = _row_iota(x.shape)
    y = b_ref[...] + jnp.zeros_like(x)
    for j, d in enumerate(CONV_TAPS):
        valid = (rows + d >= 0) & (rows + d < t)
        y = y + jnp.where(valid, _shift_rows(x, d), 0.0) * w_ref[j:j + 1, :]
    return y


def _mod_kernel(c_ref, w_ref, b_ref, o_ref):
    c = c_ref[...]
    o_ref[...] = _mm(_silu(c), w_ref[...]) + b_ref[...]


def _mod_call(cond8, mod_w, mod_b):
    n_col = MOD_CH
    return pl.pallas_call(
        _mod_kernel,
        grid=(DEPTH, n_col),
        in_specs=[
            pl.BlockSpec((8, D_MODEL), lambda l, j: (0, 0)),
            pl.BlockSpec((None, D_MODEL, D_MODEL), lambda l, j: (l, 0, j)),
            pl.BlockSpec((None, 1, D_MODEL), lambda l, j: (l, 0, j)),
        ],
        out_specs=pl.BlockSpec((None, 8, D_MODEL), lambda l, j: (l, 0, j)),
        out_shape=jax.ShapeDtypeStruct((DEPTH, 8, MOD_CH * D_MODEL), F32),
        compiler_params=_cparams(("arbitrary", "arbitrary")),
        name="modulation",
    )(cond8, mod_w, mod_b.reshape(DEPTH, 1, MOD_CH * D_MODEL))


def _inproj_kernel(x_ref, mod_ref, g_ref, wrw, whg, wss, wlr, orw, ohg, oss, olr):
    x = x_ref[...]
    m = mod_ref[...]
    h = _rms(x, g_ref[...]) * (1.0 + m[1:2, :]) + m[0:1, :]
    hb = h.astype(BF16)
    orw[...] = jnp.dot(hb, wrw[...], preferred_element_type=F32)
    ohg[...] = jnp.dot(hb, whg[...], preferred_element_type=F32)
    oss[...] = jnp.dot(hb, wss[...], preferred_element_type=F32)
    olr[...] = jnp.dot(hb, wlr[...], preferred_element_type=F32)


def _inproj_call(x, mod_l, g, wrw, whg, wss, wlr, n_ctx_tiles, lat_tiles_per_seq):
    n_tok = x.shape[0]
    n_tiles = n_tok // TOK_TILE

    def seq_of(i):
        return jnp.where(i < n_ctx_tiles, 0, 1 + (i - n_ctx_tiles) // lat_tiles_per_seq)

    full = lambda a: pl.BlockSpec(a.shape, lambda i: (0,) * a.ndim)
    widths = (RW_PAD, HG_COLS, SS_PAD, LR_COLS)
    return pl.pallas_call(
        _inproj_kernel,
        grid=(n_tiles,),
        in_specs=[
            pl.BlockSpec((TOK_TILE, D_MODEL), lambda i: (i, 0)),
            pl.BlockSpec((None, MOD_CH, D_MODEL), lambda i: (seq_of(i), 0, 0)),
            full(g), full(wrw), full(whg), full(wss), full(wlr),
        ],
        out_specs=[pl.BlockSpec((TOK_TILE, w), lambda i: (i, 0)) for w in widths],
        out_shape=[jax.ShapeDtypeStruct((n_tok, w), F32) for w in widths],
        compiler_params=_cparams(("arbitrary",)),
        name="in_proj",
    )(x, mod_l, g, wrw, whg, wss, wlr)


def _outmlp_kernel(n_ctx_tiles, x_ref, c0, c1, c2, c3, l0, l1, l2, l3, mod_ref, g_ref,
                   wout, w1, w2, o_ref):
    is_ctx = pl.program_id(0) < n_ctx_tiles
    mix = jnp.concatenate(
        [jnp.where(is_ctx, c[...], l[...]) for c, l in ((c0, l0), (c1, l1), (c2, l2), (c3, l3))],
        axis=-1).astype(BF16)
    m = mod_ref[...]
    g = g_ref[...]
    x = x_ref[...]
    t = jnp.dot(mix, wout[...], preferred_element_type=F32)
    x1 = x + m[2:3, :] * _rms(t, g[1:2, :])
    h2 = (_rms(x1, g[2:3, :]) * (1.0 + m[4:5, :]) + m[3:4, :]).astype(BF16)
    acc = jnp.zeros_like(x)
    for f in range(D_FF // D_MODEL):
        u = jnp.dot(h2, w1[:, f * D_MODEL:(f + 1) * D_MODEL], preferred_element_type=F32)
        u = jnp.square(jnp.maximum(u, 0.0)).astype(BF16)
        acc = acc + jnp.dot(u, w2[f * D_MODEL:(f + 1) * D_MODEL, :], preferred_element_type=F32)
    o_ref[...] = x1 + m[5:6, :] * _rms(acc, g[3:4, :])


def _outmlp_call(x, ys_ctx, ys_lat, mod_l, g4, wout, w1, w2, n_ctx_tiles, lat_tiles_per_seq):
    n_tok = x.shape[0]
    n_tiles = n_tok // TOK_TILE
    n_lat_tiles = n_tiles - n_ctx_tiles

    def seq_of(i):
        return jnp.where(i < n_ctx_tiles, 0, 1 + (i - n_ctx_tiles) // lat_tiles_per_seq)

    ctx_spec = pl.BlockSpec((TOK_TILE, GROUP_W), lambda i: (jnp.minimum(i, n_ctx_tiles - 1), 0))
    lat_spec = pl.BlockSpec((TOK_TILE, GROUP_W),
                            lambda i: (jnp.clip(i - n_ctx_tiles, 0, n_lat_tiles - 1), 0))
    const = lambda a: pl.BlockSpec(a.shape, lambda i: (0,) * a.ndim, pipeline_mode=pl.Buffered(1))
    return pl.pallas_call(
        functools.partial(_outmlp_kernel, n_ctx_tiles),
        grid=(n_tiles,),
        in_specs=[pl.BlockSpec((TOK_TILE, D_MODEL), lambda i: (i, 0))]
        + [ctx_spec] * 4 + [lat_spec] * 4
        + [pl.BlockSpec((None, MOD_CH, D_MODEL), lambda i: (seq_of(i), 0, 0)),
           const(g4), const(wout), const(w1), const(w2)],
        out_specs=pl.BlockSpec((TOK_TILE, D_MODEL), lambda i: (i, 0)),
        out_shape=jax.ShapeDtypeStruct((n_tok, D_MODEL), F32),
        compiler_params=_cparams(("arbitrary",)),
        name="out_mlp",
    )(x, *ys_ctx, *ys_lat, mod_l, g4, wout, w1, w2)


def _lru_kernel(T, p_ref, cw_ref, cb_ref, wcat_ref, bcat_ref, lam_ref, h0_ref, y_ref, fin_ref,
                ac_ref, bc_ref, h_ref):
    ybr = p_ref[:, 0:GROUP_W]
    xbr = p_ref[:, GROUP_W:2 * GROUP_W]
    xc = _conv_rows(xbr, cw_ref, cb_ref)
    gates = _sigmoid(_mm(xc, wcat_ref[...]) + bcat_ref[...])
    sp = _softplus(-lam_ref[...])
    rows = _row_iota((T, GROUP_W)) % SCAN_BLOCK
    n_blk = T // SCAN_BLOCK
    total = jnp.zeros((T, GROUP_W), F32)
    for d in range(2):
        r = gates[:, (2 * d) * GROUP_W:(2 * d + 1) * GROUP_W]
        ig = gates[:, (2 * d + 1) * GROUP_W:(2 * d + 2) * GROUP_W]
        log_a = (-LRU_C) * r * sp[d:d + 1, :]
        a = jnp.exp(log_a)
        b = jnp.sqrt(jnp.maximum(1.0 - a * a, 0.0)) * (ig * xc)
        k = 1
        while k < SCAN_BLOCK:
            if d == 0:
                ok = rows >= k
                a_s, b_s = _shift_rows(a, -k), _shift_rows(b, -k)
            else:
                ok = rows < SCAN_BLOCK - k
                a_s, b_s = _shift_rows(a, k), _shift_rows(b, k)
            b = jnp.where(ok, a * b_s + b, b)
            a = jnp.where(ok, a * a_s, a)
            k *= 2
        ac_ref[...] = a
        bc_ref[...] = b
        edge = SCAN_BLOCK - 1 if d == 0 else 0

        def body(j, carry, d=d, edge=edge):
            jb = j if d == 0 else n_blk - 1 - j
            r0 = pl.multiple_of(jb * SCAN_BLOCK, SCAN_BLOCK)
            hb = ac_ref[pl.ds(r0, SCAN_BLOCK), :] * carry + bc_ref[pl.ds(r0, SCAN_BLOCK), :]
            h_ref[pl.ds(r0, SCAN_BLOCK), :] = hb
            return jnp.broadcast_to(hb[edge:edge + 1, :], (SCAN_BLOCK, GROUP_W))

        carry0 = jnp.broadcast_to(h0_ref[d:d + 1, :], (SCAN_BLOCK, GROUP_W))
        last = lax.fori_loop(0, n_blk, body, carry0)
        fin_ref[d:d + 1, :] = last[0:1, :]
        total = total + h_ref[...]
    y_ref[...] = total * _gelu_tanh(ybr)


def _lru_call(p, T, n_seq, blk0, cw, cb, wcat, bcat, lam, h0):
    const = lambda a: pl.BlockSpec(a.shape, lambda i: (0,) * a.ndim)
    return pl.pallas_call(
        functools.partial(_lru_kernel, T),
        grid=(n_seq,),
        in_specs=[pl.BlockSpec((T, LR_COLS), lambda i: (blk0 + i, 0)),
                  const(cw), const(cb), const(wcat), const(bcat), const(lam),
                  pl.BlockSpec((None, 2, GROUP_W), lambda i: (i, 0, 0))],
        out_specs=[pl.BlockSpec((T, GROUP_W), lambda i: (i, 0)),
                   pl.BlockSpec((None, 2, GROUP_W), lambda i: (i, 0, 0))],
        out_shape=[jax.ShapeDtypeStruct((n_seq * T, GROUP_W), F32),
                   jax.ShapeDtypeStruct((n_seq, 2, GROUP_W), F32)],
        scratch_shapes=[pltpu.VMEM((T, GROUP_W), F32)] * 3,
        compiler_params=_cparams(("arbitrary",)),
        name="rglru_T%d" % T,
    )(p, cw, cb, wcat, bcat, lam, h0)


def _ssd_kernel(T, has_init, p_ref, cw_ref, cb_ref, dtb_ref, aneg_ref, dexp_ref, nrm_ref, s0_ref,
                y_ref, fin_ref):
    Q = SSD_CHUNK
    nq = T // Q
    z = p_ref[:, 0:GROUP_W]
    xbc = _silu(_conv_rows(p_ref[:, GROUP_W:4 * GROUP_W], cw_ref, cb_ref))
    xs = xbc[:, 0:GROUP_W]
    bm = xbc[:, GROUP_W:2 * GROUP_W]
    cm = xbc[:, 2 * GROUP_W:3 * GROUP_W]
    dt = _softplus(p_ref[:, 4 * GROUP_W:4 * GROUP_W + PAIR_W] + dtb_ref[...])
    da = dt * aneg_ref[...]
    acf = _seg_cumsum(da, Q, rev=False)
    acb = _seg_cumsum(da, Q, rev=True)
    ti = _row_iota((Q, Q))
    si = _lane_iota((Q, Q))
    low = si <= ti
    upp = si >= ti
    lane_lo = _lane_iota((Q, PAIR_W)) < HEAD_DIM
    row_lo = _row_iota((PAIR_W, SSD_STATE)) < HEAD_DIM

    def col(a, j):
        return a[:, j:j + 1]

    ydiag, ds, dec, ecol = [], [], [], []
    for q in range(nq):
        sl = slice(q * Q, (q + 1) * Q)
        acf_q, acb_q, dt_q = acf[sl], acb[sl], dt[sl]
        acf_t, acb_t, dt_t = acf_q.T, acb_q.T, dt_q.T
        y_q, ds_q, dec_q, ec_q = [], [], [], []
        for pr in range(N_PAIRS):
            ls = slice(pr * PAIR_W, (pr + 1) * PAIR_W)
            bg, cg, xp = bm[sl, ls], cm[sl, ls], xs[sl, ls]
            g = _mm_nt(cg, bg)
            outs = []
            wf_cols, wb_cols, ef_cols, eb_cols, decf, decb = [], [], [], [], [], []
            for hh in range(2):
                h = 2 * pr + hh
                hb_ = N_HEADS + h
                lf = jnp.exp(jnp.minimum(col(acf_q, h) - acf_t[h:h + 1, :], 0.0)) * dt_t[h:h + 1, :]
                lb = jnp.exp(jnp.minimum(col(acb_q, hb_) - acb_t[hb_:hb_ + 1, :], 0.0)) \
                    * dt_t[hb_:hb_ + 1, :]
                mh = g * (jnp.where(low, lf, 0.0) + jnp.where(upp, lb, 0.0))
                outs.append(_mm(mh, xp))
                af_last = acf_q[Q - 1:Q, h:h + 1]
                ab_last = acb_q[0:1, hb_:hb_ + 1]
                wf_cols.append(jnp.exp(af_last - col(acf_q, h)) * col(dt_q, h))
                wb_cols.append(jnp.exp(ab_last - col(acb_q, hb_)) * col(dt_q, hb_))
                ef_cols.append(jnp.exp(col(acf_q, h)))
                eb_cols.append(jnp.exp(col(acb_q, hb_)))
                decf.append(jnp.exp(af_last))
                decb.append(jnp.exp(ab_last))
            y_q.append(jnp.where(lane_lo, outs[0], outs[1]))
            wf = jnp.where(lane_lo, wf_cols[0], wf_cols[1])
            wb = jnp.where(lane_lo, wb_cols[0], wb_cols[1])
            ds_q.append((_mm_tn(xp * wf, bg), _mm_tn(xp * wb, bg)))
            dec_q.append((jnp.where(row_lo, decf[0], decf[1]), jnp.where(row_lo, decb[0], decb[1])))
            ec_q.append((jnp.where(lane_lo, ef_cols[0], ef_cols[1]),
                         jnp.where(lane_lo, eb_cols[0], eb_cols[1])))
        ydiag.append(y_q)
        ds.append(ds_q)
        dec.append(dec_q)
        ecol.append(ec_q)

    prev = [[[None, None] for _ in range(N_PAIRS)] for _ in range(nq)]
    for pr in range(N_PAIRS):
        for d in range(2):
            s = s0_ref[d, pr] if has_init else jnp.zeros((PAIR_W, SSD_STATE), F32)
            order = range(nq) if d == 0 else range(nq - 1, -1, -1)
            for q in order:
                prev[q][pr][d] = s
                s = dec[q][pr][d] * s + ds[q][pr][d]
            fin_ref[d, pr] = s

    ys = []
    for q in range(nq):
        sl = slice(q * Q, (q + 1) * Q)
        parts = []
        for pr in range(N_PAIRS):
            ls = slice(pr * PAIR_W, (pr + 1) * PAIR_W)
            yp = ydiag[q][pr]
            if has_init or nq > 1:
                cg = cm[sl, ls]
                for d in range(2):
                    yp = yp + _mm_nt(cg, prev[q][pr][d]) * ecol[q][pr][d]
            parts.append(yp)
        ys.append(jnp.concatenate(parts, axis=-1))
    y = jnp.concatenate(ys, axis=0) if nq > 1 else ys[0]
    y = y + dexp_ref[...] * xs
    y_ref[...] = _rms(y * _silu(z), nrm_ref[...])


def _ssd_call(p, T, n_seq, blk0, has_init, cw, cb, dtb, aneg, dexp, nrm, s0):
    const = lambda a: pl.BlockSpec(a.shape, lambda i: (0,) * a.ndim)
    st_spec = pl.BlockSpec((None, 2, N_PAIRS, PAIR_W, SSD_STATE), lambda i: (i, 0, 0, 0, 0))
    return pl.pallas_call(
        functools.partial(_ssd_kernel, T, has_init),
        grid=(n_seq,),
        in_specs=[pl.BlockSpec((T, SS_PAD), lambda i: (blk0 + i, 0)),
                  const(cw), const(cb), const(dtb), const(aneg), const(dexp), const(nrm), st_spec],
        out_specs=[pl.BlockSpec((T, GROUP_W), lambda i: (i, 0)), st_spec],
        out_shape=[jax.ShapeDtypeStruct((n_seq * T, GROUP_W), F32),
                   jax.ShapeDtypeStruct((n_seq, 2, N_PAIRS, PAIR_W, SSD_STATE), F32)],
        compiler_params=_cparams(("arbitrary",)),
        name="ssd_T%d" % T,
    )(p, cw, cb, dtb, aneg, dexp, nrm, s0)


def _hgrn_kernel(T, p_ref, la_ref, l1_ref, lbd_ref, nrm_ref, bd_ref, s0_ref, y_ref, fin_ref,
                 q_sc, v_sc, k_sc, b_sc, o_sc):
    NB = HG_BLOCK
    n_blk = T // NB
    q_sc[...] = _silu(p_ref[:, 0:GROUP_W])
    v_sc[...] = p_ref[:, GROUP_W:2 * GROUP_W]
    for d in range(2):
        x = p_ref[:, (2 + d) * GROUP_W:(3 + d) * GROUP_W]
        e1 = jnp.exp(-jnp.abs(x))
        lsig = jnp.minimum(x, 0.0) - jnp.log(1.0 + e1)
        a_ = la_ref[d:d + 1, :] + jnp.zeros_like(x)
        b_ = l1_ref[d:d + 1, :] + lsig
        logf = jnp.maximum(a_, b_) + jnp.log(1.0 + jnp.exp(-jnp.abs(a_ - b_)))
        k_sc[d] = jnp.exp(l1_ref[d:d + 1, :]) * (jnp.where(x >= 0.0, e1, 1.0) / (1.0 + e1)) \
            - lbd_ref[d:d + 1, :]
        b_sc[d] = _seg_cumsum(logf, NB, rev=(d == 1))
    bd = bd_ref[...]
    rows16 = _row_iota((NB, GROUP_W))
    bdm = (_row_iota((PAIR_W, PAIR_W)) < HEAD_DIM) == (_lane_iota((PAIR_W, PAIR_W)) < HEAD_DIM)
    chains = [(d, pr) for d in range(2) for pr in range(N_PAIRS)]

    def body(j, states):
        new_states = []
        for d in range(2):
            jb = j if d == 0 else n_blk - 1 - j
            r0 = pl.multiple_of(jb * NB, NB)
            qb = q_sc[pl.ds(r0, NB), :]
            vb = v_sc[pl.ds(r0, NB), :]
            kb = k_sc[d, pl.ds(r0, NB), :]
            bb = b_sc[d, pl.ds(r0, NB), :]
            parts = []
            for s in range(NB):
                es = jnp.exp(bb - bb[s:s + 1, :])
                keep = (rows16 >= s) if d == 0 else (rows16 <= s)
                parts.append(jnp.where(keep, qb * es * kb[s:s + 1, :], 0.0))
            att = _mm(jnp.concatenate(parts, axis=0), bd)
            o = jnp.zeros((NB, GROUP_W), F32)
            for s in range(NB):
                o = o + att[s * NB:(s + 1) * NB, :] * vb[s:s + 1, :]
            bl = bb[NB - 1:NB, :] if d == 0 else bb[0:1, :]
            qd = qb * jnp.exp(bb)
            kh = kb * jnp.exp(bl - bb)
            dec = jnp.exp(bl)
            inter = []
            for pr in range(N_PAIRS):
                ls = slice(pr * PAIR_W, (pr + 1) * PAIR_W)
                st = states[d * N_PAIRS + pr]
                inter.append(_mm_nt(qd[:, ls], st))
                new_states.append(st * dec[:, ls] + jnp.where(bdm, _mm_tn(vb[:, ls], kh[:, ls]), 0.0))
            o_sc[d, pl.ds(r0, NB), :] = o + jnp.concatenate(inter, axis=-1)
        return tuple(new_states)

    states = lax.fori_loop(0, n_blk, body, tuple(s0_ref[d, pr] for d, pr in chains), unroll=4)
    for (d, pr), s_ in zip(chains, states):
        fin_ref[d, pr] = s_
    g = p_ref[:, 4 * GROUP_W:5 * GROUP_W]
    y_ref[...] = _rms(o_sc[0] + o_sc[1], nrm_ref[...]) * _silu(g)


def _hgrn_call(p, T, n_seq, blk0, la, l1, lbd, nrm, bd, s0):
    const = lambda a: pl.BlockSpec(a.shape, lambda i: (0,) * a.ndim)
    st_spec = pl.BlockSpec((None, 2, N_PAIRS, PAIR_W, PAIR_W), lambda i: (i, 0, 0, 0, 0))
    tw = pltpu.VMEM((T, GROUP_W), F32)
    tw2 = pltpu.VMEM((2, T, GROUP_W), F32)
    return pl.pallas_call(
        functools.partial(_hgrn_kernel, T),
        grid=(n_seq,),
        in_specs=[pl.BlockSpec((T, HG_COLS), lambda i: (blk0 + i, 0)),
                  const(la), const(l1), const(lbd), const(nrm), const(bd), st_spec],
        out_specs=[pl.BlockSpec((T, GROUP_W), lambda i: (i, 0)), st_spec],
        out_shape=[jax.ShapeDtypeStruct((n_seq * T, GROUP_W), F32),
                   jax.ShapeDtypeStruct((n_seq, 2, N_PAIRS, PAIR_W, PAIR_W), F32)],
        scratch_shapes=[tw, tw, tw2, tw2, tw2],
        compiler_params=_cparams(("arbitrary",)),
        name="hgrn2_T%d" % T,
    )(p, la, l1, lbd, nrm, bd, s0)


def _rwkv_kernel(T, nb, lat, p_ref, mu_ref, sm_ref, w0_ref, a0_ref, wl_ref, kk_ref, ka_ref, rk_ref,
                 lnw_ref, lnb_ref, bd_ref, s0_ref, y_ref, fin_ref,
                 xa_sc, xr_sc, yb_sc, yk_sc, c_sc, v_sc, yo_sc):
    C = RW_CHUNK
    n_chunk = T // C
    P = p_ref[...]
    rowi = _row_iota(P.shape) % T
    if lat:
        gcol = rowi % GRID_W
        shifts = [(-1, gcol >= 1), (1, gcol <= GRID_W - 2), (-GRID_W, rowi >= GRID_W),
                  (GRID_W, rowi < T - GRID_W)]
    else:
        shifts = [(-1, rowi >= 1), (1, rowi <= T - 2)]
    sh = jnp.zeros_like(P)
    for j, (dlt, valid) in enumerate(shifts):
        sh = jnp.where(valid & (sm_ref[j:j + 1, :] > 0.5), _shift_rows(P, dlt), sh)
    P = P + (sh - P) * mu_ref[...]
    r = P[:, 0:GROUP_W]
    k = P[:, GROUP_W:2 * GROUP_W]
    v = P[:, 2 * GROUP_W:3 * GROUP_W]
    lo = P[:, 3 * GROUP_W:3 * GROUP_W + PAIR_W]
    ln = _lane_iota(lo.shape)
    act = jnp.where(ln < 32, jnp.tanh(lo), jnp.where(ln < 64, lo, jnp.where(ln < 96, _sigmoid(lo), 0.0)))
    lora = _mm(act, wl_ref[...])
    gate = lora[:, 4 * GROUP_W:5 * GROUP_W]
    bd = bd_ref[...]
    kkv = k * kk_ref[...]
    kkn = kkv / jnp.maximum(jnp.sqrt(_mm_split(kkv * kkv, bd)), 1e-12)
    v_sc[...] = v
    ksum = jnp.zeros_like(k)
    for d in range(2):
        w_lora = lora[:, d * GROUP_W:(d + 1) * GROUP_W]
        a_lora = lora[:, (2 + d) * GROUP_W:(3 + d) * GROUP_W]
        w_log = -_softplus(-(w0_ref[d:d + 1, :] + w_lora)) - 0.5
        logw = -jnp.exp(w_log)
        a_sig = _sigmoid(a0_ref[d:d + 1, :] + a_lora)
        k_d = k * (1.0 + (a_sig - 1.0) * ka_ref[...])
        ksum = ksum + k_d
        c = _seg_cumsum(logw, C, rev=(d == 1))
        einv = jnp.exp(-c)
        xa_sc[d] = (-kkn) * jnp.exp(c - logw)
        xr_sc[d] = r * jnp.exp(c)
        yb_sc[d] = (kkn * a_sig) * einv
        yk_sc[d] = k_d * einv
        c_sc[d] = c

    lane_lo = _lane_iota((C, PAIR_W)) < HEAD_DIM
    tt = _row_iota((PAIR_W, PAIR_W)) % C
    ss = _lane_iota((PAIR_W, PAIR_W)) % C
    tt2 = _row_iota((PAIR_W, 2 * PAIR_W)) % C
    ss2 = _lane_iota((PAIR_W, 2 * PAIR_W)) % C
    n_fac = C.bit_length() - 1

    def stack2(x):
        return jnp.concatenate([jnp.where(lane_lo, x, 0.0), jnp.where(lane_lo, 0.0, x)], axis=0)

    chains = [(sq, d, pr) for sq in range(nb) for d in range(2) for pr in range(N_PAIRS)]
    each = lambda f, *cols: [f(*a) for a in zip(*cols)]

    def body(j, states):
        lhs, rhs_t, vst, pc, strict, incl2, dst = [], [], [], [], [], [], []
        for sq, d, pr in chains:
            jc = j if d == 0 else n_chunk - 1 - j
            r0 = pl.multiple_of(sq * T + jc * C, C)
            ls = slice(pr * PAIR_W, (pr + 1) * PAIR_W)
            at = xa_sc[d, pl.ds(r0, C), ls]
            rt = xr_sc[d, pl.ds(r0, C), ls]
            bt = yb_sc[d, pl.ds(r0, C), ls]
            kt = yk_sc[d, pl.ds(r0, C), ls]
            cc = c_sc[d, pl.ds(r0, C), ls]
            pc.append(jnp.exp(cc[C - 1:C, :] if d == 0 else cc[0:1, :]))
            lhs.append(jnp.concatenate([stack2(at), stack2(rt)], axis=0))
            rhs_t.append(jnp.concatenate([stack2(bt), stack2(kt)], axis=0))
            vst.append(stack2(v_sc[pl.ds(r0, C), ls]))
            strict.append((ss < tt) if d == 0 else (ss > tt))
            incl2.append((ss2 <= tt2) if d == 0 else (ss2 >= tt2))
            dst.append((d, r0, ls))
        gg = each(_mm_nt, lhs, rhs_t)
        uu = each(_mm_nt, lhs, states)
        qb = each(lambda g, m: jnp.where(m, g[0:PAIR_W, 0:PAIR_W], 0.0).astype(BF16), gg, strict)
        mak = each(lambda g, m: jnp.where(m, g[0:PAIR_W, PAIR_W:2 * PAIR_W], 0.0), gg, strict)
        ncat = each(lambda g, m: jnp.where(m, g[PAIR_W:2 * PAIR_W, :], 0.0), gg, incl2)
        xs = each(lambda u, m, v: u[0:PAIR_W, :] + _mm(m, v), uu, mak, vst)
        for i in range(n_fac):
            xh = each(lambda x: x.astype(BF16), xs)
            xl = each(lambda x, h: (x - h.astype(F32)).astype(BF16), xs, xh)
            corr = each(lambda q, l: jnp.dot(q, l, preferred_element_type=F32), qb, xl)
            if i < n_fac - 1:
                zz = each(lambda q, h: jnp.dot(q, jnp.concatenate([h, q], axis=1),
                                               preferred_element_type=F32), qb, xh)
                xs = each(lambda x, z, c_: x + z[:, 0:PAIR_W] + c_, xs, zz, corr)
                qb = each(lambda z: z[:, PAIR_W:2 * PAIR_W].astype(BF16), zz)
            else:
                xs = each(lambda x, q, h, c_: x + jnp.dot(q, h, preferred_element_type=F32) + c_,
                          xs, qb, xh, corr)
        sv = each(lambda x, v: jnp.concatenate([x, v], axis=0), xs, vst)
        yst = each(lambda u, n, s_: u[PAIR_W:2 * PAIR_W, :] + _mm(n, s_), uu, ncat, sv)
        for (d, r0, ls), y_ in zip(dst, yst):
            yo_sc[d, pl.ds(r0, C), ls] = y_[0:C, :] + y_[C:2 * C, :]
        return tuple(each(lambda s_, p, v, r_: s_ * p + _mm_tn(v, r_ * p), states, pc, sv, rhs_t))

    states0 = tuple(s0_ref[sq, d, pr] for sq, d, pr in chains)
    states = lax.fori_loop(0, n_chunk, body, states0)
    for (sq, d, pr), s_ in zip(chains, states):
        fin_ref[sq, d, pr] = s_
    y = yo_sc[0] + yo_sc[1]
    inv_n = 1.0 / HEAD_DIM
    mu = _mm_split(y, bd) * inv_n
    yc = y - mu
    var = _mm_split(yc * yc, bd) * inv_n
    yn = yc * lax.rsqrt(var + RW_GN_EPS) * lnw_ref[...] + lnb_ref[...]
    bonus = _mm_split(r * ksum * rk_ref[...], bd) * v
    y_ref[...] = (yn + bonus) * gate


def _rwkv_call(p, T, n_seq, blk0, nb, lat, mu, sm, w0, a0, wl, kk, ka, rk, lnw, lnb, bd, s0):
    assert n_seq % nb == 0 and blk0 % nb == 0
    const = lambda a: pl.BlockSpec(a.shape, lambda i: (0,) * a.ndim)
    st_spec = pl.BlockSpec((nb, 2, N_PAIRS, PAIR_W, PAIR_W), lambda i: (i, 0, 0, 0, 0))
    tw = pltpu.VMEM((nb * T, GROUP_W), F32)
    tw2 = pltpu.VMEM((2, nb * T, GROUP_W), F32)
    return pl.pallas_call(
        functools.partial(_rwkv_kernel, T, nb, lat),
        grid=(n_seq // nb,),
        in_specs=[pl.BlockSpec((nb * T, RW_PAD), lambda i: (blk0 // nb + i, 0)),
                  const(mu), const(sm), const(w0), const(a0), const(wl), const(kk), const(ka),
                  const(rk), const(lnw), const(lnb), const(bd), st_spec],
        out_specs=[pl.BlockSpec((nb * T, GROUP_W), lambda i: (i, 0)), st_spec],
        out_shape=[jax.ShapeDtypeStruct((n_seq * T, GROUP_W), F32),
                   jax.ShapeDtypeStruct((n_seq, 2, N_PAIRS, PAIR_W, PAIR_W), F32)],
        scratch_shapes=[tw2, tw2, tw2, tw2, tw2, tw, tw2],
        compiler_params=_cparams(("arbitrary",)),
        name="rwkv7_T%d" % T,
    )(p, mu, sm, w0, a0, wl, kk, ka, rk, lnw, lnb, bd, s0)


def _pair_blockdiag(s):
    b = s.shape[0]
    s = s.reshape(b, 2, N_PAIRS, 2, HEAD_DIM, HEAD_DIM)
    z = jnp.zeros_like(s[:, :, :, 0])
    top = jnp.concatenate([s[:, :, :, 0], z], axis=-1)
    bot = jnp.concatenate([z, s[:, :, :, 1]], axis=-1)
    return jnp.concatenate([top, bot], axis=-2)


def _pair_diag_blocks(s):
    b = s.shape[0]
    h0 = s[:, :, :, 0:HEAD_DIM, 0:HEAD_DIM]
    h1 = s[:, :, :, HEAD_DIM:, HEAD_DIM:]
    return jnp.stack([h0, h1], axis=3).reshape(b, 2, N_HEADS, HEAD_DIM, HEAD_DIM)


def _head_blockdiag(w):
    out = jnp.zeros((GROUP_W, GROUP_W), w.dtype)
    for h in range(N_HEADS):
        out = out.at[h * HEAD_DIM:(h + 1) * HEAD_DIM, h * HEAD_DIM:(h + 1) * HEAD_DIM].set(w[h])
    return out


def _pad_cols(a, n):
    return jnp.pad(a, ((0, 0),) * (a.ndim - 1) + ((0, n - a.shape[-1]),))


def kernel(x_prompt, x_sample, state_rwkv, state_hgrn, state_ssd, state_lru, c, c_ctx, mod_w, mod_b, norm_g, w_in, w_out, rw_mu, rw_w0, rw_w2, rw_a0, rw_a2, rw_g2, rw_kk, rw_ka, rw_rk, rw_ln_w, rw_ln_b, hg_lb, hg_norm, ss_conv_w, ss_conv_b, ss_dt_bias, ss_A_log, ss_D, ss_norm, lr_conv_w, lr_conv_b, lr_wa, lr_ba, lr_wx, lr_bx, lr_lam, mlp_w1, mlp_w2):
    bp, t_ctx, _ = x_prompt.shape
    bl, t_lat, _ = x_sample.shape
    n_ctx_tok = bp * t_ctx
    n_ctx_tiles = n_ctx_tok // TOK_TILE
    lat_tiles_per_seq = t_lat // TOK_TILE
    lat_blk0 = n_ctx_tok // t_lat
    assert n_ctx_tok % t_lat == 0 and t_lat % TOK_TILE == 0 and n_ctx_tok % TOK_TILE == 0
    assert t_lat // GRID_W * GRID_W == t_lat and 1 + bl <= 8

    x = jnp.concatenate([x_prompt.reshape(n_ctx_tok, D_MODEL), x_sample.reshape(bl * t_lat, D_MODEL)], axis=0)
    cond8 = jnp.zeros((8, D_MODEL), F32).at[0].set(c_ctx).at[1:1 + bl].set(c)
    mod = _mod_call(cond8, mod_w, mod_b).reshape(DEPTH, 8, MOD_CH, D_MODEL)

    lb_soft = jax.nn.softmax(hg_lb.astype(F32), axis=0)
    lower = jnp.cumsum(lb_soft, axis=0) - lb_soft[0]
    lb_floor = jnp.maximum(lower, LB_FLOOR)
    hg_la = jnp.log(lb_floor)
    hg_l1 = jnp.log1p(-lower)
    hg_lbd = lb_floor - lower

    lane = jnp.arange(GROUP_W)
    bd_ones = (lane[:, None] // HEAD_DIM == lane[None, :] // HEAD_DIM).astype(BF16)
    col = jnp.arange(RW_PAD)
    quarter = col // RW_SHIFT_SPLIT
    real = col < RW_COLS
    sm_lat = jnp.stack([(quarter == i) & real for i in range(4)]).astype(F32)
    sm_ctx = jnp.stack([(quarter < 2) & real, (quarter >= 2) & real,
                        jnp.zeros_like(real), jnp.zeros_like(real)]).astype(F32)

    zeros_pair = jnp.zeros((bp, 2, N_PAIRS, PAIR_W, PAIR_W), F32)
    zeros_ssd = jnp.zeros((bp, 2, N_PAIRS, PAIR_W, SSD_STATE), F32)
    zeros_lru = jnp.zeros((bp, 2, GROUP_W), F32)

    new_rw, new_hg, new_ss, new_lr = [], [], [], []
    for l in range(DEPTH):
        o1 = RW_COLS
        o2 = o1 + HG_COLS
        o3 = o2 + SS_COLS
        wl_in = w_in[l].astype(BF16)
        w_rw = _pad_cols(wl_in[:, :o1], RW_PAD)
        w_hg = wl_in[:, o1:o2]
        w_ss = _pad_cols(wl_in[:, o2:o3], SS_PAD)
        w_lr = wl_in[:, o3:]
        g4 = norm_g[l]
        p_rw, p_hg, p_ss, p_lr = _inproj_call(x, mod[l], g4[0:1], w_rw, w_hg, w_ss, w_lr,
                                              n_ctx_tiles, lat_tiles_per_seq)

        wl = jnp.zeros((PAIR_W, 5 * GROUP_W), F32)
        wl = wl.at[0:16, 0:GROUP_W].set(rw_w2[l, 0]).at[16:32, GROUP_W:2 * GROUP_W].set(rw_w2[l, 1])
        wl = wl.at[32:48, 2 * GROUP_W:3 * GROUP_W].set(rw_a2[l, 0])
        wl = wl.at[48:64, 3 * GROUP_W:4 * GROUP_W].set(rw_a2[l, 1])
        wl = wl.at[64:96, 4 * GROUP_W:5 * GROUP_W].set(rw_g2[l]).astype(BF16)
        rw_args = (_pad_cols(rw_mu[l][None, :], RW_PAD),)
        rw_tail = (rw_w0[l], rw_a0[l], wl, rw_kk[l][None, :], rw_ka[l][None, :],
                   rw_rk[l].reshape(1, GROUP_W), rw_ln_w[l][None, :], rw_ln_b[l][None, :], bd_ones)
        y_rw_c, f_rw = _rwkv_call(p_rw, t_ctx, bp, 0, RW_CTX_SEQS, False, *rw_args, sm_ctx, *rw_tail,
                                  zeros_pair)
        y_rw_l, _ = _rwkv_call(p_rw, t_lat, bl, lat_blk0, RW_LAT_SEQS, True, *rw_args, sm_lat,
                               *rw_tail, _pair_blockdiag(state_rwkv[:, l]))

        hg_args = (hg_la[l], hg_l1[l], hg_lbd[l], hg_norm[l][None, :], bd_ones)
        y_hg_c, f_hg = _hgrn_call(p_hg, t_ctx, bp, 0, *hg_args, zeros_pair)
        y_hg_l, _ = _hgrn_call(p_hg, t_lat, bl, lat_blk0, *hg_args,
                               _pair_blockdiag(jnp.swapaxes(state_hgrn[:, l], -1, -2)))

        dtb = _pad_cols(ss_dt_bias[l].reshape(1, 2 * N_HEADS), PAIR_W)
        aneg = _pad_cols(-jnp.exp(ss_A_log[l].astype(F32)).reshape(1, 2 * N_HEADS), PAIR_W)
        dexp = jnp.repeat(ss_D[l], HEAD_DIM)[None, :]
        ss_args = (ss_conv_w[l], ss_conv_b[l][None, :], dtb, aneg, dexp, ss_norm[l][None, :])
        y_ss_c, f_ss = _ssd_call(p_ss, t_ctx, bp, 0, False, *ss_args, zeros_ssd)
        y_ss_l, _ = _ssd_call(p_ss, t_lat, bl, lat_blk0, True, *ss_args,
                              state_ssd[:, l].reshape(bl, 2, N_PAIRS, PAIR_W, SSD_STATE))

        wcat = jnp.concatenate([_head_blockdiag(lr_wa[l, 0]), _head_blockdiag(lr_wx[l, 0]),
                                _head_blockdiag(lr_wa[l, 1]), _head_blockdiag(lr_wx[l, 1])],
                               axis=1).astype(BF16)
        bcat = jnp.concatenate([lr_ba[l, 0], lr_bx[l, 0], lr_ba[l, 1], lr_bx[l, 1]])[None, :]
        lr_args = (lr_conv_w[l], lr_conv_b[l][None, :], wcat, bcat, lr_lam[l])
        y_lr_c, f_lr = _lru_call(p_lr, t_ctx, bp, 0, *lr_args, zeros_lru)
        y_lr_l, _ = _lru_call(p_lr, t_lat, bl, lat_blk0, *lr_args, state_lru[:, l])

        x = _outmlp_call(x, (y_rw_c, y_hg_c, y_ss_c, y_lr_c), (y_rw_l, y_hg_l, y_ss_l, y_lr_l),
                         mod[l], g4, w_out[l].astype(BF16), mlp_w1[l].astype(BF16),
                         mlp_w2[l].astype(BF16), n_ctx_tiles, lat_tiles_per_seq)

        new_rw.append(_pair_diag_blocks(f_rw))
        new_hg.append(jnp.swapaxes(_pair_diag_blocks(f_hg), -1, -2))
        new_ss.append(f_ss.reshape(bp, 2, N_HEADS, HEAD_DIM, SSD_STATE))
        new_lr.append(f_lr)

    y_prompt = x[:n_ctx_tok].reshape(bp, t_ctx, D_MODEL)
    y_sample = x[n_ctx_tok:].reshape(bl, t_lat, D_MODEL)
    return (y_prompt, y_sample, jnp.stack(new_rw, axis=1), jnp.stack(new_hg, axis=1),
            jnp.stack(new_ss, axis=1), jnp.stack(new_lr, axis=1))
```

```python
import functools

import jax
import jax.numpy as jnp
from jax import lax
from jax.experimental import pallas as pl
from jax.experimental.pallas import tpu as pltpu

F32 = jnp.float32
BF16 = jnp.bfloat16

D_MODEL = 1024
DEPTH = 4
MOD_CH = 6
GROUP_W = 256
HEAD_DIM = 64
N_HEADS = 4
N_PAIRS = 2
PAIR_W = 128
D_FF = 4096
GRID_W = 64
NORM_EPS = 1e-6
LB_FLOOR = 1e-30
RW_GN_EPS = 64e-5
LRU_C = 8.0
SSD_STATE = 128
CONV_TAPS = (-2, -1, 0, 1)

RW_COLS = 864
RW_PAD = 896
RW_SHIFT_SPLIT = RW_COLS // 4
RW_LORA_W = 16
RW_LORA_G = 32
HG_COLS = 1280
SS_COLS = 1032
SS_PAD = 1152
LR_COLS = 512
RW_CHUNK = 64
RW_CTX_SEQS = 4
RW_LAT_SEQS = 1
HG_BLOCK = 16
HG_UNROLL = 4
SSD_CHUNK = 256
SCAN_BLOCK = 8

TOK_TILE = 512
VMEM_LIMIT = 56 * 1024 * 1024


def _cparams(sem):
    return pltpu.CompilerParams(dimension_semantics=sem, vmem_limit_bytes=VMEM_LIMIT)


def _layer_spec(a, l, **kw):
    return pl.BlockSpec((None,) + a.shape[1:], lambda *_: (l,) + (0,) * (a.ndim - 1), **kw)


def _const_spec(a):
    return pl.BlockSpec(a.shape, lambda *_: (0,) * a.ndim)


def _mm(a, b):
    return jnp.dot(a.astype(BF16), b.astype(BF16), preferred_element_type=F32)


def _mm_nt(a, b):
    return lax.dot_general(a.astype(BF16), b.astype(BF16), (((1,), (1,)), ((), ())),
                           preferred_element_type=F32)


def _mm_tn(a, b):
    return lax.dot_general(a.astype(BF16), b.astype(BF16), (((0,), (0,)), ((), ())),
                           preferred_element_type=F32)


def _mm_split(a, b_bf16):
    hi = a.astype(BF16)
    lo = (a - hi.astype(F32)).astype(BF16)
    return (jnp.dot(hi, b_bf16, preferred_element_type=F32)
            + jnp.dot(lo, b_bf16, preferred_element_type=F32))


def _row_iota(shape):
    return lax.broadcasted_iota(jnp.int32, shape, 0)


def _lane_iota(shape):
    return lax.broadcasted_iota(jnp.int32, shape, 1)


def _shift_rows(x, d):
    n = x.shape[0]
    s = (-d) % n
    if s == 0:
        return x
    return pltpu.roll(x, s, 0)


def _seg_cumsum(x, seg, rev):
    r = _row_iota(x.shape) % seg
    k = 1
    while k < seg:
        if rev:
            x = x + jnp.where(r < seg - k, _shift_rows(x, k), 0.0)
        else:
            x = x + jnp.where(r >= k, _shift_rows(x, -k), 0.0)
        k *= 2
    return x


def _sigmoid(x):
    return jax.nn.sigmoid(x)


def _silu(x):
    return x * _sigmoid(x)


def _softplus(x):
    return jnp.maximum(x, 0.0) + jnp.log(1.0 + jnp.exp(-jnp.abs(x)))


def _gelu_tanh(x):
    c = 0.7978845608028654
    return 0.5 * x * (1.0 + jnp.tanh(c * (x + 0.044715 * (x * x * x))))


def _rms(x, g):
    return x * lax.rsqrt(jnp.mean(x * x, axis=-1, keepdims=True) + NORM_EPS) * g


def _conv_rows(x, w_ref, b_ref):
    t = x.shape[0]
    rows = _row_iota(x.shape)
    y = b_ref[...] + jnp.zeros_like(x)
    for j, d in enumerate(CONV_TAPS):
        valid = (rows + d >= 0) & (rows + d < t)
        y = y + jnp.where(valid, _shift_rows(x, d), 0.0) * w_ref[j:j + 1, :]
    return y


def _pair_blockdiag(a, b):
    z = jnp.zeros_like(a)
    return jnp.concatenate([jnp.concatenate([a, z], axis=1), jnp.concatenate([z, b], axis=1)], axis=0)


def _pair_diag(m, hh):
    return m[hh * HEAD_DIM:(hh + 1) * HEAD_DIM, hh * HEAD_DIM:(hh + 1) * HEAD_DIM]


def _mod_kernel(c_ref, w_ref, b_ref, o_ref):
    c = c_ref[...]
    o_ref[...] = _mm(_silu(c), w_ref[...]) + b_ref[...]


def _mod_call(cond8, mod_w, mod_b):
    return pl.pallas_call(
        _mod_kernel,
        grid=(DEPTH, MOD_CH),
        in_specs=[
            pl.BlockSpec((8, D_MODEL), lambda l, j: (0, 0)),
            pl.BlockSpec((None, D_MODEL, D_MODEL), lambda l, j: (l, 0, j)),
            pl.BlockSpec((None, 1, D_MODEL), lambda l, j: (l, 0, j)),
        ],
        out_specs=pl.BlockSpec((None, 8, D_MODEL), lambda l, j: (l, 0, j)),
        out_shape=jax.ShapeDtypeStruct((DEPTH, 8, MOD_CH * D_MODEL), F32),
        compiler_params=_cparams(("arbitrary", "arbitrary")),
        name="modulation",
    )(cond8, mod_w, mod_b.reshape(DEPTH, 1, MOD_CH * D_MODEL))


class _Tiles:
    def __init__(self, n_ctx_tok, n_lat_tok, t_lat):
        assert n_ctx_tok % TOK_TILE == 0 and t_lat % TOK_TILE == 0
        self.n_ctx = n_ctx_tok // TOK_TILE
        self.n_lat = n_lat_tok // TOK_TILE
        self.per_seq = t_lat // TOK_TILE
        self.n = self.n_ctx + self.n_lat

    def ctx_idx(self, i):
        return jnp.minimum(i, self.n_ctx - 1)

    def lat_idx(self, i):
        return jnp.clip(i - self.n_ctx, 0, self.n_lat - 1)

    def seq_of(self, i):
        return jnp.where(i < self.n_ctx, 0, 1 + (i - self.n_ctx) // self.per_seq)


def _inproj_kernel(n_ctx_tiles, xc_ref, xl_ref, mod_ref, g_ref, wrw, whg, wss, wlr,
                   orw, ohg, oss, olr):
    x = jnp.where(pl.program_id(0) < n_ctx_tiles, xc_ref[...], xl_ref[...])
    m = mod_ref[...]
    h = _rms(x, g_ref[0:1, :]) * (1.0 + m[1:2, :]) + m[0:1, :]
    hb = h.astype(BF16)
    orw[...] = jnp.dot(hb, wrw[...], preferred_element_type=F32)
    ohg[...] = jnp.dot(hb, whg[...], preferred_element_type=F32)
    oss[...] = jnp.dot(hb, wss[...], preferred_element_type=F32)
    olr[...] = jnp.dot(hb, wlr[...], preferred_element_type=F32)


def _inproj_call(tiles, l, x_ctx, x_lat, mod, norm_g, ws):
    widths = (RW_PAD, HG_COLS, SS_PAD, LR_COLS)
    n_tok = tiles.n * TOK_TILE
    return pl.pallas_call(
        functools.partial(_inproj_kernel, tiles.n_ctx),
        grid=(tiles.n,),
        in_specs=[
            pl.BlockSpec((TOK_TILE, D_MODEL), lambda i: (tiles.ctx_idx(i), 0)),
            pl.BlockSpec((TOK_TILE, D_MODEL), lambda i: (tiles.lat_idx(i), 0)),
            pl.BlockSpec((None, None, MOD_CH, D_MODEL), lambda i: (l, tiles.seq_of(i), 0, 0)),
            _layer_spec(norm_g, l),
        ] + [_layer_spec(w, l) for w in ws],
        out_specs=[pl.BlockSpec((TOK_TILE, w), lambda i: (i, 0)) for w in widths],
        out_shape=[jax.ShapeDtypeStruct((n_tok, w), F32) for w in widths],
        compiler_params=_cparams(("arbitrary",)),
        name="in_proj",
    )(x_ctx, x_lat, mod, norm_g, *ws)


def _outmlp_kernel(n_ctx_tiles, xc_ref, xl_ref, c0, c1, c2, c3, l0, l1, l2, l3, mod_ref, g_ref,
                   wout, w1, w2, oc_ref, ol_ref):
    is_ctx = pl.program_id(0) < n_ctx_tiles
    mix = jnp.concatenate(
        [jnp.where(is_ctx, c[...], l[...]) for c, l in ((c0, l0), (c1, l1), (c2, l2), (c3, l3))],
        axis=-1).astype(BF16)
    m = mod_ref[...]
    g = g_ref[...]
    x = jnp.where(is_ctx, xc_ref[...], xl_ref[...])
    t = jnp.dot(mix, wout[...], preferred_element_type=F32)
    x1 = x + m[2:3, :] * _rms(t, g[1:2, :])
    h2 = (_rms(x1, g[2:3, :]) * (1.0 + m[4:5, :]) + m[3:4, :]).astype(BF16)
    acc = jnp.zeros_like(x)
    for f in range(D_FF // D_MODEL):
        u = jnp.dot(h2, w1[:, f * D_MODEL:(f + 1) * D_MODEL], preferred_element_type=F32)
        u = jnp.square(jnp.maximum(u, 0.0)).astype(BF16)
        acc = acc + jnp.dot(u, w2[f * D_MODEL:(f + 1) * D_MODEL, :], preferred_element_type=F32)
    res = x1 + m[5:6, :] * _rms(acc, g[3:4, :])

    @pl.when(is_ctx)
    def _():
        oc_ref[...] = res

    @pl.when(jnp.logical_not(is_ctx))
    def _():
        ol_ref[...] = res


def _outmlp_call(tiles, l, x_ctx, x_lat, ys_ctx, ys_lat, mod, norm_g, wout, w1, w2):
    ctx_x = pl.BlockSpec((TOK_TILE, D_MODEL), lambda i: (tiles.ctx_idx(i), 0))
    lat_x = pl.BlockSpec((TOK_TILE, D_MODEL), lambda i: (tiles.lat_idx(i), 0))
    ctx_y = pl.BlockSpec((TOK_TILE, GROUP_W), lambda i: (tiles.ctx_idx(i), 0))
    lat_y = pl.BlockSpec((TOK_TILE, GROUP_W), lambda i: (tiles.lat_idx(i), 0))
    single = dict(pipeline_mode=pl.Buffered(1))
    return pl.pallas_call(
        functools.partial(_outmlp_kernel, tiles.n_ctx),
        grid=(tiles.n,),
        in_specs=[ctx_x, lat_x] + [ctx_y] * 4 + [lat_y] * 4
        + [pl.BlockSpec((None, None, MOD_CH, D_MODEL), lambda i: (l, tiles.seq_of(i), 0, 0)),
           _layer_spec(norm_g, l), _layer_spec(wout, l, **single), _layer_spec(w1, l, **single),
           _layer_spec(w2, l, **single)],
        out_specs=[ctx_x, lat_x],
        out_shape=[jax.ShapeDtypeStruct(x_ctx.shape, F32), jax.ShapeDtypeStruct(x_lat.shape, F32)],
        compiler_params=_cparams(("arbitrary",)),
        name="out_mlp",
    )(x_ctx, x_lat, *ys_ctx, *ys_lat, mod, norm_g, wout, w1, w2)


def _mixer_call(kernel_fn, name, p, width, T, n_seq, blk0, nb, l, params, consts, state, state_blk,
                scratch):
    assert n_seq % nb == 0 and blk0 % nb == 0
    st_arr, st_in = state
    st_spec = pl.BlockSpec((nb, None) + state_blk, lambda i: (i, l) + (0,) * len(state_blk))
    in_specs = [pl.BlockSpec((nb * T, width), lambda i: (blk0 // nb + i, 0))]
    in_specs += [_layer_spec(a, l) for a in params] + [_const_spec(a) for a in consts]
    y_spec = pl.BlockSpec((nb * T, GROUP_W), lambda i: (i, 0))
    y_shape = jax.ShapeDtypeStruct((n_seq * T, GROUP_W), F32)
    if st_in:
        in_specs.append(st_spec)
        out_specs, out_shape, aliases = [y_spec], [y_shape], {}
    else:
        in_specs.append(pl.BlockSpec(memory_space=pl.ANY))
        out_specs = [y_spec, st_spec]
        out_shape = [y_shape, jax.ShapeDtypeStruct(st_arr.shape, st_arr.dtype)]
        aliases = {len(in_specs) - 1: 1}
    return pl.pallas_call(
        kernel_fn,
        grid=(n_seq // nb,),
        in_specs=in_specs,
        out_specs=out_specs,
        out_shape=out_shape,
        input_output_aliases=aliases,
        scratch_shapes=scratch,
        compiler_params=_cparams(("arbitrary",)),
        name=name,
    )(p, *params, *consts, st_arr)


def _lru_kernel(T, has_init, p_ref, cw_ref, cb_ref, wcat_ref, bcat_ref, lam_ref, st_ref, y_ref, *rest):
    fin_ref = None if has_init else rest[0]
    ac_ref, bc_ref, h_ref = rest[-3:]
    ybr = p_ref[:, 0:GROUP_W]
    xbr = p_ref[:, GROUP_W:2 * GROUP_W]
    xc = _conv_rows(xbr, cw_ref, cb_ref)
    gates = _sigmoid(_mm(xc, wcat_ref[...]) + bcat_ref[...])
    sp = _softplus(-lam_ref[...])
    rows = _row_iota((T, GROUP_W)) % SCAN_BLOCK
    n_blk = T // SCAN_BLOCK
    total = jnp.zeros((T, GROUP_W), F32)
    for d in range(2):
        r = gates[:, (2 * d) * GROUP_W:(2 * d + 1) * GROUP_W]
        ig = gates[:, (2 * d + 1) * GROUP_W:(2 * d + 2) * GROUP_W]
        log_a = (-LRU_C) * r * sp[d:d + 1, :]
        a = jnp.exp(log_a)
        b = jnp.sqrt(jnp.maximum(1.0 - a * a, 0.0)) * (ig * xc)
        k = 1
        while k < SCAN_BLOCK:
            if d == 0:
                ok = rows >= k
                a_s, b_s = _shift_rows(a, -k), _shift_rows(b, -k)
            else:
                ok = rows < SCAN_BLOCK - k
                a_s, b_s = _shift_rows(a, k), _shift_rows(b, k)
            b = jnp.where(ok, a * b_s + b, b)
            a = jnp.where(ok, a * a_s, a)
            k *= 2
        ac_ref[...] = a
        bc_ref[...] = b
        edge = SCAN_BLOCK - 1 if d == 0 else 0

        def body(j, carry, d=d, edge=edge):
            jb = j if d == 0 else n_blk - 1 - j
            r0 = pl.multiple_of(jb * SCAN_BLOCK, SCAN_BLOCK)
            hb = ac_ref[pl.ds(r0, SCAN_BLOCK), :] * carry + bc_ref[pl.ds(r0, SCAN_BLOCK), :]
            h_ref[pl.ds(r0, SCAN_BLOCK), :] = hb
            return jnp.broadcast_to(hb[edge:edge + 1, :], (SCAN_BLOCK, GROUP_W))

        if has_init:
            carry0 = jnp.broadcast_to(st_ref[0, d:d + 1, :], (SCAN_BLOCK, GROUP_W))
        else:
            carry0 = jnp.zeros((SCAN_BLOCK, GROUP_W), F32)
        last = lax.fori_loop(0, n_blk, body, carry0)
        if not has_init:
            fin_ref[0, d:d + 1, :] = last[0:1, :]
        total = total + h_ref[...]
    y_ref[...] = total * _gelu_tanh(ybr)


def _lru_call(p, T, n_seq, blk0, l, params, state):
    return _mixer_call(functools.partial(_lru_kernel, T, state[1]), "rglru_T%d" % T, p, LR_COLS, T,
                       n_seq, blk0, 1, l, params, (), state, (2, GROUP_W),
                       [pltpu.VMEM((T, GROUP_W), F32)] * 3)


def _ssd_kernel(T, has_init, p_ref, cw_ref, cb_ref, dtb_ref, aneg_ref, dexp_ref, nrm_ref, st_ref,
                y_ref, *rest):
    fin_ref = None if has_init else rest[0]
    Q = SSD_CHUNK
    nq = T // Q
    z = p_ref[:, 0:GROUP_W]
    xbc = _silu(_conv_rows(p_ref[:, GROUP_W:4 * GROUP_W], cw_ref, cb_ref))
    xs = xbc[:, 0:GROUP_W]
    bm = xbc[:, GROUP_W:2 * GROUP_W]
    cm = xbc[:, 2 * GROUP_W:3 * GROUP_W]
    dt = _softplus(p_ref[:, 4 * GROUP_W:4 * GROUP_W + PAIR_W] + dtb_ref[...])
    da = dt * aneg_ref[...]
    acf = _seg_cumsum(da, Q, rev=False)
    acb = _seg_cumsum(da, Q, rev=True)
    ti = _row_iota((Q, Q))
    si = _lane_iota((Q, Q))
    low = si <= ti
    upp = si >= ti
    lane_lo = _lane_iota((Q, PAIR_W)) < HEAD_DIM
    row_lo = _row_iota((PAIR_W, SSD_STATE)) < HEAD_DIM

    def col(a, j):
        return a[:, j:j + 1]

    ydiag, ds, dec, ecol = [], [], [], []
    for q in range(nq):
        sl = slice(q * Q, (q + 1) * Q)
        acf_q, acb_q, dt_q = acf[sl], acb[sl], dt[sl]
        acf_t, acb_t, dt_t = acf_q.T, acb_q.T, dt_q.T
        y_q, ds_q, dec_q, ec_q = [], [], [], []
        for pr in range(N_PAIRS):
            ls = slice(pr * PAIR_W, (pr + 1) * PAIR_W)
            bg, cg, xp = bm[sl, ls], cm[sl, ls], xs[sl, ls]
            g = _mm_nt(cg, bg)
            outs = []
            wf_cols, wb_cols, ef_cols, eb_cols, decf, decb = [], [], [], [], [], []
            for hh in range(2):
                h = 2 * pr + hh
                hb_ = N_HEADS + h
                lf = jnp.exp(jnp.minimum(col(acf_q, h) - acf_t[h:h + 1, :], 0.0)) * dt_t[h:h + 1, :]
                lb = jnp.exp(jnp.minimum(col(acb_q, hb_) - acb_t[hb_:hb_ + 1, :], 0.0)) \
                    * dt_t[hb_:hb_ + 1, :]
                mh = g * (jnp.where(low, lf, 0.0) + jnp.where(upp, lb, 0.0))
                outs.append(_mm(mh, xp))
                af_last = acf_q[Q - 1:Q, h:h + 1]
                ab_last = acb_q[0:1, hb_:hb_ + 1]
                wf_cols.append(jnp.exp(af_last - col(acf_q, h)) * col(dt_q, h))
                wb_cols.append(jnp.exp(ab_last - col(acb_q, hb_)) * col(dt_q, hb_))
                ef_cols.append(jnp.exp(col(acf_q, h)))
                eb_cols.append(jnp.exp(col(acb_q, hb_)))
                decf.append(jnp.exp(af_last))
                decb.append(jnp.exp(ab_last))
            y_q.append(jnp.where(lane_lo, outs[0], outs[1]))
            wf = jnp.where(lane_lo, wf_cols[0], wf_cols[1])
            wb = jnp.where(lane_lo, wb_cols[0], wb_cols[1])
            ds_q.append((_mm_tn(xp * wf, bg), _mm_tn(xp * wb, bg)))
            dec_q.append((jnp.where(row_lo, decf[0], decf[1]), jnp.where(row_lo, decb[0], decb[1])))
            ec_q.append((jnp.where(lane_lo, ef_cols[0], ef_cols[1]),
                         jnp.where(lane_lo, eb_cols[0], eb_cols[1])))
        ydiag.append(y_q)
        ds.append(ds_q)
        dec.append(dec_q)
        ecol.append(ec_q)

    prev = [[[None, None] for _ in range(N_PAIRS)] for _ in range(nq)]
    for pr in range(N_PAIRS):
        for d in range(2):
            if has_init:
                s = jnp.concatenate([st_ref[0, d, 2 * pr], st_ref[0, d, 2 * pr + 1]], axis=0)
            else:
                s = jnp.zeros((PAIR_W, SSD_STATE), F32)
            order = range(nq) if d == 0 else range(nq - 1, -1, -1)
            for q in order:
                prev[q][pr][d] = s
                s = dec[q][pr][d] * s + ds[q][pr][d]
            if not has_init:
                fin_ref[0, d, 2 * pr] = s[0:HEAD_DIM, :]
                fin_ref[0, d, 2 * pr + 1] = s[HEAD_DIM:PAIR_W, :]

    ys = []
    for q in range(nq):
        sl = slice(q * Q, (q + 1) * Q)
        parts = []
        for pr in range(N_PAIRS):
            ls = slice(pr * PAIR_W, (pr + 1) * PAIR_W)
            yp = ydiag[q][pr]
            if has_init or nq > 1:
                cg = cm[sl, ls]
                for d in range(2):
                    yp = yp + _mm_nt(cg, prev[q][pr][d]) * ecol[q][pr][d]
            parts.append(yp)
        ys.append(jnp.concatenate(parts, axis=-1))
    y = jnp.concatenate(ys, axis=0) if nq > 1 else ys[0]
    y = y + dexp_ref[...] * xs
    y_ref[...] = _rms(y * _silu(z), nrm_ref[...])


def _ssd_call(p, T, n_seq, blk0, l, params, state):
    return _mixer_call(functools.partial(_ssd_kernel, T, state[1]), "ssd_T%d" % T, p, SS_PAD, T,
                       n_seq, blk0, 1, l, params, (), state, (2, N_HEADS, HEAD_DIM, SSD_STATE), [])


def _hgrn_kernel(T, has_init, p_ref, la_ref, l1_ref, lbd_ref, nrm_ref, bd_ref, st_ref, y_ref, *rest):
    fin_ref = None if has_init else rest[0]
    q_sc, v_sc, k_sc, b_sc, o_sc = rest[-5:]
    NB = HG_BLOCK
    n_blk = T // NB
    q_sc[...] = _silu(p_ref[:, 0:GROUP_W])
    v_sc[...] = p_ref[:, GROUP_W:2 * GROUP_W]
    for d in range(2):
        x = p_ref[:, (2 + d) * GROUP_W:(3 + d) * GROUP_W]
        e1 = jnp.exp(-jnp.abs(x))
        lsig = jnp.minimum(x, 0.0) - jnp.log(1.0 + e1)
        a_ = la_ref[d:d + 1, :] + jnp.zeros_like(x)
        b_ = l1_ref[d:d + 1, :] + lsig
        logf = jnp.maximum(a_, b_) + jnp.log(1.0 + jnp.exp(-jnp.abs(a_ - b_)))
        k_sc[d] = jnp.exp(l1_ref[d:d + 1, :]) * (jnp.where(x >= 0.0, e1, 1.0) / (1.0 + e1)) \
            - lbd_ref[d:d + 1, :]
        b_sc[d] = _seg_cumsum(logf, NB, rev=(d == 1))
    bd = bd_ref[...]
    rows16 = _row_iota((NB, GROUP_W))
    bdm = (_row_iota((PAIR_W, PAIR_W)) < HEAD_DIM) == (_lane_iota((PAIR_W, PAIR_W)) < HEAD_DIM)
    chains = [(d, pr) for d in range(2) for pr in range(N_PAIRS)]

    def body(j, states):
        new_states = []
        for d in range(2):
            jb = j if d == 0 else n_blk - 1 - j
            r0 = pl.multiple_of(jb * NB, NB)
            qb = q_sc[pl.ds(r0, NB), :]
            vb = v_sc[pl.ds(r0, NB), :]
            kb = k_sc[d, pl.ds(r0, NB), :]
            bb = b_sc[d, pl.ds(r0, NB), :]
            parts = []
            for s in range(NB):
                es = jnp.exp(bb - bb[s:s + 1, :])
                keep = (rows16 >= s) if d == 0 else (rows16 <= s)
                parts.append(jnp.where(keep, qb * es * kb[s:s + 1, :], 0.0))
            att = _mm(jnp.concatenate(parts, axis=0), bd)
            o = jnp.zeros((NB, GROUP_W), F32)
            for s in range(NB):
                o = o + att[s * NB:(s + 1) * NB, :] * vb[s:s + 1, :]
            bl = bb[NB - 1:NB, :] if d == 0 else bb[0:1, :]
            qd = qb * jnp.exp(bb)
            kh = kb * jnp.exp(bl - bb)
            dec = jnp.exp(bl)
            inter = []
            for pr in range(N_PAIRS):
                ls = slice(pr * PAIR_W, (pr + 1) * PAIR_W)
                st = states[d * N_PAIRS + pr]
                inter.append(_mm_nt(qd[:, ls], st))
                new_states.append(st * dec[:, ls] + jnp.where(bdm, _mm_tn(vb[:, ls], kh[:, ls]), 0.0))
            o_sc[d, pl.ds(r0, NB), :] = o + jnp.concatenate(inter, axis=-1)
        return tuple(new_states)

    if has_init:
        states0 = tuple(_pair_blockdiag(st_ref[0, d, 2 * pr], st_ref[0, d, 2 * pr + 1]).T
                        for d, pr in chains)
    else:
        states0 = tuple(jnp.zeros((PAIR_W, PAIR_W), F32) for _ in chains)
    states = lax.fori_loop(0, n_blk, body, states0, unroll=HG_UNROLL)
    if not has_init:
        for (d, pr), s_ in zip(chains, states):
            kv = s_.T
            for hh in range(2):
                fin_ref[0, d, 2 * pr + hh] = _pair_diag(kv, hh)
    g = p_ref[:, 4 * GROUP_W:5 * GROUP_W]
    y_ref[...] = _rms(o_sc[0] + o_sc[1], nrm_ref[...]) * _silu(g)


def _hgrn_call(p, T, n_seq, blk0, l, params, bd, state):
    tw = pltpu.VMEM((T, GROUP_W), F32)
    tw2 = pltpu.VMEM((2, T, GROUP_W), F32)
    return _mixer_call(functools.partial(_hgrn_kernel, T, state[1]), "hgrn2_T%d" % T, p, HG_COLS, T,
                       n_seq, blk0, 1, l, params, (bd,), state, (2, N_HEADS, HEAD_DIM, HEAD_DIM),
                       [tw, tw, tw2, tw2, tw2])


def _rwkv_kernel(T, nb, lat, p_ref, mu_ref, w0_ref, a0_ref, wl_ref, kk_ref, ka_ref, rk_ref,
                 lnw_ref, lnb_ref, sm_ref, bd_ref, st_ref, y_ref, *rest):
    fin_ref = None if lat else rest[0]
    xa_sc, xr_sc, yb_sc, yk_sc, c_sc, v_sc, yo_sc = rest[-7:]
    C = RW_CHUNK
    n_chunk = T // C
    P = p_ref[...]
    rowi = _row_iota(P.shape) % T
    if lat:
        gcol = rowi % GRID_W
        shifts = [(-1, gcol >= 1), (1, gcol <= GRID_W - 2), (-GRID_W, rowi >= GRID_W),
                  (GRID_W, rowi < T - GRID_W)]
    else:
        shifts = [(-1, rowi >= 1), (1, rowi <= T - 2)]
    sh = jnp.zeros_like(P)
    for j, (dlt, valid) in enumerate(shifts):
        sh = jnp.where(valid & (sm_ref[j:j + 1, :] > 0.5), _shift_rows(P, dlt), sh)
    P = P + (sh - P) * mu_ref[...]
    r = P[:, 0:GROUP_W]
    k = P[:, GROUP_W:2 * GROUP_W]
    v = P[:, 2 * GROUP_W:3 * GROUP_W]
    lo = P[:, 3 * GROUP_W:3 * GROUP_W + PAIR_W]
    ln = _lane_iota(lo.shape)
    n_t = 2 * RW_LORA_W
    n_i = 4 * RW_LORA_W
    n_s = n_i + RW_LORA_G
    act = jnp.where(ln < n_t, jnp.tanh(lo),
                    jnp.where(ln < n_i, lo, jnp.where(ln < n_s, _sigmoid(lo), 0.0)))
    lora = _mm(act, wl_ref[...])
    gate = lora[:, 4 * GROUP_W:5 * GROUP_W]
    bd = bd_ref[...]
    kkv = k * kk_ref[...]
    kkn = kkv / jnp.maximum(jnp.sqrt(_mm_split(kkv * kkv, bd)), 1e-12)
    v_sc[...] = v
    ksum = jnp.zeros_like(k)
    for d in range(2):
        w_lora = lora[:, d * GROUP_W:(d + 1) * GROUP_W]
        a_lora = lora[:, (2 + d) * GROUP_W:(3 + d) * GROUP_W]
        w_log = -_softplus(-(w0_ref[d:d + 1, :] + w_lora)) - 0.5
        logw = -jnp.exp(w_log)
        a_sig = _sigmoid(a0_ref[d:d + 1, :] + a_lora)
        k_d = k * (1.0 + (a_sig - 1.0) * ka_ref[...])
        ksum = ksum + k_d
        c = _seg_cumsum(logw, C, rev=(d == 1))
        einv = jnp.exp(-c)
        xa_sc[d] = (-kkn) * jnp.exp(c - logw)
        xr_sc[d] = r * jnp.exp(c)
        yb_sc[d] = (kkn * a_sig) * einv
        yk_sc[d] = k_d * einv
        c_sc[d] = c

    R2 = 2 * C
    assert R2 % PAIR_W == 0
    lane_lo = _lane_iota((C, PAIR_W)) < HEAD_DIM
    tt2 = _row_iota((R2, 2 * R2)) % C
    ss2 = _lane_iota((R2, 2 * R2)) % C
    n_fac = C.bit_length() - 1

    def stack2(x):
        return jnp.concatenate([jnp.where(lane_lo, x, 0.0), jnp.where(lane_lo, 0.0, x)], axis=0)

    chains = [(sq, d, pr) for sq in range(nb) for d in range(2) for pr in range(N_PAIRS)]
    each = lambda f, *cols: [f(*a) for a in zip(*cols)]

    def body(j, states):
        lhs, rhs_t, vst, pc, strict2, incl2, dst = [], [], [], [], [], [], []
        for sq, d, pr in chains:
            jc = j if d == 0 else n_chunk - 1 - j
            r0 = pl.multiple_of(sq * T + jc * C, C)
            ls = slice(pr * PAIR_W, (pr + 1) * PAIR_W)
            at = xa_sc[d, pl.ds(r0, C), ls]
            rt = xr_sc[d, pl.ds(r0, C), ls]
            bt = yb_sc[d, pl.ds(r0, C), ls]
            kt = yk_sc[d, pl.ds(r0, C), ls]
            cc = c_sc[d, pl.ds(r0, C), ls]
            pc.append(jnp.exp(cc[C - 1:C, :] if d == 0 else cc[0:1, :]))
            lhs.append(jnp.concatenate([stack2(at), stack2(rt)], axis=0))
            rhs_t.append(jnp.concatenate([stack2(bt), stack2(kt)], axis=0))
            vst.append(stack2(v_sc[pl.ds(r0, C), ls]))
            strict2.append((ss2 < tt2) if d == 0 else (ss2 > tt2))
            incl2.append((ss2 <= tt2) if d == 0 else (ss2 >= tt2))
            dst.append((d, r0, ls))
        gg = each(_mm_nt, lhs, rhs_t)
        uu = each(_mm_nt, lhs, states)
        am = each(lambda g, m: jnp.where(m, g[0:R2, :], 0.0), gg, strict2)
        ncat = each(lambda g, m: jnp.where(m, g[R2:2 * R2, :], 0.0), gg, incl2)
        qb = each(lambda a_: a_[:, 0:R2].astype(BF16), am)
        xs = each(lambda u, a_, v_: u[0:R2, :] + _mm(a_[:, R2:2 * R2], v_), uu, am, vst)
        for i in range(n_fac):
            xh = each(lambda x: x.astype(BF16), xs)
            xl = each(lambda x, h: (x - h.astype(F32)).astype(BF16), xs, xh)
            corr = each(lambda q, l_: jnp.dot(q, l_, preferred_element_type=F32), qb, xl)
            if i < n_fac - 1:
                zz = each(lambda q, h: jnp.dot(q, jnp.concatenate([h, q], axis=1),
                                               preferred_element_type=F32), qb, xh)
                xs = each(lambda x, z, c_: x + z[:, 0:PAIR_W] + c_, xs, zz, corr)
                qb = each(lambda z: z[:, PAIR_W:PAIR_W + R2].astype(BF16), zz)
            else:
                xs = each(lambda x, q, h, c_: x + jnp.dot(q, h, preferred_element_type=F32) + c_,
                          xs, qb, xh, corr)
        sv = each(lambda x, v_: jnp.concatenate([x, v_], axis=0), xs, vst)
        yst = each(lambda u, n, s_: u[R2:2 * R2, :] + _mm(n, s_), uu, ncat, sv)
        for (d, r0, ls), y_ in zip(dst, yst):
            yo_sc[d, pl.ds(r0, C), ls] = y_[0:C, :] + y_[C:2 * C, :]
        return tuple(each(lambda s_, p, v_, r_: s_ * p + _mm_tn(v_, r_ * p), states, pc, sv, rhs_t))

    if lat:
        states0 = tuple(_pair_blockdiag(st_ref[sq, d, 2 * pr], st_ref[sq, d, 2 * pr + 1])
                        for sq, d, pr in chains)
    else:
        states0 = tuple(jnp.zeros((PAIR_W, PAIR_W), F32) for _ in chains)
    states = lax.fori_loop(0, n_chunk, body, states0)
    if not lat:
        for (sq, d, pr), s_ in zip(chains, states):
            for hh in range(2):
                fin_ref[sq, d, 2 * pr + hh] = _pair_diag(s_, hh)
    y = yo_sc[0] + yo_sc[1]
    inv_n = 1.0 / HEAD_DIM
    mu = _mm_split(y, bd) * inv_n
    yc = y - mu
    var = _mm_split(yc * yc, bd) * inv_n
    yn = yc * lax.rsqrt(var + RW_GN_EPS) * lnw_ref[...] + lnb_ref[...]
    bonus = _mm_split(r * ksum * rk_ref[...], bd) * v
    y_ref[...] = (yn + bonus) * gate


def _rwkv_call(p, T, n_seq, blk0, nb, lat, l, params, sm, bd, state):
    tw = pltpu.VMEM((nb * T, GROUP_W), F32)
    tw2 = pltpu.VMEM((2, nb * T, GROUP_W), F32)
    return _mixer_call(functools.partial(_rwkv_kernel, T, nb, lat), "rwkv7_T%d" % T, p, RW_PAD, T,
                       n_seq, blk0, nb, l, params, (sm, bd), state,
                       (2, N_HEADS, HEAD_DIM, HEAD_DIM), [tw2, tw2, tw2, tw2, tw2, tw, tw2])


def _pad_last(a, n):
    return jnp.pad(a, ((0, 0),) * (a.ndim - 1) + ((0, n - a.shape[-1]),))


def _row(a):
    return a.reshape(a.shape[0], 1, -1)


def _heads_blockdiag(w):
    eye = jnp.eye(N_HEADS, dtype=w.dtype)
    full = w[..., :, :, None, :] * eye[:, None, :, None]
    return full.reshape(w.shape[:-3] + (GROUP_W, GROUP_W))


def kernel(x_prompt, x_sample, state_rwkv, state_hgrn, state_ssd, state_lru, c, c_ctx, mod_w, mod_b, norm_g, w_in, w_out, rw_mu, rw_w0, rw_w2, rw_a0, rw_a2, rw_g2, rw_kk, rw_ka, rw_rk, rw_ln_w, rw_ln_b, hg_lb, hg_norm, ss_conv_w, ss_conv_b, ss_dt_bias, ss_A_log, ss_D, ss_norm, lr_conv_w, lr_conv_b, lr_wa, lr_ba, lr_wx, lr_bx, lr_lam, mlp_w1, mlp_w2):
    bp, t_ctx, _ = x_prompt.shape
    bl, t_lat, _ = x_sample.shape
    n_ctx_tok = bp * t_ctx
    assert n_ctx_tok % t_lat == 0 and t_lat % GRID_W == 0 and 1 + bl <= 8
    lat_blk0 = n_ctx_tok // t_lat
    tiles = _Tiles(n_ctx_tok, bl * t_lat, t_lat)

    x_ctx = x_prompt.reshape(n_ctx_tok, D_MODEL)
    x_lat = x_sample.reshape(bl * t_lat, D_MODEL)
    cond8 = jnp.zeros((8, D_MODEL), F32).at[0].set(c_ctx).at[1:1 + bl].set(c)
    mod = _mod_call(cond8, mod_w, mod_b).reshape(DEPTH, 8, MOD_CH, D_MODEL)

    o1 = RW_COLS
    o2 = o1 + HG_COLS
    o3 = o2 + SS_COLS
    w_in_b = w_in.astype(BF16)
    ws_in = (_pad_last(w_in_b[:, :, :o1], RW_PAD), w_in_b[:, :, o1:o2],
             _pad_last(w_in_b[:, :, o2:o3], SS_PAD), w_in_b[:, :, o3:])
    w_out_b, w1_b, w2_b = w_out.astype(BF16), mlp_w1.astype(BF16), mlp_w2.astype(BF16)

    lane = jnp.arange(GROUP_W)
    bd_ones = (lane[:, None] // HEAD_DIM == lane[None, :] // HEAD_DIM).astype(BF16)

    col = jnp.arange(RW_PAD)
    quarter = col // RW_SHIFT_SPLIT
    real = col < RW_COLS
    sm_lat = jnp.stack([(quarter == i) & real for i in range(4)]).astype(F32)
    sm_ctx = jnp.stack([(quarter < 2) & real, (quarter >= 2) & real,
                        jnp.zeros_like(real), jnp.zeros_like(real)]).astype(F32)
    lora_blocks = (rw_w2[:, 0], rw_w2[:, 1], rw_a2[:, 0], rw_a2[:, 1], rw_g2)
    wl = jnp.concatenate(
        [jnp.pad(blk, ((0, 0), (0, 0), (j * GROUP_W, (len(lora_blocks) - 1 - j) * GROUP_W)))
         for j, blk in enumerate(lora_blocks)], axis=1)
    wl = jnp.pad(wl, ((0, 0), (0, PAIR_W - wl.shape[1]), (0, 0))).astype(BF16)
    rw_params = (_pad_last(_row(rw_mu), RW_PAD), rw_w0, rw_a0, wl, _row(rw_kk), _row(rw_ka),
                 _row(rw_rk), _row(rw_ln_w), _row(rw_ln_b))

    lb_soft = jax.nn.softmax(hg_lb.astype(F32), axis=0)
    lower = jnp.cumsum(lb_soft, axis=0) - lb_soft[0]
    lb_floor = jnp.maximum(lower, LB_FLOOR)
    hg_params = (jnp.log(lb_floor), jnp.log1p(-lower), lb_floor - lower, _row(hg_norm))

    ss_params = (ss_conv_w, _row(ss_conv_b), _pad_last(_row(ss_dt_bias), PAIR_W),
                 _pad_last(_row(-jnp.exp(ss_A_log.astype(F32))), PAIR_W),
                 _row(jnp.repeat(ss_D, HEAD_DIM, axis=-1)), _row(ss_norm))

    wcat = jnp.concatenate([_heads_blockdiag(lr_wa[:, 0]), _heads_blockdiag(lr_wx[:, 0]),
                            _heads_blockdiag(lr_wa[:, 1]), _heads_blockdiag(lr_wx[:, 1])],
                           axis=-1).astype(BF16)
    bcat = _row(jnp.stack([lr_ba[:, 0], lr_bx[:, 0], lr_ba[:, 1], lr_bx[:, 1]], axis=1))
    lr_params = (lr_conv_w, _row(lr_conv_b), wcat, bcat, lr_lam)

    new_rw = jnp.zeros((bp, DEPTH, 2, N_HEADS, HEAD_DIM, HEAD_DIM), F32)
    new_hg = jnp.zeros((bp, DEPTH, 2, N_HEADS, HEAD_DIM, HEAD_DIM), F32)
    new_ss = jnp.zeros((bp, DEPTH, 2, N_HEADS, HEAD_DIM, SSD_STATE), F32)
    new_lr = jnp.zeros((bp, DEPTH, 2, GROUP_W), F32)

    for l in range(DEPTH):
        p_rw, p_hg, p_ss, p_lr = _inproj_call(tiles, l, x_ctx, x_lat, mod, norm_g, ws_in)

        y_rw_c, new_rw = _rwkv_call(p_rw, t_ctx, bp, 0, RW_CTX_SEQS, False, l, rw_params, sm_ctx,
                                    bd_ones, (new_rw, False))
        y_rw_l, = _rwkv_call(p_rw, t_lat, bl, lat_blk0, RW_LAT_SEQS, True, l, rw_params, sm_lat,
                             bd_ones, (state_rwkv, True))
        y_hg_c, new_hg = _hgrn_call(p_hg, t_ctx, bp, 0, l, hg_params, bd_ones, (new_hg, False))
        y_hg_l, = _hgrn_call(p_hg, t_lat, bl, lat_blk0, l, hg_params, bd_ones, (state_hgrn, True))
        y_ss_c, new_ss = _ssd_call(p_ss, t_ctx, bp, 0, l, ss_params, (new_ss, False))
        y_ss_l, = _ssd_call(p_ss, t_lat, bl, lat_blk0, l, ss_params, (state_ssd, True))
        y_lr_c, new_lr = _lru_call(p_lr, t_ctx, bp, 0, l, lr_params, (new_lr, False))
        y_lr_l, = _lru_call(p_lr, t_lat, bl, lat_blk0, l, lr_params, (state_lru, True))

        x_ctx, x_lat = _outmlp_call(tiles, l, x_ctx, x_lat, (y_rw_c, y_hg_c, y_ss_c, y_lr_c),
                                    (y_rw_l, y_hg_l, y_ss_l, y_lr_l), mod, norm_g,
                                    w_out_b, w1_b, w2_b)

    return (x_ctx.reshape(bp, t_ctx, D_MODEL), x_lat.reshape(bl, t_lat, D_MODEL),
            new_rw, new_hg, new_ss, new_lr)
```

```python
import functools

import jax
import jax.numpy as jnp
from jax import lax
from jax.experimental import pallas as pl
from jax.experimental.pallas import tpu as pltpu

F32 = jnp.float32
BF16 = jnp.bfloat16

D_MODEL = 1024
DEPTH = 4
MOD_CH = 6
GROUP_W = 256
HEAD_DIM = 64
N_HEADS = 4
N_PAIRS = 2
PAIR_W = 128
D_FF = 4096
GRID_W = 64
NORM_EPS = 1e-6
LB_FLOOR = 1e-30
RW_GN_EPS = 64e-5
LRU_C = 8.0
SSD_STATE = 128
CONV_TAPS = (-2, -1, 0, 1)

RW_COLS = 864
RW_PAD = 896
RW_SHIFT_SPLIT = RW_COLS // 4
RW_LORA_W = 16
RW_LORA_G = 32
HG_COLS = 1280
SS_COLS = 1032
SS_PAD = 1152
LR_COLS = 512
RW_CHUNK = 64
RW_CTX_SEQS = 4
RW_LAT_SEQS = 1
HG_BLOCK = 16
HG_UNROLL = 4
SSD_CHUNK = 256
SUBLANES = 8
SCAN_BLOCK = SUBLANES

TOK_TILE = 512
VMEM_LIMIT = 56 * 1024 * 1024


def _cparams(sem):
    return pltpu.CompilerParams(dimension_semantics=sem, vmem_limit_bytes=VMEM_LIMIT)


def _layer_spec(a, l, **kw):
    return pl.BlockSpec((None,) + a.shape[1:], lambda *_: (l,) + (0,) * (a.ndim - 1), **kw)


def _const_spec(a):
    return pl.BlockSpec(a.shape, lambda *_: (0,) * a.ndim)


def _mm(a, b):
    return jnp.dot(a.astype(BF16), b.astype(BF16), preferred_element_type=F32)


def _mm_nt(a, b):
    return lax.dot_general(a.astype(BF16), b.astype(BF16), (((1,), (1,)), ((), ())),
                           preferred_element_type=F32)


def _mm_tn(a, b):
    return lax.dot_general(a.astype(BF16), b.astype(BF16), (((0,), (0,)), ((), ())),
                           preferred_element_type=F32)


def _mm_split(a, b_bf16):
    hi = a.astype(BF16)
    lo = (a - hi.astype(F32)).astype(BF16)
    return (jnp.dot(hi, b_bf16, preferred_element_type=F32)
            + jnp.dot(lo, b_bf16, preferred_element_type=F32))


def _row_iota(shape):
    return lax.broadcasted_iota(jnp.int32, shape, 0)


def _lane_iota(shape):
    return lax.broadcasted_iota(jnp.int32, shape, 1)


def _shift_rows(x, d):
    n = x.shape[0]
    s = (-d) % n
    if s == 0:
        return x
    return pltpu.roll(x, s, 0)


def _seg_cumsum_slab(src, dst, tot, n, seg, rev):
    sb = SUBLANES
    nb = n // sb
    g = seg // sb
    order = list(range(sb))[::-1] if rev else list(range(sb))
    part, prev = {}, None
    for r in order:
        x_r = src[pl.ds(r, nb, stride=sb), :]
        part[r] = x_r if prev is None else part[prev] + x_r
        prev = r
    off = None
    if g > 1:
        tot[...] = part[prev]
        ng = nb // g
        qorder = list(range(g))[::-1] if rev else list(range(g))
        run, excl = None, {}
        for q in qorder:
            excl[q] = run
            t_q = tot[pl.ds(q, ng, stride=g), :]
            run = t_q if run is None else run + t_q
        for q in qorder:
            tot[pl.ds(q, ng, stride=g), :] = jnp.zeros((ng, PAIR_W), F32) if excl[q] is None else excl[q]
        off = tot[...]
    for r in range(sb):
        dst[pl.ds(r, nb, stride=sb), :] = part[r] if off is None else part[r] + off


def _sigmoid(x):
    return jax.nn.sigmoid(x)


def _silu(x):
    return x * _sigmoid(x)


def _softplus(x):
    return jnp.maximum(x, 0.0) + jnp.log(1.0 + jnp.exp(-jnp.abs(x)))


def _gelu_tanh(x):
    c = 0.7978845608028654
    return 0.5 * x * (1.0 + jnp.tanh(c * (x + 0.044715 * (x * x * x))))


def _rms(x, g):
    return x * lax.rsqrt(jnp.mean(x * x, axis=-1, keepdims=True) + NORM_EPS) * g


def _conv_rows(x, w_ref, b_ref):
    t = x.shape[0]
    rows = _row_iota(x.shape)
    y = b_ref[...] + jnp.zeros_like(x)
    for j, d in enumerate(CONV_TAPS):
        valid = (rows + d >= 0) & (rows + d < t)
        y = y + jnp.where(valid, _shift_rows(x, d), 0.0) * w_ref[j:j + 1, :]
    return y


def _pair_blockdiag(a, b):
    z = jnp.zeros_like(a)
    return jnp.concatenate([jnp.concatenate([a, z], axis=1), jnp.concatenate([z, b], axis=1)], axis=0)


def _pair_diag(m, hh):
    return m[hh * HEAD_DIM:(hh + 1) * HEAD_DIM, hh * HEAD_DIM:(hh + 1) * HEAD_DIM]


def _mod_kernel(c_ref, w_ref, b_ref, o_ref):
    c = c_ref[...]
    o_ref[...] = _mm(_silu(c), w_ref[...]) + b_ref[...]


def _mod_call(cond8, mod_w, mod_b):
    return pl.pallas_call(
        _mod_kernel,
        grid=(DEPTH, MOD_CH),
        in_specs=[
            pl.BlockSpec((8, D_MODEL), lambda l, j: (0, 0)),
            pl.BlockSpec((None, D_MODEL, D_MODEL), lambda l, j: (l, 0, j)),
            pl.BlockSpec((None, 1, D_MODEL), lambda l, j: (l, 0, j)),
        ],
        out_specs=pl.BlockSpec((None, 8, D_MODEL), lambda l, j: (l, 0, j)),
        out_shape=jax.ShapeDtypeStruct((DEPTH, 8, MOD_CH * D_MODEL), F32),
        compiler_params=_cparams(("arbitrary", "arbitrary")),
        name="modulation",
    )(cond8, mod_w, mod_b.reshape(DEPTH, 1, MOD_CH * D_MODEL))


class _Tiles:
    def __init__(self, n_ctx_tok, n_lat_tok, t_lat):
        assert n_ctx_tok % TOK_TILE == 0 and t_lat % TOK_TILE == 0
        self.n_ctx = n_ctx_tok // TOK_TILE
        self.n_lat = n_lat_tok // TOK_TILE
        self.per_seq = t_lat // TOK_TILE
        self.n = self.n_ctx + self.n_lat

    def ctx_idx(self, i):
        return jnp.minimum(i, self.n_ctx - 1)

    def lat_idx(self, i):
        return jnp.clip(i - self.n_ctx, 0, self.n_lat - 1)

    def seq_of(self, i):
        return jnp.where(i < self.n_ctx, 0, 1 + (i - self.n_ctx) // self.per_seq)


def _inproj_kernel(n_ctx_tiles, xc_ref, xl_ref, mod_ref, g_ref, wrw, whg, wss, wlr,
                   orw, ohg, oss, olr):
    x = jnp.where(pl.program_id(0) < n_ctx_tiles, xc_ref[...], xl_ref[...])
    m = mod_ref[...]
    h = _rms(x, g_ref[0:1, :]) * (1.0 + m[1:2, :]) + m[0:1, :]
    hb = h.astype(BF16)
    orw[...] = jnp.dot(hb, wrw[...], preferred_element_type=F32)
    ohg[...] = jnp.dot(hb, whg[...], preferred_element_type=F32)
    oss[...] = jnp.dot(hb, wss[...], preferred_element_type=F32)
    olr[...] = jnp.dot(hb, wlr[...], preferred_element_type=F32)


def _inproj_call(tiles, l, x_ctx, x_lat, mod, norm_g, ws):
    widths = (RW_PAD, HG_COLS, SS_PAD, LR_COLS)
    n_tok = tiles.n * TOK_TILE
    return pl.pallas_call(
        functools.partial(_inproj_kernel, tiles.n_ctx),
        grid=(tiles.n,),
        in_specs=[
            pl.BlockSpec((TOK_TILE, D_MODEL), lambda i: (tiles.ctx_idx(i), 0)),
            pl.BlockSpec((TOK_TILE, D_MODEL), lambda i: (tiles.lat_idx(i), 0)),
            pl.BlockSpec((None, None, MOD_CH, D_MODEL), lambda i: (l, tiles.seq_of(i), 0, 0)),
            _layer_spec(norm_g, l),
        ] + [_layer_spec(w, l) for w in ws],
        out_specs=[pl.BlockSpec((TOK_TILE, w), lambda i: (i, 0)) for w in widths],
        out_shape=[jax.ShapeDtypeStruct((n_tok, w), F32) for w in widths],
        compiler_params=_cparams(("arbitrary",)),
        name="in_proj",
    )(x_ctx, x_lat, mod, norm_g, *ws)


def _outmlp_kernel(n_ctx_tiles, xc_ref, xl_ref, c0, c1, c2, c3, l0, l1, l2, l3, mod_ref, g_ref,
                   wout, w1, w2, oc_ref, ol_ref):
    is_ctx = pl.program_id(0) < n_ctx_tiles
    mix = jnp.concatenate(
        [jnp.where(is_ctx, c[...], l[...]) for c, l in ((c0, l0), (c1, l1), (c2, l2), (c3, l3))],
        axis=-1).astype(BF16)
    m = mod_ref[...]
    g = g_ref[...]
    x = jnp.where(is_ctx, xc_ref[...], xl_ref[...])
    t = jnp.dot(mix, wout[...], preferred_element_type=F32)
    x1 = x + m[2:3, :] * _rms(t, g[1:2, :])
    h2 = (_rms(x1, g[2:3, :]) * (1.0 + m[4:5, :]) + m[3:4, :]).astype(BF16)
    acc = jnp.zeros_like(x)
    for f in range(D_FF // D_MODEL):
        u = jnp.dot(h2, w1[:, f * D_MODEL:(f + 1) * D_MODEL], preferred_element_type=F32)
        u = jnp.square(jnp.maximum(u, 0.0)).astype(BF16)
        acc = acc + jnp.dot(u, w2[f * D_MODEL:(f + 1) * D_MODEL, :], preferred_element_type=F32)
    res = x1 + m[5:6, :] * _rms(acc, g[3:4, :])

    @pl.when(is_ctx)
    def _():
        oc_ref[...] = res

    @pl.when(jnp.logical_not(is_ctx))
    def _():
        ol_ref[...] = res


def _outmlp_call(tiles, l, x_ctx, x_lat, ys_ctx, ys_lat, mod, norm_g, wout, w1, w2):
    ctx_x = pl.BlockSpec((TOK_TILE, D_MODEL), lambda i: (tiles.ctx_idx(i), 0))
    lat_x = pl.BlockSpec((TOK_TILE, D_MODEL), lambda i: (tiles.lat_idx(i), 0))
    ctx_y = pl.BlockSpec((TOK_TILE, GROUP_W), lambda i: (tiles.ctx_idx(i), 0))
    lat_y = pl.BlockSpec((TOK_TILE, GROUP_W), lambda i: (tiles.lat_idx(i), 0))
    single = dict(pipeline_mode=pl.Buffered(1))
    return pl.pallas_call(
        functools.partial(_outmlp_kernel, tiles.n_ctx),
        grid=(tiles.n,),
        in_specs=[ctx_x, lat_x] + [ctx_y] * 4 + [lat_y] * 4
        + [pl.BlockSpec((None, None, MOD_CH, D_MODEL), lambda i: (l, tiles.seq_of(i), 0, 0)),
           _layer_spec(norm_g, l), _layer_spec(wout, l, **single), _layer_spec(w1, l, **single),
           _layer_spec(w2, l, **single)],
        out_specs=[ctx_x, lat_x],
        out_shape=[jax.ShapeDtypeStruct(x_ctx.shape, F32), jax.ShapeDtypeStruct(x_lat.shape, F32)],
        compiler_params=_cparams(("arbitrary",)),
        name="out_mlp",
    )(x_ctx, x_lat, *ys_ctx, *ys_lat, mod, norm_g, wout, w1, w2)


def _mixer_call(kernel_fn, name, p, width, T, n_seq, blk0, nb, l, params, consts, state, state_blk,
                scratch):
    assert n_seq % nb == 0 and blk0 % nb == 0
    st_arr, st_in = state
    st_spec = pl.BlockSpec((nb, None) + state_blk, lambda i: (i, l) + (0,) * len(state_blk))
    in_specs = [pl.BlockSpec((nb * T, width), lambda i: (blk0 // nb + i, 0))]
    in_specs += [_layer_spec(a, l) for a in params] + [_const_spec(a) for a in consts]
    y_spec = pl.BlockSpec((nb * T, GROUP_W), lambda i: (i, 0))
    y_shape = jax.ShapeDtypeStruct((n_seq * T, GROUP_W), F32)
    if st_in:
        in_specs.append(st_spec)
        out_specs, out_shape, aliases = [y_spec], [y_shape], {}
    else:
        in_specs.append(pl.BlockSpec(memory_space=pl.ANY))
        out_specs = [y_spec, st_spec]
        out_shape = [y_shape, jax.ShapeDtypeStruct(st_arr.shape, st_arr.dtype)]
        aliases = {len(in_specs) - 1: 1}
    return pl.pallas_call(
        kernel_fn,
        grid=(n_seq // nb,),
        in_specs=in_specs,
        out_specs=out_specs,
        out_shape=out_shape,
        input_output_aliases=aliases,
        scratch_shapes=scratch,
        compiler_params=_cparams(("arbitrary",)),
        name=name,
    )(p, *params, *consts, st_arr)


def _lru_kernel(T, has_init, p_ref, cw_ref, cb_ref, wcat_ref, bcat_ref, lam_ref, st_ref, y_ref, *rest):
    fin_ref = None if has_init else rest[0]
    a_sc, b_sc, h_sc, al_sc, bl_sc, c_sc = rest[-6:]
    SB = SCAN_BLOCK
    n_blk = T // SB
    n_slab = GROUP_W // PAIR_W
    ybr = p_ref[:, 0:GROUP_W]
    xbr = p_ref[:, GROUP_W:2 * GROUP_W]
    xc = _conv_rows(xbr, cw_ref, cb_ref)
    gates = _sigmoid(_mm(xc, wcat_ref[...]) + bcat_ref[...])
    sp = _softplus(-lam_ref[...])
    chains = [(d, hv) for d in range(2) for hv in range(n_slab)]
    for d in range(2):
        r = gates[:, (2 * d) * GROUP_W:(2 * d + 1) * GROUP_W]
        ig = gates[:, (2 * d + 1) * GROUP_W:(2 * d + 2) * GROUP_W]
        log_a = (-LRU_C) * r * sp[d:d + 1, :]
        a = jnp.exp(log_a)
        b = jnp.sqrt(jnp.maximum(1.0 - a * a, 0.0)) * (ig * xc)
        for hv in range(n_slab):
            a_sc[d, hv] = a[:, hv * PAIR_W:(hv + 1) * PAIR_W]
            b_sc[d, hv] = b[:, hv * PAIR_W:(hv + 1) * PAIR_W]

    def rows_of(ref, r):
        return ref[pl.ds(r, n_blk, stride=SB), :]

    acc, bcc = {}, {}
    for d, hv in chains:
        order = list(range(SB)) if d == 0 else list(range(SB - 1, -1, -1))
        prev = None
        for r in order:
            a_r, b_r = rows_of(a_sc.at[d, hv], r), rows_of(b_sc.at[d, hv], r)
            if prev is None:
                acc[d, hv, r], bcc[d, hv, r] = a_r, b_r
            else:
                acc[d, hv, r] = a_r * acc[d, hv, prev]
                bcc[d, hv, r] = a_r * bcc[d, hv, prev] + b_r
            prev = r
        al_sc[d, hv] = acc[d, hv, prev]
        bl_sc[d, hv] = bcc[d, hv, prev]

    def body(j, carry):
        new = []
        for (d, hv), c in zip(chains, carry):
            jb = j if d == 0 else n_blk - 1 - j
            c_sc[d, hv, pl.ds(jb, 1), :] = c
            new.append(al_sc[d, hv, pl.ds(jb, 1), :] * c + bl_sc[d, hv, pl.ds(jb, 1), :])
        return tuple(new)

    if has_init:
        carry0 = tuple(st_ref[0, d:d + 1, hv * PAIR_W:(hv + 1) * PAIR_W] for d, hv in chains)
    else:
        carry0 = tuple(jnp.zeros((1, PAIR_W), F32) for _ in chains)
    last = lax.fori_loop(0, n_blk, body, carry0, unroll=SB)
    for (d, hv), c in zip(chains, last):
        if not has_init:
            fin_ref[0, d:d + 1, hv * PAIR_W:(hv + 1) * PAIR_W] = c
        cin = c_sc[d, hv]
        for r in range(SB):
            h_sc[d, hv, pl.ds(r, n_blk, stride=SB), :] = acc[d, hv, r] * cin + bcc[d, hv, r]
    total = jnp.concatenate([h_sc[0, hv] + h_sc[1, hv] for hv in range(n_slab)], axis=-1)
    y_ref[...] = total * _gelu_tanh(ybr)


def _lru_call(p, T, n_seq, blk0, l, params, state):
    n_slab = GROUP_W // PAIR_W
    full = pltpu.VMEM((2, n_slab, T, PAIR_W), F32)
    edge = pltpu.VMEM((2, n_slab, T // SCAN_BLOCK, PAIR_W), F32)
    return _mixer_call(functools.partial(_lru_kernel, T, state[1]), "rglru_T%d" % T, p, LR_COLS, T,
                       n_seq, blk0, 1, l, params, (), state, (2, GROUP_W),
                       [full, full, full, edge, edge, edge])


def _ssd_kernel(T, has_init, p_ref, cw_ref, cb_ref, dtb_ref, aneg_ref, dexp_ref, nrm_ref, st_ref,
                y_ref, *rest):
    fin_ref = None if has_init else rest[0]
    Q = SSD_CHUNK
    nq = T // Q
    z = p_ref[:, 0:GROUP_W]
    xbc = _silu(_conv_rows(p_ref[:, GROUP_W:4 * GROUP_W], cw_ref, cb_ref))
    xs = xbc[:, 0:GROUP_W]
    bm = xbc[:, GROUP_W:2 * GROUP_W]
    cm = xbc[:, 2 * GROUP_W:3 * GROUP_W]
    dt = _softplus(p_ref[:, 4 * GROUP_W:4 * GROUP_W + PAIR_W] + dtb_ref[...])
    da = dt * aneg_ref[...]
    da_sc, ac_sc, ct_sc = rest[-3:]
    da_sc[...] = da
    _seg_cumsum_slab(da_sc, ac_sc.at[0], ct_sc, T, Q, rev=False)
    _seg_cumsum_slab(da_sc, ac_sc.at[1], ct_sc, T, Q, rev=True)
    acf = ac_sc[0]
    acb = ac_sc[1]
    ti = _row_iota((Q, Q))
    si = _lane_iota((Q, Q))
    low = si <= ti
    upp = si >= ti
    lane_lo = _lane_iota((Q, PAIR_W)) < HEAD_DIM
    row_lo = _row_iota((PAIR_W, SSD_STATE)) < HEAD_DIM

    def col(a, j):
        return a[:, j:j + 1]

    ydiag, ds, dec, ecol = [], [], [], []
    for q in range(nq):
        sl = slice(q * Q, (q + 1) * Q)
        acf_q, acb_q, dt_q = acf[sl], acb[sl], dt[sl]
        acf_t, acb_t, dt_t = acf_q.T, acb_q.T, dt_q.T
        y_q, ds_q, dec_q, ec_q = [], [], [], []
        for pr in range(N_PAIRS):
            ls = slice(pr * PAIR_W, (pr + 1) * PAIR_W)
            bg, cg, xp = bm[sl, ls], cm[sl, ls], xs[sl, ls]
            g = _mm_nt(cg, bg)
            outs = []
            wf_cols, wb_cols, ef_cols, eb_cols, decf, decb = [], [], [], [], [], []
            for hh in range(2):
                h = 2 * pr + hh
                hb_ = N_HEADS + h
                lf = jnp.exp(jnp.minimum(col(acf_q, h) - acf_t[h:h + 1, :], 0.0)) * dt_t[h:h + 1, :]
                lb = jnp.exp(jnp.minimum(col(acb_q, hb_) - acb_t[hb_:hb_ + 1, :], 0.0)) \
                    * dt_t[hb_:hb_ + 1, :]
                mh = g * (jnp.where(low, lf, 0.0) + jnp.where(upp, lb, 0.0))
                outs.append(_mm(mh, xp))
                af_last = acf_q[Q - 1:Q, h:h + 1]
                ab_last = acb_q[0:1, hb_:hb_ + 1]
                wf_cols.append(jnp.exp(af_last - col(acf_q, h)) * col(dt_q, h))
                wb_cols.append(jnp.exp(ab_last - col(acb_q, hb_)) * col(dt_q, hb_))
                ef_cols.append(jnp.exp(col(acf_q, h)))
                eb_cols.append(jnp.exp(col(acb_q, hb_)))
                decf.append(jnp.exp(af_last))
                decb.append(jnp.exp(ab_last))
            y_q.append(jnp.where(lane_lo, outs[0], outs[1]))
            wf = jnp.where(lane_lo, wf_cols[0], wf_cols[1])
            wb = jnp.where(lane_lo, wb_cols[0], wb_cols[1])
            ds_q.append((_mm_tn(xp * wf, bg), _mm_tn(xp * wb, bg)))
            dec_q.append((jnp.where(row_lo, decf[0], decf[1]), jnp.where(row_lo, decb[0], decb[1])))
            ec_q.append((jnp.where(lane_lo, ef_cols[0], ef_cols[1]),
                         jnp.where(lane_lo, eb_cols[0], eb_cols[1])))
        ydiag.append(y_q)
        ds.append(ds_q)
        dec.append(dec_q)
        ecol.append(ec_q)

    prev = [[[None, None] for _ in range(N_PAIRS)] for _ in range(nq)]
    for pr in range(N_PAIRS):
        for d in range(2):
            if has_init:
                s = jnp.concatenate([st_ref[0, d, 2 * pr], st_ref[0, d, 2 * pr + 1]], axis=0)
            else:
                s = jnp.zeros((PAIR_W, SSD_STATE), F32)
            order = range(nq) if d == 0 else range(nq - 1, -1, -1)
            for q in order:
                prev[q][pr][d] = s
                s = dec[q][pr][d] * s + ds[q][pr][d]
            if not has_init:
                fin_ref[0, d, 2 * pr] = s[0:HEAD_DIM, :]
                fin_ref[0, d, 2 * pr + 1] = s[HEAD_DIM:PAIR_W, :]

    ys = []
    for q in range(nq):
        sl = slice(q * Q, (q + 1) * Q)
        parts = []
        for pr in range(N_PAIRS):
            ls = slice(pr * PAIR_W, (pr + 1) * PAIR_W)
            yp = ydiag[q][pr]
            if has_init or nq > 1:
                cg = cm[sl, ls]
                for d in range(2):
                    yp = yp + _mm_nt(cg, prev[q][pr][d]) * ecol[q][pr][d]
            parts.append(yp)
        ys.append(jnp.concatenate(parts, axis=-1))
    y = jnp.concatenate(ys, axis=0) if nq > 1 else ys[0]
    y = y + dexp_ref[...] * xs
    y_ref[...] = _rms(y * _silu(z), nrm_ref[...])


def _ssd_call(p, T, n_seq, blk0, l, params, state):
    return _mixer_call(functools.partial(_ssd_kernel, T, state[1]), "ssd_T%d" % T, p, SS_PAD, T,
                       n_seq, blk0, 1, l, params, (), state, (2, N_HEADS, HEAD_DIM, SSD_STATE),
                       [pltpu.VMEM((T, PAIR_W), F32), pltpu.VMEM((2, T, PAIR_W), F32),
                        pltpu.VMEM((T // SUBLANES, PAIR_W), F32)])


def _hgrn_kernel(T, has_init, p_ref, la_ref, l1_ref, lbd_ref, nrm_ref, bd_ref, st_ref, y_ref, *rest):
    fin_ref = None if has_init else rest[0]
    q_sc, v_sc, k_sc, b_sc, o_sc, t_sc = rest[-6:]
    NB = HG_BLOCK
    n_blk = T // NB

    def put(dst, val):
        for pr in range(N_PAIRS):
            dst[pr] = val[:, pr * PAIR_W:(pr + 1) * PAIR_W]

    put(q_sc, _silu(p_ref[:, 0:GROUP_W]))
    put(v_sc, p_ref[:, GROUP_W:2 * GROUP_W])
    for d in range(2):
        x = p_ref[:, (2 + d) * GROUP_W:(3 + d) * GROUP_W]
        e1 = jnp.exp(-jnp.abs(x))
        lsig = jnp.minimum(x, 0.0) - jnp.log(1.0 + e1)
        a_ = la_ref[d:d + 1, :] + jnp.zeros_like(x)
        b_ = l1_ref[d:d + 1, :] + lsig
        logf = jnp.maximum(a_, b_) + jnp.log(1.0 + jnp.exp(-jnp.abs(a_ - b_)))
        put(k_sc.at[d], jnp.exp(l1_ref[d:d + 1, :]) * (jnp.where(x >= 0.0, e1, 1.0) / (1.0 + e1))
            - lbd_ref[d:d + 1, :])
        put(b_sc.at[d], logf)
        for pr in range(N_PAIRS):
            _seg_cumsum_slab(b_sc.at[d, pr], b_sc.at[d, pr], t_sc, T, NB, rev=(d == 1))
    bd = bd_ref[0:PAIR_W, 0:PAIR_W]
    rows8 = _row_iota((SUBLANES, PAIR_W))
    bdm = (_row_iota((PAIR_W, PAIR_W)) < HEAD_DIM) == (_lane_iota((PAIR_W, PAIR_W)) < HEAD_DIM)
    chains = [(d, pr) for d in range(2) for pr in range(N_PAIRS)]
    n_half = NB // SUBLANES

    def body(j, states):
        new_states = []
        for d, pr in chains:
            jb = j if d == 0 else n_blk - 1 - j
            r0 = pl.multiple_of(jb * NB, NB)
            blk = pl.ds(r0, NB)
            qb, vb, kb, bb = q_sc[pr, blk, :], v_sc[pr, blk, :], k_sc[d, pr, blk, :], b_sc[d, pr, blk, :]

            def row(ref, s):
                return jnp.broadcast_to(ref[pl.ds(r0 + s, 1), :], (SUBLANES, PAIR_W))

            parts, where = [], []
            for s in range(NB):
                b_s, k_s = row(b_sc.at[d, pr], s), row(k_sc.at[d, pr], s)
                for hf in range(n_half):
                    t0 = hf * SUBLANES
                    t1 = t0 + SUBLANES - 1
                    if (t1 < s) if d == 0 else (t0 > s):
                        continue
                    rs = slice(t0, t0 + SUBLANES)
                    pr_ = qb[rs] * jnp.exp(bb[rs] - b_s) * k_s
                    if not ((t0 >= s) if d == 0 else (t1 <= s)):
                        keep = (rows8 + t0 >= s) if d == 0 else (rows8 + t0 <= s)
                        pr_ = jnp.where(keep, pr_, 0.0)
                    parts.append(pr_)
                    where.append((s, hf))
            att = _mm(jnp.concatenate(parts, axis=0), bd)
            halves = [jnp.zeros((SUBLANES, PAIR_W), F32) for _ in range(n_half)]
            for i, (s, hf) in enumerate(where):
                halves[hf] = halves[hf] + att[i * SUBLANES:(i + 1) * SUBLANES, :] * row(v_sc.at[pr], s)
            o = jnp.concatenate(halves, axis=0)
            bl = bb[NB - 1:NB, :] if d == 0 else bb[0:1, :]
            st = states[d * N_PAIRS + pr]
            o_sc[d, blk, pr * PAIR_W:(pr + 1) * PAIR_W] = o + _mm_nt(qb * jnp.exp(bb), st)
            new_states.append(st * jnp.exp(bl)
                              + jnp.where(bdm, _mm_tn(vb, kb * jnp.exp(bl - bb)), 0.0))
        return tuple(new_states)

    if has_init:
        states0 = tuple(_pair_blockdiag(st_ref[0, d, 2 * pr], st_ref[0, d, 2 * pr + 1]).T
                        for d, pr in chains)
    else:
        states0 = tuple(jnp.zeros((PAIR_W, PAIR_W), F32) for _ in chains)
    states = lax.fori_loop(0, n_blk, body, states0, unroll=HG_UNROLL)
    if not has_init:
        for (d, pr), s_ in zip(chains, states):
            kv = s_.T
            for hh in range(2):
                fin_ref[0, d, 2 * pr + hh] = _pair_diag(kv, hh)
    g = p_ref[:, 4 * GROUP_W:5 * GROUP_W]
    y_ref[...] = _rms(o_sc[0] + o_sc[1], nrm_ref[...]) * _silu(g)


def _hgrn_call(p, T, n_seq, blk0, l, params, bd, state):
    pw = pltpu.VMEM((N_PAIRS, T, PAIR_W), F32)
    pw2 = pltpu.VMEM((2, N_PAIRS, T, PAIR_W), F32)
    return _mixer_call(functools.partial(_hgrn_kernel, T, state[1]), "hgrn2_T%d" % T, p, HG_COLS, T,
                       n_seq, blk0, 1, l, params, (bd,), state, (2, N_HEADS, HEAD_DIM, HEAD_DIM),
                       [pw, pw, pw2, pw2, pltpu.VMEM((2, T, GROUP_W), F32),
                        pltpu.VMEM((T // SUBLANES, PAIR_W), F32)])


def _rwkv_kernel(T, nb, lat, p_ref, mu_ref, w0_ref, a0_ref, wl_ref, kk_ref, ka_ref, rk_ref,
                 lnw_ref, lnb_ref, sm_ref, bd_ref, st_ref, y_ref, *rest):
    fin_ref = None if lat else rest[0]
    xa_sc, xr_sc, yb_sc, yk_sc, c_sc, v_sc, yo_sc, cs_sc, ct_sc = rest[-9:]
    C = RW_CHUNK
    n_chunk = T // C
    P = p_ref[...]
    rowi = _row_iota(P.shape) % T
    if lat:
        gcol = rowi % GRID_W
        shifts = [(-1, gcol >= 1), (1, gcol <= GRID_W - 2), (-GRID_W, rowi >= GRID_W),
                  (GRID_W, rowi < T - GRID_W)]
    else:
        shifts = [(-1, rowi >= 1), (1, rowi <= T - 2)]
    sh = jnp.zeros_like(P)
    for j, (dlt, valid) in enumerate(shifts):
        sh = jnp.where(valid & (sm_ref[j:j + 1, :] > 0.5), _shift_rows(P, dlt), sh)
    P = P + (sh - P) * mu_ref[...]
    r = P[:, 0:GROUP_W]
    k = P[:, GROUP_W:2 * GROUP_W]
    v = P[:, 2 * GROUP_W:3 * GROUP_W]
    lo = P[:, 3 * GROUP_W:3 * GROUP_W + PAIR_W]
    ln = _lane_iota(lo.shape)
    n_t = 2 * RW_LORA_W
    n_i = 4 * RW_LORA_W
    n_s = n_i + RW_LORA_G
    act = jnp.where(ln < n_t, jnp.tanh(lo),
                    jnp.where(ln < n_i, lo, jnp.where(ln < n_s, _sigmoid(lo), 0.0)))
    lora = _mm(act, wl_ref[...])
    gate = lora[:, 4 * GROUP_W:5 * GROUP_W]
    bd = bd_ref[...]
    kkv = k * kk_ref[...]
    kkn = kkv / jnp.maximum(jnp.sqrt(_mm_split(kkv * kkv, bd)), 1e-12)
    v_sc[...] = v
    ksum = jnp.zeros_like(k)
    for d in range(2):
        w_lora = lora[:, d * GROUP_W:(d + 1) * GROUP_W]
        a_lora = lora[:, (2 + d) * GROUP_W:(3 + d) * GROUP_W]
        w_log = -_softplus(-(w0_ref[d:d + 1, :] + w_lora)) - 0.5
        logw = -jnp.exp(w_log)
        a_sig = _sigmoid(a0_ref[d:d + 1, :] + a_lora)
        k_d = k * (1.0 + (a_sig - 1.0) * ka_ref[...])
        ksum = ksum + k_d
        for pr in range(N_PAIRS):
            cs_sc[pr] = logw[:, pr * PAIR_W:(pr + 1) * PAIR_W]
            _seg_cumsum_slab(cs_sc.at[pr], cs_sc.at[pr], ct_sc, nb * T, C, rev=(d == 1))
        c = jnp.concatenate([cs_sc[pr] for pr in range(N_PAIRS)], axis=-1)
        einv = jnp.exp(-c)
        xa_sc[d] = (-kkn) * jnp.exp(c - logw)
        xr_sc[d] = r * jnp.exp(c)
        yb_sc[d] = (kkn * a_sig) * einv
        yk_sc[d] = k_d * einv
        c_sc[d] = c

    R2 = 2 * C
    assert R2 % PAIR_W == 0
    lane_lo = _lane_iota((C, PAIR_W)) < HEAD_DIM
    tt2 = _row_iota((R2, 2 * R2)) % C
    ss2 = _lane_iota((R2, 2 * R2)) % C
    n_fac = C.bit_length() - 1

    def stack2(x):
        return jnp.concatenate([jnp.where(lane_lo, x, 0.0), jnp.where(lane_lo, 0.0, x)], axis=0)

    chains = [(sq, d, pr) for sq in range(nb) for d in range(2) for pr in range(N_PAIRS)]
    each = lambda f, *cols: [f(*a) for a in zip(*cols)]

    def body(j, states):
        lhs, rhs_t, vst, pc, strict2, incl2, dst = [], [], [], [], [], [], []
        for sq, d, pr in chains:
            jc = j if d == 0 else n_chunk - 1 - j
            r0 = pl.multiple_of(sq * T + jc * C, C)
            ls = slice(pr * PAIR_W, (pr + 1) * PAIR_W)
            at = xa_sc[d, pl.ds(r0, C), ls]
            rt = xr_sc[d, pl.ds(r0, C), ls]
            bt = yb_sc[d, pl.ds(r0, C), ls]
            kt = yk_sc[d, pl.ds(r0, C), ls]
            cc = c_sc[d, pl.ds(r0, C), ls]
            pc.append(jnp.exp(cc[C - 1:C, :] if d == 0 else cc[0:1, :]))
            lhs.append(jnp.concatenate([stack2(at), stack2(rt)], axis=0))
            rhs_t.append(jnp.concatenate([stack2(bt), stack2(kt)], axis=0))
            vst.append(stack2(v_sc[pl.ds(r0, C), ls]))
            strict2.append((ss2 < tt2) if d == 0 else (ss2 > tt2))
            incl2.append((ss2 <= tt2) if d == 0 else (ss2 >= tt2))
            dst.append((d, r0, ls))
        gg = each(_mm_nt, lhs, rhs_t)
        uu = each(_mm_nt, lhs, states)
        am = each(lambda g, m: jnp.where(m, g[0:R2, :], 0.0), gg, strict2)
        ncat = each(lambda g, m: jnp.where(m, g[R2:2 * R2, :], 0.0), gg, incl2)
        qb = each(lambda a_: a_[:, 0:R2].astype(BF16), am)
        xs = each(lambda u, a_, v_: u[0:R2, :] + _mm(a_[:, R2:2 * R2], v_), uu, am, vst)
        for i in range(n_fac):
            xh = each(lambda x: x.astype(BF16), xs)
            xl = each(lambda x, h: (x - h.astype(F32)).astype(BF16), xs, xh)
            corr = each(lambda q, l_: jnp.dot(q, l_, preferred_element_type=F32), qb, xl)
            if i < n_fac - 1:
                zz = each(lambda q, h: jnp.dot(q, jnp.concatenate([h, q], axis=1),
                                               preferred_element_type=F32), qb, xh)
                xs = each(lambda x, z, c_: x + z[:, 0:PAIR_W] + c_, xs, zz, corr)
                qb = each(lambda z: z[:, PAIR_W:PAIR_W + R2].astype(BF16), zz)
            else:
                xs = each(lambda x, q, h, c_: x + jnp.dot(q, h, preferred_element_type=F32) + c_,
                          xs, qb, xh, corr)
        sv = each(lambda x, v_: jnp.concatenate([x, v_], axis=0), xs, vst)
        yst = each(lambda u, n, s_: u[R2:2 * R2, :] + _mm(n, s_), uu, ncat, sv)
        for (d, r0, ls), y_ in zip(dst, yst):
            yo_sc[d, pl.ds(r0, C), ls] = y_[0:C, :] + y_[C:2 * C, :]
        return tuple(each(lambda s_, p, v_, r_: s_ * p + _mm_tn(v_, r_ * p), states, pc, sv, rhs_t))

    if lat:
        states0 = tuple(_pair_blockdiag(st_ref[sq, d, 2 * pr], st_ref[sq, d, 2 * pr + 1])
                        for sq, d, pr in chains)
    else:
        states0 = tuple(jnp.zeros((PAIR_W, PAIR_W), F32) for _ in chains)
    states = lax.fori_loop(0, n_chunk, body, states0)
    if not lat:
        for (sq, d, pr), s_ in zip(chains, states):
            for hh in range(2):
                fin_ref[sq, d, 2 * pr + hh] = _pair_diag(s_, hh)
    y = yo_sc[0] + yo_sc[1]
    inv_n = 1.0 / HEAD_DIM
    mu = _mm_split(y, bd) * inv_n
    yc = y - mu
    var = _mm_split(yc * yc, bd) * inv_n
    yn = yc * lax.rsqrt(var + RW_GN_EPS) * lnw_ref[...] + lnb_ref[...]
    bonus = _mm_split(r * ksum * rk_ref[...], bd) * v
    y_ref[...] = (yn + bonus) * gate


def _rwkv_call(p, T, n_seq, blk0, nb, lat, l, params, sm, bd, state):
    tw = pltpu.VMEM((nb * T, GROUP_W), F32)
    tw2 = pltpu.VMEM((2, nb * T, GROUP_W), F32)
    return _mixer_call(functools.partial(_rwkv_kernel, T, nb, lat), "rwkv7_T%d" % T, p, RW_PAD, T,
                       n_seq, blk0, nb, l, params, (sm, bd), state,
                       (2, N_HEADS, HEAD_DIM, HEAD_DIM),
                       [tw2, tw2, tw2, tw2, tw2, tw, tw2,
                        pltpu.VMEM((N_PAIRS, nb * T, PAIR_W), F32),
                        pltpu.VMEM((nb * T // SUBLANES, PAIR_W), F32)])


def _pad_last(a, n):
    return jnp.pad(a, ((0, 0),) * (a.ndim - 1) + ((0, n - a.shape[-1]),))


def _row(a):
    return a.reshape(a.shape[0], 1, -1)


def _heads_blockdiag(w):
    eye = jnp.eye(N_HEADS, dtype=w.dtype)
    full = w[..., :, :, None, :] * eye[:, None, :, None]
    return full.reshape(w.shape[:-3] + (GROUP_W, GROUP_W))


def kernel(x_prompt, x_sample, state_rwkv, state_hgrn, state_ssd, state_lru, c, c_ctx, mod_w, mod_b, norm_g, w_in, w_out, rw_mu, rw_w0, rw_w2, rw_a0, rw_a2, rw_g2, rw_kk, rw_ka, rw_rk, rw_ln_w, rw_ln_b, hg_lb, hg_norm, ss_conv_w, ss_conv_b, ss_dt_bias, ss_A_log, ss_D, ss_norm, lr_conv_w, lr_conv_b, lr_wa, lr_ba, lr_wx, lr_bx, lr_lam, mlp_w1, mlp_w2):
    bp, t_ctx, _ = x_prompt.shape
    bl, t_lat, _ = x_sample.shape
    n_ctx_tok = bp * t_ctx
    assert n_ctx_tok % t_lat == 0 and t_lat % GRID_W == 0 and 1 + bl <= 8
    lat_blk0 = n_ctx_tok // t_lat
    tiles = _Tiles(n_ctx_tok, bl * t_lat, t_lat)

    x_ctx = x_prompt.reshape(n_ctx_tok, D_MODEL)
    x_lat = x_sample.reshape(bl * t_lat, D_MODEL)
    cond8 = jnp.zeros((8, D_MODEL), F32).at[0].set(c_ctx).at[1:1 + bl].set(c)
    mod = _mod_call(cond8, mod_w, mod_b).reshape(DEPTH, 8, MOD_CH, D_MODEL)

    o1 = RW_COLS
    o2 = o1 + HG_COLS
    o3 = o2 + SS_COLS
    w_in_b = w_in.astype(BF16)
    ws_in = (_pad_last(w_in_b[:, :, :o1], RW_PAD), w_in_b[:, :, o1:o2],
             _pad_last(w_in_b[:, :, o2:o3], SS_PAD), w_in_b[:, :, o3:])
    w_out_b, w1_b, w2_b = w_out.astype(BF16), mlp_w1.astype(BF16), mlp_w2.astype(BF16)

    lane = jnp.arange(GROUP_W)
    bd_ones = (lane[:, None] // HEAD_DIM == lane[None, :] // HEAD_DIM).astype(BF16)

    col = jnp.arange(RW_PAD)
    quarter = col // RW_SHIFT_SPLIT
    real = col < RW_COLS
    sm_lat = jnp.stack([(quarter == i) & real for i in range(4)]).astype(F32)
    sm_ctx = jnp.stack([(quarter < 2) & real, (quarter >= 2) & real,
                        jnp.zeros_like(real), jnp.zeros_like(real)]).astype(F32)
    lora_blocks = (rw_w2[:, 0], rw_w2[:, 1], rw_a2[:, 0], rw_a2[:, 1], rw_g2)
    wl = jnp.concatenate(
        [jnp.pad(blk, ((0, 0), (0, 0), (j * GROUP_W, (len(lora_blocks) - 1 - j) * GROUP_W)))
         for j, blk in enumerate(lora_blocks)], axis=1)
    wl = jnp.pad(wl, ((0, 0), (0, PAIR_W - wl.shape[1]), (0, 0))).astype(BF16)
    rw_params = (_pad_last(_row(rw_mu), RW_PAD), rw_w0, rw_a0, wl, _row(rw_kk), _row(rw_ka),
                 _row(rw_rk), _row(rw_ln_w), _row(rw_ln_b))

    lb_soft = jax.nn.softmax(hg_lb.astype(F32), axis=0)
    lower = jnp.cumsum(lb_soft, axis=0) - lb_soft[0]
    lb_floor = jnp.maximum(lower, LB_FLOOR)
    hg_params = (jnp.log(lb_floor), jnp.log1p(-lower), lb_floor - lower, _row(hg_norm))

    ss_params = (ss_conv_w, _row(ss_conv_b), _pad_last(_row(ss_dt_bias), PAIR_W),
                 _pad_last(_row(-jnp.exp(ss_A_log.astype(F32))), PAIR_W),
                 _row(jnp.repeat(ss_D, HEAD_DIM, axis=-1)), _row(ss_norm))

    wcat = jnp.concatenate([_heads_blockdiag(lr_wa[:, 0]), _heads_blockdiag(lr_wx[:, 0]),
                            _heads_blockdiag(lr_wa[:, 1]), _heads_blockdiag(lr_wx[:, 1])],
                           axis=-1).astype(BF16)
    bcat = _row(jnp.stack([lr_ba[:, 0], lr_bx[:, 0], lr_ba[:, 1], lr_bx[:, 1]], axis=1))
    lr_params = (lr_conv_w, _row(lr_conv_b), wcat, bcat, lr_lam)

    new_rw = jnp.zeros((bp, DEPTH, 2, N_HEADS, HEAD_DIM, HEAD_DIM), F32)
    new_hg = jnp.zeros((bp, DEPTH, 2, N_HEADS, HEAD_DIM, HEAD_DIM), F32)
    new_ss = jnp.zeros((bp, DEPTH, 2, N_HEADS, HEAD_DIM, SSD_STATE), F32)
    new_lr = jnp.zeros((bp, DEPTH, 2, GROUP_W), F32)

    for l in range(DEPTH):
        p_rw, p_hg, p_ss, p_lr = _inproj_call(tiles, l, x_ctx, x_lat, mod, norm_g, ws_in)

        y_rw_c, new_rw = _rwkv_call(p_rw, t_ctx, bp, 0, RW_CTX_SEQS, False, l, rw_params, sm_ctx,
                                    bd_ones, (new_rw, False))
        y_rw_l, = _rwkv_call(p_rw, t_lat, bl, lat_blk0, RW_LAT_SEQS, True, l, rw_params, sm_lat,
                             bd_ones, (state_rwkv, True))
        y_hg_c, new_hg = _hgrn_call(p_hg, t_ctx, bp, 0, l, hg_params, bd_ones, (new_hg, False))
        y_hg_l, = _hgrn_call(p_hg, t_lat, bl, lat_blk0, l, hg_params, bd_ones, (state_hgrn, True))
        y_ss_c, new_ss = _ssd_call(p_ss, t_ctx, bp, 0, l, ss_params, (new_ss, False))
        y_ss_l, = _ssd_call(p_ss, t_lat, bl, lat_blk0, l, ss_params, (state_ssd, True))
        y_lr_c, new_lr = _lru_call(p_lr, t_ctx, bp, 0, l, lr_params, (new_lr, False))
        y_lr_l, = _lru_call(p_lr, t_lat, bl, lat_blk0, l, lr_params, (state_lru, True))

        x_ctx, x_lat = _outmlp_call(tiles, l, x_ctx, x_lat, (y_rw_c, y_hg_c, y_ss_c, y_lr_c),
                                    (y_rw_l, y_hg_l, y_ss_l, y_lr_l), mod, norm_g,
                                    w_out_b, w1_b, w2_b)

    return (x_ctx.reshape(bp, t_ctx, D_MODEL), x_lat.reshape(bl, t_lat, D_MODEL),
            new_rw, new_hg, new_ss, new_lr)
```

```python
import functools

import jax
import jax.numpy as jnp
from jax import lax
from jax.experimental import pallas as pl
from jax.experimental.pallas import tpu as pltpu

F32 = jnp.float32
BF16 = jnp.bfloat16

D_MODEL = 1024
DEPTH = 4
MOD_CH = 6
GROUP_W = 256
HEAD_DIM = 64
N_HEADS = 4
N_PAIRS = 2
PAIR_W = 128
D_FF = 4096
GRID_W = 64
NORM_EPS = 1e-6
LB_FLOOR = 1e-30
RW_GN_EPS = 64e-5
LRU_C = 8.0
SSD_STATE = 128
CONV_TAPS = (-2, -1, 0, 1)

RW_COLS = 864
RW_PAD = 896
RW_SHIFT_SPLIT = RW_COLS // 4
RW_LORA_W = 16
RW_LORA_G = 32
HG_COLS = 1280
SS_COLS = 1032
SS_PAD = 1152
LR_COLS = 512
RW_CHUNK = 64
RW_SPLIT_STAGES = 4
RW_CTX_SEQS = 4
RW_LAT_SEQS = 2
HG_BLOCK = 16
HG_UNROLL = 4
SSD_CHUNK = 256
SUBLANES = 8
SCAN_BLOCK = SUBLANES

TOK_TILE = 512
VMEM_LIMIT = 56 * 1024 * 1024


def _cparams(sem):
    return pltpu.CompilerParams(dimension_semantics=sem, vmem_limit_bytes=VMEM_LIMIT)


def _layer_spec(a, l, **kw):
    return pl.BlockSpec((None,) + a.shape[1:], lambda *_: (l,) + (0,) * (a.ndim - 1), **kw)


def _const_spec(a):
    return pl.BlockSpec(a.shape, lambda *_: (0,) * a.ndim)


def _mm(a, b):
    return jnp.dot(a.astype(BF16), b.astype(BF16), preferred_element_type=F32)


def _mm_nt(a, b):
    return lax.dot_general(a.astype(BF16), b.astype(BF16), (((1,), (1,)), ((), ())),
                           preferred_element_type=F32)


def _mm_tn(a, b):
    return lax.dot_general(a.astype(BF16), b.astype(BF16), (((0,), (0,)), ((), ())),
                           preferred_element_type=F32)


def _mm_split(a, b_bf16):
    hi = a.astype(BF16)
    lo = (a - hi.astype(F32)).astype(BF16)
    return (jnp.dot(hi, b_bf16, preferred_element_type=F32)
            + jnp.dot(lo, b_bf16, preferred_element_type=F32))


def _row_iota(shape):
    return lax.broadcasted_iota(jnp.int32, shape, 0)


def _lane_iota(shape):
    return lax.broadcasted_iota(jnp.int32, shape, 1)


def _shift_rows(x, d):
    n = x.shape[0]
    s = (-d) % n
    if s == 0:
        return x
    return pltpu.roll(x, s, 0)


def _seg_cumsum_slab(src, dst, tot, n, seg, rev):
    sb = SUBLANES
    nb = n // sb
    g = seg // sb
    order = list(range(sb))[::-1] if rev else list(range(sb))
    part, prev = {}, None
    for r in order:
        x_r = src[pl.ds(r, nb, stride=sb), :]
        part[r] = x_r if prev is None else part[prev] + x_r
        prev = r
    off = None
    if g > 1:
        tot[...] = part[prev]
        ng = nb // g
        qorder = list(range(g))[::-1] if rev else list(range(g))
        run, excl = None, {}
        for q in qorder:
            excl[q] = run
            t_q = tot[pl.ds(q, ng, stride=g), :]
            run = t_q if run is None else run + t_q
        for q in qorder:
            tot[pl.ds(q, ng, stride=g), :] = jnp.zeros((ng, PAIR_W), F32) if excl[q] is None else excl[q]
        off = tot[...]
    for r in range(sb):
        dst[pl.ds(r, nb, stride=sb), :] = part[r] if off is None else part[r] + off


def _sigmoid(x):
    return jax.nn.sigmoid(x)


def _silu(x):
    return x * _sigmoid(x)


def _softplus(x):
    return jnp.maximum(x, 0.0) + jnp.log(1.0 + jnp.exp(-jnp.abs(x)))


def _gelu_tanh(x):
    c = 0.7978845608028654
    return 0.5 * x * (1.0 + jnp.tanh(c * (x + 0.044715 * (x * x * x))))


def _rms(x, g):
    return x * lax.rsqrt(jnp.mean(x * x, axis=-1, keepdims=True) + NORM_EPS) * g


def _conv_rows(x, w_ref, b_ref):
    t = x.shape[0]
    rows = _row_iota(x.shape)
    y = b_ref[...] + jnp.zeros_like(x)
    for j, d in enumerate(CONV_TAPS):
        valid = (rows + d >= 0) & (rows + d < t)
        y = y + jnp.where(valid, _shift_rows(x, d), 0.0) * w_ref[j:j + 1, :]
    return y


def _pair_blockdiag(a, b):
    z = jnp.zeros_like(a)
    return jnp.concatenate([jnp.concatenate([a, z], axis=1), jnp.concatenate([z, b], axis=1)], axis=0)


def _pair_diag(m, hh):
    return m[hh * HEAD_DIM:(hh + 1) * HEAD_DIM, hh * HEAD_DIM:(hh + 1) * HEAD_DIM]


def _mod_kernel(c_ref, w_ref, b_ref, o_ref):
    c = c_ref[...]
    o_ref[...] = _mm(_silu(c), w_ref[...]) + b_ref[...]


def _mod_call(cond8, mod_w, mod_b):
    return pl.pallas_call(
        _mod_kernel,
        grid=(DEPTH, MOD_CH),
        in_specs=[
            pl.BlockSpec((8, D_MODEL), lambda l, j: (0, 0)),
            pl.BlockSpec((None, D_MODEL, D_MODEL), lambda l, j: (l, 0, j)),
            pl.BlockSpec((None, 1, D_MODEL), lambda l, j: (l, 0, j)),
        ],
        out_specs=pl.BlockSpec((None, 8, D_MODEL), lambda l, j: (l, 0, j)),
        out_shape=jax.ShapeDtypeStruct((DEPTH, 8, MOD_CH * D_MODEL), F32),
        compiler_params=_cparams(("arbitrary", "arbitrary")),
        name="modulation",
    )(cond8, mod_w, mod_b.reshape(DEPTH, 1, MOD_CH * D_MODEL))


class _Tiles:
    def __init__(self, n_ctx_tok, n_lat_tok, t_lat):
        assert n_ctx_tok % TOK_TILE == 0 and t_lat % TOK_TILE == 0
        self.n_ctx = n_ctx_tok // TOK_TILE
        self.n_lat = n_lat_tok // TOK_TILE
        self.per_seq = t_lat // TOK_TILE
        self.n = self.n_ctx + self.n_lat

    def ctx_idx(self, i):
        return jnp.minimum(i, self.n_ctx - 1)

    def lat_idx(self, i):
        return jnp.clip(i - self.n_ctx, 0, self.n_lat - 1)

    def seq_of(self, i):
        return jnp.where(i < self.n_ctx, 0, 1 + (i - self.n_ctx) // self.per_seq)


def _inproj_kernel(n_ctx_tiles, xc_ref, xl_ref, mod_ref, g_ref, wrw, whg, wss, wlr,
                   orw, ohg, oss, olr):
    x = jnp.where(pl.program_id(0) < n_ctx_tiles, xc_ref[...], xl_ref[...])
    m = mod_ref[...]
    h = _rms(x, g_ref[0:1, :]) * (1.0 + m[1:2, :]) + m[0:1, :]
    hb = h.astype(BF16)
    orw[...] = jnp.dot(hb, wrw[...], preferred_element_type=F32)
    ohg[...] = jnp.dot(hb, whg[...], preferred_element_type=F32)
    oss[...] = jnp.dot(hb, wss[...], preferred_element_type=F32)
    olr[...] = jnp.dot(hb, wlr[...], preferred_element_type=F32)


def _inproj_call(tiles, l, x_ctx, x_lat, mod, norm_g, ws):
    widths = (RW_PAD, HG_COLS, SS_PAD, LR_COLS)
    n_tok = tiles.n * TOK_TILE
    return pl.pallas_call(
        functools.partial(_inproj_kernel, tiles.n_ctx),
        grid=(tiles.n,),
        in_specs=[
            pl.BlockSpec((TOK_TILE, D_MODEL), lambda i: (tiles.ctx_idx(i), 0)),
            pl.BlockSpec((TOK_TILE, D_MODEL), lambda i: (tiles.lat_idx(i), 0)),
            pl.BlockSpec((None, None, MOD_CH, D_MODEL), lambda i: (l, tiles.seq_of(i), 0, 0)),
            _layer_spec(norm_g, l),
        ] + [_layer_spec(w, l) for w in ws],
        out_specs=[pl.BlockSpec((TOK_TILE, w), lambda i: (i, 0)) for w in widths],
        out_shape=[jax.ShapeDtypeStruct((n_tok, w), F32) for w in widths],
        compiler_params=_cparams(("arbitrary",)),
        name="in_proj",
    )(x_ctx, x_lat, mod, norm_g, *ws)


def _outmlp_kernel(n_ctx_tiles, xc_ref, xl_ref, c0, c1, c2, c3, l0, l1, l2, l3, mod_ref, g_ref,
                   wout, w1, w2, oc_ref, ol_ref):
    is_ctx = pl.program_id(0) < n_ctx_tiles
    mix = jnp.concatenate(
        [jnp.where(is_ctx, c[...], l[...]) for c, l in ((c0, l0), (c1, l1), (c2, l2), (c3, l3))],
        axis=-1).astype(BF16)
    m = mod_ref[...]
    g = g_ref[...]
    x = jnp.where(is_ctx, xc_ref[...], xl_ref[...])
    t = jnp.dot(mix, wout[...], preferred_element_type=F32)
    x1 = x + m[2:3, :] * _rms(t, g[1:2, :])
    h2 = (_rms(x1, g[2:3, :]) * (1.0 + m[4:5, :]) + m[3:4, :]).astype(BF16)
    acc = jnp.zeros_like(x)
    for f in range(D_FF // D_MODEL):
        u = jnp.dot(h2, w1[:, f * D_MODEL:(f + 1) * D_MODEL], preferred_element_type=F32)
        u = jnp.square(jnp.maximum(u, 0.0)).astype(BF16)
        acc = acc + jnp.dot(u, w2[f * D_MODEL:(f + 1) * D_MODEL, :], preferred_element_type=F32)
    res = x1 + m[5:6, :] * _rms(acc, g[3:4, :])

    @pl.when(is_ctx)
    def _():
        oc_ref[...] = res

    @pl.when(jnp.logical_not(is_ctx))
    def _():
        ol_ref[...] = res


def _outmlp_call(tiles, l, x_ctx, x_lat, ys_ctx, ys_lat, mod, norm_g, wout, w1, w2):
    ctx_x = pl.BlockSpec((TOK_TILE, D_MODEL), lambda i: (tiles.ctx_idx(i), 0))
    lat_x = pl.BlockSpec((TOK_TILE, D_MODEL), lambda i: (tiles.lat_idx(i), 0))
    ctx_y = pl.BlockSpec((TOK_TILE, GROUP_W), lambda i: (tiles.ctx_idx(i), 0))
    lat_y = pl.BlockSpec((TOK_TILE, GROUP_W), lambda i: (tiles.lat_idx(i), 0))
    single = dict(pipeline_mode=pl.Buffered(1))
    return pl.pallas_call(
        functools.partial(_outmlp_kernel, tiles.n_ctx),
        grid=(tiles.n,),
        in_specs=[ctx_x, lat_x] + [ctx_y] * 4 + [lat_y] * 4
        + [pl.BlockSpec((None, None, MOD_CH, D_MODEL), lambda i: (l, tiles.seq_of(i), 0, 0)),
           _layer_spec(norm_g, l), _layer_spec(wout, l, **single), _layer_spec(w1, l, **single),
           _layer_spec(w2, l, **single)],
        out_specs=[ctx_x, lat_x],
        out_shape=[jax.ShapeDtypeStruct(x_ctx.shape, F32), jax.ShapeDtypeStruct(x_lat.shape, F32)],
        compiler_params=_cparams(("arbitrary",)),
        name="out_mlp",
    )(x_ctx, x_lat, *ys_ctx, *ys_lat, mod, norm_g, wout, w1, w2)


def _mixer_call(kernel_fn, name, p, width, T, n_seq, blk0, nb, l, params, consts, state, state_blk,
                scratch, single_buffer_p=False):
    assert n_seq % nb == 0 and blk0 % nb == 0
    st_arr, st_in = state
    st_spec = pl.BlockSpec((nb, None) + state_blk, lambda i: (i, l) + (0,) * len(state_blk))
    p_mode = dict(pipeline_mode=pl.Buffered(1)) if single_buffer_p else {}
    in_specs = [pl.BlockSpec((nb * T, width), lambda i: (blk0 // nb + i, 0), **p_mode)]
    in_specs += [_layer_spec(a, l) for a in params] + [_const_spec(a) for a in consts]
    y_spec = pl.BlockSpec((nb * T, GROUP_W), lambda i: (i, 0))
    y_shape = jax.ShapeDtypeStruct((n_seq * T, GROUP_W), F32)
    if st_in:
        in_specs.append(st_spec)
        out_specs, out_shape, aliases = [y_spec], [y_shape], {}
    else:
        in_specs.append(pl.BlockSpec(memory_space=pl.ANY))
        out_specs = [y_spec, st_spec]
        out_shape = [y_shape, jax.ShapeDtypeStruct(st_arr.shape, st_arr.dtype)]
        aliases = {len(in_specs) - 1: 1}
    return pl.pallas_call(
        kernel_fn,
        grid=(n_seq // nb,),
        in_specs=in_specs,
        out_specs=out_specs,
        out_shape=out_shape,
        input_output_aliases=aliases,
        scratch_shapes=scratch,
        compiler_params=_cparams(("arbitrary",)),
        name=name,
    )(p, *params, *consts, st_arr)


def _lru_kernel(T, has_init, p_ref, cw_ref, cb_ref, wcat_ref, bcat_ref, lam_ref, st_ref, y_ref, *rest):
    fin_ref = None if has_init else rest[0]
    a_sc, b_sc, h_sc, al_sc, bl_sc, c_sc = rest[-6:]
    SB = SCAN_BLOCK
    n_blk = T // SB
    n_slab = GROUP_W // PAIR_W
    ybr = p_ref[:, 0:GROUP_W]
    xbr = p_ref[:, GROUP_W:2 * GROUP_W]
    xc = _conv_rows(xbr, cw_ref, cb_ref)
    gates = _sigmoid(_mm(xc, wcat_ref[...]) + bcat_ref[...])
    sp = _softplus(-lam_ref[...])
    chains = [(d, hv) for d in range(2) for hv in range(n_slab)]
    for d in range(2):
        r = gates[:, (2 * d) * GROUP_W:(2 * d + 1) * GROUP_W]
        ig = gates[:, (2 * d + 1) * GROUP_W:(2 * d + 2) * GROUP_W]
        log_a = (-LRU_C) * r * sp[d:d + 1, :]
        a = jnp.exp(log_a)
        b = jnp.sqrt(jnp.maximum(1.0 - a * a, 0.0)) * (ig * xc)
        for hv in range(n_slab):
            a_sc[d, hv] = a[:, hv * PAIR_W:(hv + 1) * PAIR_W]
            b_sc[d, hv] = b[:, hv * PAIR_W:(hv + 1) * PAIR_W]

    def rows_of(ref, r):
        return ref[pl.ds(r, n_blk, stride=SB), :]

    acc, bcc = {}, {}
    for d, hv in chains:
        order = list(range(SB)) if d == 0 else list(range(SB - 1, -1, -1))
        prev = None
        for r in order:
            a_r, b_r = rows_of(a_sc.at[d, hv], r), rows_of(b_sc.at[d, hv], r)
            if prev is None:
                acc[d, hv, r], bcc[d, hv, r] = a_r, b_r
            else:
                acc[d, hv, r] = a_r * acc[d, hv, prev]
                bcc[d, hv, r] = a_r * bcc[d, hv, prev] + b_r
            prev = r
        al_sc[d, hv] = acc[d, hv, prev]
        bl_sc[d, hv] = bcc[d, hv, prev]

    def body(j, carry):
        new = []
        for (d, hv), c in zip(chains, carry):
            jb = j if d == 0 else n_blk - 1 - j
            c_sc[d, hv, pl.ds(jb, 1), :] = c
            new.append(al_sc[d, hv, pl.ds(jb, 1), :] * c + bl_sc[d, hv, pl.ds(jb, 1), :])
        return tuple(new)

    if has_init:
        carry0 = tuple(st_ref[0, d:d + 1, hv * PAIR_W:(hv + 1) * PAIR_W] for d, hv in chains)
    else:
        carry0 = tuple(jnp.zeros((1, PAIR_W), F32) for _ in chains)
    last = lax.fori_loop(0, n_blk, body, carry0, unroll=SB)
    for (d, hv), c in zip(chains, last):
        if not has_init:
            fin_ref[0, d:d + 1, hv * PAIR_W:(hv + 1) * PAIR_W] = c
        cin = c_sc[d, hv]
        for r in range(SB):
            h_sc[d, hv, pl.ds(r, n_blk, stride=SB), :] = acc[d, hv, r] * cin + bcc[d, hv, r]
    total = jnp.concatenate([h_sc[0, hv] + h_sc[1, hv] for hv in range(n_slab)], axis=-1)
    y_ref[...] = total * _gelu_tanh(ybr)


def _lru_call(p, T, n_seq, blk0, l, params, state):
    n_slab = GROUP_W // PAIR_W
    full = pltpu.VMEM((2, n_slab, T, PAIR_W), F32)
    edge = pltpu.VMEM((2, n_slab, T // SCAN_BLOCK, PAIR_W), F32)
    return _mixer_call(functools.partial(_lru_kernel, T, state[1]), "rglru_T%d" % T, p, LR_COLS, T,
                       n_seq, blk0, 1, l, params, (), state, (2, GROUP_W),
                       [full, full, full, edge, edge, edge])


def _ssd_kernel(T, has_init, p_ref, cw_ref, cb_ref, dtb_ref, aneg_ref, dexp_ref, nrm_ref, st_ref,
                y_ref, *rest):
    fin_ref = None if has_init else rest[0]
    Q = SSD_CHUNK
    nq = T // Q
    z = p_ref[:, 0:GROUP_W]
    xbc = _silu(_conv_rows(p_ref[:, GROUP_W:4 * GROUP_W], cw_ref, cb_ref))
    xs = xbc[:, 0:GROUP_W]
    bm = xbc[:, GROUP_W:2 * GROUP_W]
    cm = xbc[:, 2 * GROUP_W:3 * GROUP_W]
    dt = _softplus(p_ref[:, 4 * GROUP_W:4 * GROUP_W + PAIR_W] + dtb_ref[...])
    da = dt * aneg_ref[...]
    da_sc, ac_sc, ct_sc = rest[-3:]
    da_sc[...] = da
    _seg_cumsum_slab(da_sc, ac_sc.at[0], ct_sc, T, Q, rev=False)
    _seg_cumsum_slab(da_sc, ac_sc.at[1], ct_sc, T, Q, rev=True)
    acf = ac_sc[0]
    acb = ac_sc[1]
    ti = _row_iota((Q, Q))
    si = _lane_iota((Q, Q))
    low = si <= ti
    upp = si >= ti
    lane_lo = _lane_iota((Q, PAIR_W)) < HEAD_DIM
    row_lo = _row_iota((PAIR_W, SSD_STATE)) < HEAD_DIM

    def col(a, j):
        return a[:, j:j + 1]

    ydiag, ds, dec, ecol = [], [], [], []
    for q in range(nq):
        sl = slice(q * Q, (q + 1) * Q)
        acf_q, acb_q, dt_q = acf[sl], acb[sl], dt[sl]
        acf_t, acb_t, ldt_t = acf_q.T, acb_q.T, jnp.log(dt_q).T
        y_q, ds_q, dec_q, ec_q = [], [], [], []
        for pr in range(N_PAIRS):
            ls = slice(pr * PAIR_W, (pr + 1) * PAIR_W)
            bg, cg, xp = bm[sl, ls], cm[sl, ls], xs[sl, ls]
            g = _mm_nt(cg, bg)
            outs = []
            wf_cols, wb_cols, ef_cols, eb_cols, decf, decb = [], [], [], [], [], []
            for hh in range(2):
                h = 2 * pr + hh
                hb_ = N_HEADS + h
                lf = jnp.exp(col(acf_q, h) - (acf_t[h:h + 1, :] - ldt_t[h:h + 1, :]))
                lb = jnp.exp(col(acb_q, hb_) - (acb_t[hb_:hb_ + 1, :] - ldt_t[hb_:hb_ + 1, :]))
                mh = g * (jnp.where(low, lf, 0.0) + jnp.where(upp, lb, 0.0))
                outs.append(_mm(mh, xp))
                af_last = acf_q[Q - 1:Q, h:h + 1]
                ab_last = acb_q[0:1, hb_:hb_ + 1]
                wf_cols.append(jnp.exp(af_last - col(acf_q, h)) * col(dt_q, h))
                wb_cols.append(jnp.exp(ab_last - col(acb_q, hb_)) * col(dt_q, hb_))
                ef_cols.append(jnp.exp(col(acf_q, h)))
                eb_cols.append(jnp.exp(col(acb_q, hb_)))
                decf.append(jnp.exp(af_last))
                decb.append(jnp.exp(ab_last))
            y_q.append(jnp.where(lane_lo, outs[0], outs[1]))
            wf = jnp.where(lane_lo, wf_cols[0], wf_cols[1])
            wb = jnp.where(lane_lo, wb_cols[0], wb_cols[1])
            ds_q.append((_mm_tn(xp * wf, bg), _mm_tn(xp * wb, bg)))
            dec_q.append((jnp.where(row_lo, decf[0], decf[1]), jnp.where(row_lo, decb[0], decb[1])))
            ec_q.append((jnp.where(lane_lo, ef_cols[0], ef_cols[1]),
                         jnp.where(lane_lo, eb_cols[0], eb_cols[1])))
        ydiag.append(y_q)
        ds.append(ds_q)
        dec.append(dec_q)
        ecol.append(ec_q)

    prev = [[[None, None] for _ in range(N_PAIRS)] for _ in range(nq)]
    for pr in range(N_PAIRS):
        for d in range(2):
            if has_init:
                s = jnp.concatenate([st_ref[0, d, 2 * pr], st_ref[0, d, 2 * pr + 1]], axis=0)
            else:
                s = jnp.zeros((PAIR_W, SSD_STATE), F32)
            order = range(nq) if d == 0 else range(nq - 1, -1, -1)
            for q in order:
                prev[q][pr][d] = s
                s = dec[q][pr][d] * s + ds[q][pr][d]
            if not has_init:
                fin_ref[0, d, 2 * pr] = s[0:HEAD_DIM, :]
                fin_ref[0, d, 2 * pr + 1] = s[HEAD_DIM:PAIR_W, :]

    ys = []
    for q in range(nq):
        sl = slice(q * Q, (q + 1) * Q)
        parts = []
        for pr in range(N_PAIRS):
            ls = slice(pr * PAIR_W, (pr + 1) * PAIR_W)
            yp = ydiag[q][pr]
            if has_init or nq > 1:
                cg = cm[sl, ls]
                for d in range(2):
                    yp = yp + _mm_nt(cg, prev[q][pr][d]) * ecol[q][pr][d]
            parts.append(yp)
        ys.append(jnp.concatenate(parts, axis=-1))
    y = jnp.concatenate(ys, axis=0) if nq > 1 else ys[0]
    y = y + dexp_ref[...] * xs
    y_ref[...] = _rms(y * _silu(z), nrm_ref[...])


def _ssd_call(p, T, n_seq, blk0, l, params, state):
    return _mixer_call(functools.partial(_ssd_kernel, T, state[1]), "ssd_T%d" % T, p, SS_PAD, T,
                       n_seq, blk0, 1, l, params, (), state, (2, N_HEADS, HEAD_DIM, SSD_STATE),
                       [pltpu.VMEM((T, PAIR_W), F32), pltpu.VMEM((2, T, PAIR_W), F32),
                        pltpu.VMEM((T // SUBLANES, PAIR_W), F32)])


def _hgrn_kernel(T, has_init, p_ref, la_ref, l1_ref, lbd_ref, nrm_ref, bd_ref, st_ref, y_ref, *rest):
    fin_ref = None if has_init else rest[0]
    q_sc, v_sc, k_sc, b_sc, o_sc, t_sc = rest[-6:]
    NB = HG_BLOCK
    n_blk = T // NB

    def put(dst, val):
        for pr in range(N_PAIRS):
            dst[pr] = val[:, pr * PAIR_W:(pr + 1) * PAIR_W]

    put(q_sc, _silu(p_ref[:, 0:GROUP_W]))
    put(v_sc, p_ref[:, GROUP_W:2 * GROUP_W])
    for d in range(2):
        x = p_ref[:, (2 + d) * GROUP_W:(3 + d) * GROUP_W]
        e1 = jnp.exp(-jnp.abs(x))
        lsig = jnp.minimum(x, 0.0) - jnp.log(1.0 + e1)
        a_ = la_ref[d:d + 1, :] + jnp.zeros_like(x)
        b_ = l1_ref[d:d + 1, :] + lsig
        logf = jnp.maximum(a_, b_) + jnp.log(1.0 + jnp.exp(-jnp.abs(a_ - b_)))
        put(k_sc.at[d], jnp.exp(l1_ref[d:d + 1, :]) * (jnp.where(x >= 0.0, e1, 1.0) / (1.0 + e1))
            - lbd_ref[d:d + 1, :])
        put(b_sc.at[d], logf)
        for pr in range(N_PAIRS):
            _seg_cumsum_slab(b_sc.at[d, pr], b_sc.at[d, pr], t_sc, T, NB, rev=(d == 1))
    bd = bd_ref[0:PAIR_W, 0:PAIR_W]
    rows8 = _row_iota((SUBLANES, PAIR_W))
    bdm = (_row_iota((PAIR_W, PAIR_W)) < HEAD_DIM) == (_lane_iota((PAIR_W, PAIR_W)) < HEAD_DIM)
    chains = [(d, pr) for d in range(2) for pr in range(N_PAIRS)]
    n_half = NB // SUBLANES

    def body(j, states):
        new_states = []
        for d, pr in chains:
            jb = j if d == 0 else n_blk - 1 - j
            r0 = pl.multiple_of(jb * NB, NB)
            blk = pl.ds(r0, NB)
            qb, vb, kb, bb = q_sc[pr, blk, :], v_sc[pr, blk, :], k_sc[d, pr, blk, :], b_sc[d, pr, blk, :]

            def row(ref, s):
                return jnp.broadcast_to(ref[pl.ds(r0 + s, 1), :], (SUBLANES, PAIR_W))

            parts, where = [], []
            for s in range(NB):
                b_s, k_s = row(b_sc.at[d, pr], s), row(k_sc.at[d, pr], s)
                for hf in range(n_half):
                    t0 = hf * SUBLANES
                    t1 = t0 + SUBLANES - 1
                    if (t1 < s) if d == 0 else (t0 > s):
                        continue
                    rs = slice(t0, t0 + SUBLANES)
                    pr_ = qb[rs] * jnp.exp(bb[rs] - b_s) * k_s
                    if not ((t0 >= s) if d == 0 else (t1 <= s)):
                        keep = (rows8 + t0 >= s) if d == 0 else (rows8 + t0 <= s)
                        pr_ = jnp.where(keep, pr_, 0.0)
                    parts.append(pr_)
                    where.append((s, hf))
            att = _mm(jnp.concatenate(parts, axis=0), bd)
            halves = [jnp.zeros((SUBLANES, PAIR_W), F32) for _ in range(n_half)]
            for i, (s, hf) in enumerate(where):
                halves[hf] = halves[hf] + att[i * SUBLANES:(i + 1) * SUBLANES, :] * row(v_sc.at[pr], s)
            o = jnp.concatenate(halves, axis=0)
            bl = bb[NB - 1:NB, :] if d == 0 else bb[0:1, :]
            st = states[d * N_PAIRS + pr]
            o_sc[d, blk, pr * PAIR_W:(pr + 1) * PAIR_W] = o + _mm_nt(qb * jnp.exp(bb), st)
            new_states.append(st * jnp.exp(bl)
                              + jnp.where(bdm, _mm_tn(vb, kb * jnp.exp(bl - bb)), 0.0))
        return tuple(new_states)

    if has_init:
        states0 = tuple(_pair_blockdiag(st_ref[0, d, 2 * pr], st_ref[0, d, 2 * pr + 1]).T
                        for d, pr in chains)
    else:
        states0 = tuple(jnp.zeros((PAIR_W, PAIR_W), F32) for _ in chains)
    states = lax.fori_loop(0, n_blk, body, states0, unroll=HG_UNROLL)
    if not has_init:
        for (d, pr), s_ in zip(chains, states):
            kv = s_.T
            for hh in range(2):
                fin_ref[0, d, 2 * pr + hh] = _pair_diag(kv, hh)
    g = p_ref[:, 4 * GROUP_W:5 * GROUP_W]
    y_ref[...] = _rms(o_sc[0] + o_sc[1], nrm_ref[...]) * _silu(g)


def _hgrn_call(p, T, n_seq, blk0, l, params, bd, state):
    pw = pltpu.VMEM((N_PAIRS, T, PAIR_W), F32)
    pw2 = pltpu.VMEM((2, N_PAIRS, T, PAIR_W), F32)
    return _mixer_call(functools.partial(_hgrn_kernel, T, state[1]), "hgrn2_T%d" % T, p, HG_COLS, T,
                       n_seq, blk0, 1, l, params, (bd,), state, (2, N_HEADS, HEAD_DIM, HEAD_DIM),
                       [pw, pw, pw2, pw2, pltpu.VMEM((2, T, GROUP_W), F32),
                        pltpu.VMEM((T // SUBLANES, PAIR_W), F32)])


def _rwkv_kernel(T, nb, lat, p_ref, mu_ref, w0_ref, a0_ref, wl_ref, kk_ref, ka_ref, rk_ref,
                 lnw_ref, lnb_ref, sm_ref, bd_ref, st_ref, y_ref, *rest):
    fin_ref = None if lat else rest[0]
    xa_sc, xr_sc, yb_sc, yk_sc, pc_sc, v_sc, yo_sc, g_sc, rk_sc, cs_sc, ct_sc = rest[-11:]
    C = RW_CHUNK
    n_chunk = T // C
    bd = bd_ref[...]
    rowi = _row_iota((T, RW_PAD))
    if lat:
        gcol = rowi % GRID_W
        shifts = [(-1, gcol >= 1), (1, gcol <= GRID_W - 2), (-GRID_W, rowi >= GRID_W),
                  (GRID_W, rowi < T - GRID_W)]
    else:
        shifts = [(-1, rowi >= 1), (1, rowi <= T - 2)]
    ln = _lane_iota((T, PAIR_W))
    n_t = 2 * RW_LORA_W
    n_i = 4 * RW_LORA_W
    n_s = n_i + RW_LORA_G
    for sq in range(nb):
        rows = slice(sq * T, (sq + 1) * T)
        P = p_ref[rows, :]
        sh = jnp.zeros_like(P)
        for j, (dlt, valid) in enumerate(shifts):
            sh = jnp.where(valid & (sm_ref[j:j + 1, :] > 0.5), _shift_rows(P, dlt), sh)
        P = P + (sh - P) * mu_ref[...]
        r = P[:, 0:GROUP_W]
        k = P[:, GROUP_W:2 * GROUP_W]
        lo = P[:, 3 * GROUP_W:3 * GROUP_W + PAIR_W]
        act = jnp.where(ln < n_t, jnp.tanh(lo),
                        jnp.where(ln < n_i, lo, jnp.where(ln < n_s, _sigmoid(lo), 0.0)))
        lora = _mm(act, wl_ref[...])
        g_sc[rows, :] = lora[:, 4 * GROUP_W:5 * GROUP_W]
        kkv = k * kk_ref[...]
        kkn = kkv / jnp.maximum(jnp.sqrt(_mm_split(kkv * kkv, bd)), 1e-12)
        v_sc[rows, :] = P[:, 2 * GROUP_W:3 * GROUP_W]
        ksum = jnp.zeros_like(k)
        for d in range(2):
            w_lora = lora[:, d * GROUP_W:(d + 1) * GROUP_W]
            a_lora = lora[:, (2 + d) * GROUP_W:(3 + d) * GROUP_W]
            w_log = -_softplus(-(w0_ref[d:d + 1, :] + w_lora)) - 0.5
            logw = -jnp.exp(w_log)
            a_sig = _sigmoid(a0_ref[d:d + 1, :] + a_lora)
            k_d = k * (1.0 + (a_sig - 1.0) * ka_ref[...])
            ksum = ksum + k_d
            edge = C - 1 if d == 0 else 0
            for pr in range(N_PAIRS):
                ls = slice(pr * PAIR_W, (pr + 1) * PAIR_W)
                cs_sc[pr] = logw[:, ls]
                _seg_cumsum_slab(cs_sc.at[pr], cs_sc.at[pr], ct_sc, T, C, rev=(d == 1))
                pc_sc[d, pr, sq * n_chunk:(sq + 1) * n_chunk, :] = jnp.exp(
                    cs_sc[pr, pl.ds(edge, n_chunk, stride=C), :])
            c = jnp.concatenate([cs_sc[pr] for pr in range(N_PAIRS)], axis=-1)
            einv = jnp.exp(-c)
            xa_sc[d, rows, :] = ((-kkn) * jnp.exp(c - logw)).astype(BF16)
            xr_sc[d, rows, :] = (r * jnp.exp(c)).astype(BF16)
            yb_sc[d, rows, :] = ((kkn * a_sig) * einv).astype(BF16)
            yk_sc[d, rows, :] = (k_d * einv).astype(BF16)
        rk_sc[rows, :] = r * ksum * rk_ref[...]

    R2 = 2 * C
    assert R2 % PAIR_W == 0
    lane_lo = _lane_iota((C, PAIR_W)) < HEAD_DIM
    tt2 = _row_iota((R2, 2 * R2)) % C
    ss2 = _lane_iota((R2, 2 * R2)) % C
    n_fac = C.bit_length() - 1

    def stack2(x):
        z = jnp.zeros_like(x)
        return jnp.concatenate([jnp.where(lane_lo, x, z), jnp.where(lane_lo, z, x)], axis=0)

    chains = [(sq, d, pr) for sq in range(nb) for d in range(2) for pr in range(N_PAIRS)]
    each = lambda f, *cols: [f(*a) for a in zip(*cols)]

    def body(j, states):
        lhs, rhs_t, vst, pc, strict2, incl2, dst = [], [], [], [], [], [], []
        for sq, d, pr in chains:
            jc = j if d == 0 else n_chunk - 1 - j
            r0 = pl.multiple_of(sq * T + jc * C, C)
            ls = slice(pr * PAIR_W, (pr + 1) * PAIR_W)
            at = xa_sc[d, pl.ds(r0, C), ls]
            rt = xr_sc[d, pl.ds(r0, C), ls]
            bt = yb_sc[d, pl.ds(r0, C), ls]
            kt = yk_sc[d, pl.ds(r0, C), ls]
            pc.append(pc_sc[d, pr, pl.ds(sq * n_chunk + jc, 1), :])
            lhs.append(jnp.concatenate([stack2(at), stack2(rt)], axis=0))
            rhs_t.append(jnp.concatenate([stack2(bt), stack2(kt)], axis=0))
            vst.append(stack2(v_sc[pl.ds(r0, C), ls]))
            strict2.append((ss2 < tt2) if d == 0 else (ss2 > tt2))
            incl2.append((ss2 <= tt2) if d == 0 else (ss2 >= tt2))
            dst.append((d, r0, ls))
        gg = each(_mm_nt, lhs, rhs_t)
        uu = each(_mm_nt, lhs, states)
        am = each(lambda g, m: jnp.where(m, g[0:R2, :], 0.0), gg, strict2)
        ncat = each(lambda g, m: jnp.where(m, g[R2:2 * R2, :], 0.0), gg, incl2)
        qb = each(lambda a_: a_[:, 0:R2].astype(BF16), am)
        xs = each(lambda u, a_, v_: u[0:R2, :] + _mm(a_[:, R2:2 * R2], v_), uu, am, vst)
        for i in range(n_fac):
            xh = each(lambda x: x.astype(BF16), xs)
            if i < RW_SPLIT_STAGES:
                xl = each(lambda x, h: (x - h.astype(F32)).astype(BF16), xs, xh)
                corr = each(lambda q, l_: jnp.dot(q, l_, preferred_element_type=F32), qb, xl)
            else:
                corr = [0.0] * len(xs)
            if i < n_fac - 1:
                zz = each(lambda q, h: jnp.dot(q, jnp.concatenate([h, q], axis=1),
                                               preferred_element_type=F32), qb, xh)
                xs = each(lambda x, z, c_: x + z[:, 0:PAIR_W] + c_, xs, zz, corr)
                qb = each(lambda z: z[:, PAIR_W:PAIR_W + R2].astype(BF16), zz)
            else:
                xs = each(lambda x, q, h, c_: x + jnp.dot(q, h, preferred_element_type=F32) + c_,
                          xs, qb, xh, corr)
        sv = each(lambda x, v_: jnp.concatenate([x, v_], axis=0), xs, vst)
        yst = each(lambda u, n, s_: u[R2:2 * R2, :] + _mm(n, s_), uu, ncat, sv)
        for (d, r0, ls), y_ in zip(dst, yst):
            yo_sc[d, pl.ds(r0, C), ls] = y_[0:C, :] + y_[C:2 * C, :]
        return tuple(each(lambda s_, p, v_, r_: s_ * p + _mm_tn(v_, r_) * p, states, pc, sv, rhs_t))

    if lat:
        states0 = tuple(_pair_blockdiag(st_ref[sq, d, 2 * pr], st_ref[sq, d, 2 * pr + 1])
                        for sq, d, pr in chains)
    else:
        states0 = tuple(jnp.zeros((PAIR_W, PAIR_W), F32) for _ in chains)
    states = lax.fori_loop(0, n_chunk, body, states0)
    if not lat:
        for (sq, d, pr), s_ in zip(chains, states):
            for hh in range(2):
                fin_ref[sq, d, 2 * pr + hh] = _pair_diag(s_, hh)
    inv_n = 1.0 / HEAD_DIM
    for sq in range(nb):
        rows = slice(sq * T, (sq + 1) * T)
        y = yo_sc[0, rows, :] + yo_sc[1, rows, :]
        mu = _mm_split(y, bd) * inv_n
        yc = y - mu
        var = _mm_split(yc * yc, bd) * inv_n
        yn = yc * lax.rsqrt(var + RW_GN_EPS) * lnw_ref[...] + lnb_ref[...]
        bonus = _mm_split(rk_sc[rows, :], bd) * v_sc[rows, :]
        y_ref[rows, :] = (yn + bonus) * g_sc[rows, :]


def _rwkv_call(p, T, n_seq, blk0, nb, lat, l, params, sm, bd, state):
    tw = pltpu.VMEM((nb * T, GROUP_W), F32)
    tw2 = pltpu.VMEM((2, nb * T, GROUP_W), F32)
    op2 = pltpu.VMEM((2, nb * T, GROUP_W), BF16)
    return _mixer_call(functools.partial(_rwkv_kernel, T, nb, lat), "rwkv7_T%d" % T, p, RW_PAD, T,
                       n_seq, blk0, nb, l, params, (sm, bd), state,
                       (2, N_HEADS, HEAD_DIM, HEAD_DIM),
                       [op2, op2, op2, op2,
                        pltpu.VMEM((2, N_PAIRS, nb * T // RW_CHUNK, PAIR_W), F32),
                        tw, tw2, tw, tw,
                        pltpu.VMEM((N_PAIRS, T, PAIR_W), F32),
                        pltpu.VMEM((T // SUBLANES, PAIR_W), F32)],
                       single_buffer_p=(nb > 1 and lat))


def _pad_last(a, n):
    return jnp.pad(a, ((0, 0),) * (a.ndim - 1) + ((0, n - a.shape[-1]),))


def _row(a):
    return a.reshape(a.shape[0], 1, -1)


def _heads_blockdiag(w):
    eye = jnp.eye(N_HEADS, dtype=w.dtype)
    full = w[..., :, :, None, :] * eye[:, None, :, None]
    return full.reshape(w.shape[:-3] + (GROUP_W, GROUP_W))


def kernel(x_prompt, x_sample, state_rwkv, state_hgrn, state_ssd, state_lru, c, c_ctx, mod_w, mod_b, norm_g, w_in, w_out, rw_mu, rw_w0, rw_w2, rw_a0, rw_a2, rw_g2, rw_kk, rw_ka, rw_rk, rw_ln_w, rw_ln_b, hg_lb, hg_norm, ss_conv_w, ss_conv_b, ss_dt_bias, ss_A_log, ss_D, ss_norm, lr_conv_w, lr_conv_b, lr_wa, lr_ba, lr_wx, lr_bx, lr_lam, mlp_w1, mlp_w2):
    bp, t_ctx, _ = x_prompt.shape
    bl, t_lat, _ = x_sample.shape
    n_ctx_tok = bp * t_ctx
    assert n_ctx_tok % t_lat == 0 and t_lat % GRID_W == 0 and 1 + bl <= 8
    lat_blk0 = n_ctx_tok // t_lat
    tiles = _Tiles(n_ctx_tok, bl * t_lat, t_lat)

    x_ctx = x_prompt.reshape(n_ctx_tok, D_MODEL)
    x_lat = x_sample.reshape(bl * t_lat, D_MODEL)
    cond8 = jnp.zeros((8, D_MODEL), F32).at[0].set(c_ctx).at[1:1 + bl].set(c)
    mod = _mod_call(cond8, mod_w, mod_b).reshape(DEPTH, 8, MOD_CH, D_MODEL)

    o1 = RW_COLS
    o2 = o1 + HG_COLS
    o3 = o2 + SS_COLS
    w_in_b = w_in.astype(BF16)
    ws_in = (_pad_last(w_in_b[:, :, :o1], RW_PAD), w_in_b[:, :, o1:o2],
             _pad_last(w_in_b[:, :, o2:o3], SS_PAD), w_in_b[:, :, o3:])
    w_out_b, w1_b, w2_b = w_out.astype(BF16), mlp_w1.astype(BF16), mlp_w2.astype(BF16)

    lane = jnp.arange(GROUP_W)
    bd_ones = (lane[:, None] // HEAD_DIM == lane[None, :] // HEAD_DIM).astype(BF16)

    col = jnp.arange(RW_PAD)
    quarter = col // RW_SHIFT_SPLIT
    real = col < RW_COLS
    sm_lat = jnp.stack([(quarter == i) & real for i in range(4)]).astype(F32)
    sm_ctx = jnp.stack([(quarter < 2) & real, (quarter >= 2) & real,
                        jnp.zeros_like(real), jnp.zeros_like(real)]).astype(F32)
    lora_blocks = (rw_w2[:, 0], rw_w2[:, 1], rw_a2[:, 0], rw_a2[:, 1], rw_g2)
    wl = jnp.concatenate(
        [jnp.pad(blk, ((0, 0), (0, 0), (j * GROUP_W, (len(lora_blocks) - 1 - j) * GROUP_W)))
         for j, blk in enumerate(lora_blocks)], axis=1)
    wl = jnp.pad(wl, ((0, 0), (0, PAIR_W - wl.shape[1]), (0, 0))).astype(BF16)
    rw_params = (_pad_last(_row(rw_mu), RW_PAD), rw_w0, rw_a0, wl, _row(rw_kk), _row(rw_ka),
                 _row(rw_rk), _row(rw_ln_w), _row(rw_ln_b))

    lb_soft = jax.nn.softmax(hg_lb.astype(F32), axis=0)
    lower = jnp.cumsum(lb_soft, axis=0) - lb_soft[0]
    lb_floor = jnp.maximum(lower, LB_FLOOR)
    hg_params = (jnp.log(lb_floor), jnp.log1p(-lower), lb_floor - lower, _row(hg_norm))

    ss_params = (ss_conv_w, _row(ss_conv_b), _pad_last(_row(ss_dt_bias), PAIR_W),
                 _pad_last(_row(-jnp.exp(ss_A_log.astype(F32))), PAIR_W),
                 _row(jnp.repeat(ss_D, HEAD_DIM, axis=-1)), _row(ss_norm))

    wcat = jnp.concatenate([_heads_blockdiag(lr_wa[:, 0]), _heads_blockdiag(lr_wx[:, 0]),
                            _heads_blockdiag(lr_wa[:, 1]), _heads_blockdiag(lr_wx[:, 1])],
                           axis=-1).astype(BF16)
    bcat = _row(jnp.stack([lr_ba[:, 0], lr_bx[:, 0], lr_ba[:, 1], lr_bx[:, 1]], axis=1))
    lr_params = (lr_conv_w, _row(lr_conv_b), wcat, bcat, lr_lam)

    new_rw = jnp.zeros((bp, DEPTH, 2, N_HEADS, HEAD_DIM, HEAD_DIM), F32)
    new_hg = jnp.zeros((bp, DEPTH, 2, N_HEADS, HEAD_DIM, HEAD_DIM), F32)
    new_ss = jnp.zeros((bp, DEPTH, 2, N_HEADS, HEAD_DIM, SSD_STATE), F32)
    new_lr = jnp.zeros((bp, DEPTH, 2, GROUP_W), F32)

    for l in range(DEPTH):
        p_rw, p_hg, p_ss, p_lr = _inproj_call(tiles, l, x_ctx, x_lat, mod, norm_g, ws_in)

        y_rw_c, new_rw = _rwkv_call(p_rw, t_ctx, bp, 0, RW_CTX_SEQS, False, l, rw_params, sm_ctx,
                                    bd_ones, (new_rw, False))
        y_rw_l, = _rwkv_call(p_rw, t_lat, bl, lat_blk0, RW_LAT_SEQS, True, l, rw_params, sm_lat,
                             bd_ones, (state_rwkv, True))
        y_hg_c, new_hg = _hgrn_call(p_hg, t_ctx, bp, 0, l, hg_params, bd_ones, (new_hg, False))
        y_hg_l, = _hgrn_call(p_hg, t_lat, bl, lat_blk0, l, hg_params, bd_ones, (state_hgrn, True))
        y_ss_c, new_ss = _ssd_call(p_ss, t_ctx, bp, 0, l, ss_params, (new_ss, False))
        y_ss_l, = _ssd_call(p_ss, t_lat, bl, lat_blk0, l, ss_params, (state_ssd, True))
        y_lr_c, new_lr = _lru_call(p_lr, t_ctx, bp, 0, l, lr_params, (new_lr, False))
        y_lr_l, = _lru_call(p_lr, t_lat, bl, lat_blk0, l, lr_params, (state_lru, True))

        x_ctx, x_lat = _outmlp_call(tiles, l, x_ctx, x_lat, (y_rw_c, y_hg_c, y_ss_c, y_lr_c),
                                    (y_rw_l, y_hg_l, y_ss_l, y_lr_l), mod, norm_g,
                                    w_out_b, w1_b, w2_b)

    return (x_ctx.reshape(bp, t_ctx, D_MODEL), x_lat.reshape(bl, t_lat, D_MODEL),
            new_rw, new_hg, new_ss, new_lr)
```

```python
import functools

import jax
import jax.numpy as jnp
from jax import lax
from jax.experimental import pallas as pl
from jax.experimental.pallas import tpu as pltpu

F32 = jnp.float32
BF16 = jnp.bfloat16

D_MODEL = 1024
DEPTH = 4
MOD_CH = 6
GROUP_W = 256
HEAD_DIM = 64
N_HEADS = 4
N_PAIRS = 2
PAIR_W = 128
D_FF = 4096
GRID_W = 64
NORM_EPS = 1e-6
LB_FLOOR = 1e-30
RW_GN_EPS = 64e-5
LRU_C = 8.0
SSD_STATE = 128
CONV_TAPS = (-2, -1, 0, 1)

RW_COLS = 864
RW_PAD = 896
RW_SHIFT_SPLIT = RW_COLS // 4
RW_LORA_W = 16
RW_LORA_G = 32
HG_COLS = 1280
SS_COLS = 1032
SS_PAD = 1152
LR_COLS = 512
RW_CHUNK = 64
RW_SPLIT_STAGES = 4
RW_CTX_SEQS = 4
RW_LAT_SEQS = 2
HG_BLOCK = 16
HG_UNROLL = 4
SSD_CHUNK = 256
SUBLANES = 8
SCAN_BLOCK = SUBLANES

TOK_TILE = 512
VMEM_LIMIT = 56 * 1024 * 1024


def _cparams(sem):
    return pltpu.CompilerParams(dimension_semantics=sem, vmem_limit_bytes=VMEM_LIMIT)


def _layer_spec(a, l, **kw):
    return pl.BlockSpec((None,) + a.shape[1:], lambda *_: (l,) + (0,) * (a.ndim - 1), **kw)


def _const_spec(a):
    return pl.BlockSpec(a.shape, lambda *_: (0,) * a.ndim)


def _mm(a, b):
    return jnp.dot(a.astype(BF16), b.astype(BF16), preferred_element_type=F32)


def _mm_nt(a, b):
    return lax.dot_general(a.astype(BF16), b.astype(BF16), (((1,), (1,)), ((), ())),
                           preferred_element_type=F32)


def _mm_tn(a, b):
    return lax.dot_general(a.astype(BF16), b.astype(BF16), (((0,), (0,)), ((), ())),
                           preferred_element_type=F32)


def _mm_split(a, b_bf16):
    hi = a.astype(BF16)
    lo = (a - hi.astype(F32)).astype(BF16)
    return (jnp.dot(hi, b_bf16, preferred_element_type=F32)
            + jnp.dot(lo, b_bf16, preferred_element_type=F32))


def _row_iota(shape):
    return lax.broadcasted_iota(jnp.int32, shape, 0)


def _lane_iota(shape):
    return lax.broadcasted_iota(jnp.int32, shape, 1)


def _shift_rows(x, d):
    n = x.shape[0]
    s = (-d) % n
    if s == 0:
        return x
    return pltpu.roll(x, s, 0)


def _seg_cumsum_slab(src, dst, tot, n, seg, rev):
    sb = SUBLANES
    nb = n // sb
    g = seg // sb
    order = list(range(sb))[::-1] if rev else list(range(sb))
    part, prev = {}, None
    for r in order:
        x_r = src[pl.ds(r, nb, stride=sb), :]
        part[r] = x_r if prev is None else part[prev] + x_r
        prev = r
    off = None
    if g > 1:
        tot[...] = part[prev]
        ng = nb // g
        qorder = list(range(g))[::-1] if rev else list(range(g))
        run, excl = None, {}
        for q in qorder:
            excl[q] = run
            t_q = tot[pl.ds(q, ng, stride=g), :]
            run = t_q if run is None else run + t_q
        for q in qorder:
            tot[pl.ds(q, ng, stride=g), :] = jnp.zeros((ng, PAIR_W), F32) if excl[q] is None else excl[q]
        off = tot[...]
    for r in range(sb):
        dst[pl.ds(r, nb, stride=sb), :] = part[r] if off is None else part[r] + off


def _sigmoid(x):
    return jax.nn.sigmoid(x)


def _silu(x):
    return x * _sigmoid(x)


def _softplus(x):
    return jnp.maximum(x, 0.0) + jnp.log(1.0 + jnp.exp(-jnp.abs(x)))


def _gelu_tanh(x):
    c = 0.7978845608028654
    return 0.5 * x * (1.0 + jnp.tanh(c * (x + 0.044715 * (x * x * x))))


def _rms(x, g):
    return x * lax.rsqrt(jnp.mean(x * x, axis=-1, keepdims=True) + NORM_EPS) * g


def _conv_rows(x, w_ref, b_ref):
    t = x.shape[0]
    rows = _row_iota(x.shape)
    y = b_ref[...] + jnp.zeros_like(x)
    for j, d in enumerate(CONV_TAPS):
        valid = (rows + d >= 0) & (rows + d < t)
        y = y + jnp.where(valid, _shift_rows(x, d), 0.0) * w_ref[j:j + 1, :]
    return y


def _pair_blockdiag(a, b):
    z = jnp.zeros_like(a)
    return jnp.concatenate([jnp.concatenate([a, z], axis=1), jnp.concatenate([z, b], axis=1)], axis=0)


def _pair_diag(m, hh):
    return m[hh * HEAD_DIM:(hh + 1) * HEAD_DIM, hh * HEAD_DIM:(hh + 1) * HEAD_DIM]


CAST_STEPS = 16
IN_SPLITS = (0, RW_COLS, RW_COLS + HG_COLS, RW_COLS + HG_COLS + SS_COLS, RW_COLS + HG_COLS + SS_COLS + LR_COLS)
IN_WIDTHS = (RW_PAD, HG_COLS, SS_PAD, LR_COLS)


def _cast_block_specs(a, l, step_of):
    rows = a.shape[1] // CAST_STEPS
    blk = lambda *i: jnp.minimum(step_of(*i), CAST_STEPS - 1)
    in_spec = pl.BlockSpec((None, rows) + a.shape[2:], lambda *i: (l, blk(*i), 0))
    return in_spec, rows, blk


def _win_cast_specs(w_in, l, step_of):
    in_spec, rows, blk = _cast_block_specs(w_in, l, step_of)
    out_specs = [pl.BlockSpec((rows, n), lambda *i: (blk(*i), 0)) for n in IN_WIDTHS]
    out_shape = [jax.ShapeDtypeStruct((w_in.shape[1], n), BF16) for n in IN_WIDTHS]
    return in_spec, out_specs, out_shape


def _win_cast(w_ref, o_refs):
    w = w_ref[...]
    for j, o in enumerate(o_refs):
        piece = w[:, IN_SPLITS[j]:IN_SPLITS[j + 1]]
        pad = IN_WIDTHS[j] - piece.shape[1]
        if pad:
            piece = jnp.concatenate([piece, jnp.zeros((piece.shape[0], pad), F32)], axis=1)
        o[...] = piece.astype(BF16)


def _mod_kernel(c_ref, w_ref, b_ref, win_ref, o_ref, *w_out_refs):
    c = c_ref[...]
    o_ref[...] = _mm(_silu(c), w_ref[...]) + b_ref[...]

    @pl.when(pl.program_id(0) * MOD_CH + pl.program_id(1) < CAST_STEPS)
    def _():
        _win_cast(win_ref, w_out_refs)


def _mod_call(cond8, mod_w, mod_b, w_in):
    assert DEPTH * MOD_CH >= CAST_STEPS
    win_spec, wo_specs, wo_shape = _win_cast_specs(w_in, 0, lambda l, j: l * MOD_CH + j)
    outs = pl.pallas_call(
        _mod_kernel,
        grid=(DEPTH, MOD_CH),
        in_specs=[
            pl.BlockSpec((8, D_MODEL), lambda l, j: (0, 0)),
            pl.BlockSpec((None, D_MODEL, D_MODEL), lambda l, j: (l, 0, j)),
            pl.BlockSpec((None, 1, D_MODEL), lambda l, j: (l, 0, j)),
            win_spec,
        ],
        out_specs=[pl.BlockSpec((None, 8, D_MODEL), lambda l, j: (l, 0, j))] + wo_specs,
        out_shape=[jax.ShapeDtypeStruct((DEPTH, 8, MOD_CH * D_MODEL), F32)] + wo_shape,
        compiler_params=_cparams(("arbitrary", "arbitrary")),
        name="modulation",
    )(cond8, mod_w, mod_b.reshape(DEPTH, 1, MOD_CH * D_MODEL), w_in)
    return outs[0], tuple(outs[1:])


class _Tiles:
    def __init__(self, n_ctx_tok, n_lat_tok, t_lat):
        assert n_ctx_tok % TOK_TILE == 0 and t_lat % TOK_TILE == 0
        self.n_ctx = n_ctx_tok // TOK_TILE
        self.n_lat = n_lat_tok // TOK_TILE
        self.per_seq = t_lat // TOK_TILE
        self.n = self.n_ctx + self.n_lat

    def ctx_idx(self, i):
        return jnp.minimum(i, self.n_ctx - 1)

    def lat_idx(self, i):
        return jnp.clip(i - self.n_ctx, 0, self.n_lat - 1)

    def seq_of(self, i):
        return jnp.where(i < self.n_ctx, 0, 1 + (i - self.n_ctx) // self.per_seq)


def _inproj_kernel(n_ctx_tiles, xc_ref, xl_ref, mod_ref, g_ref, wrw, whg, wss, wlr, f0, f1, f2,
                   orw, ohg, oss, olr, b0, b1, b2):
    x = jnp.where(pl.program_id(0) < n_ctx_tiles, xc_ref[...], xl_ref[...])
    m = mod_ref[...]
    h = _rms(x, g_ref[0:1, :]) * (1.0 + m[1:2, :]) + m[0:1, :]
    hb = h.astype(BF16)
    orw[...] = jnp.dot(hb, wrw[...], preferred_element_type=F32)
    ohg[...] = jnp.dot(hb, whg[...], preferred_element_type=F32)
    oss[...] = jnp.dot(hb, wss[...], preferred_element_type=F32)
    olr[...] = jnp.dot(hb, wlr[...], preferred_element_type=F32)

    @pl.when(pl.program_id(0) < CAST_STEPS)
    def _():
        for f, b in ((f0, b0), (f1, b1), (f2, b2)):
            b[...] = f[...].astype(BF16)


def _inproj_call(tiles, l, x_ctx, x_lat, mod, norm_g, ws, mlp_f32):
    assert tiles.n >= CAST_STEPS
    n_tok = tiles.n * TOK_TILE
    cast_in, cast_out, cast_shape = [], [], []
    for a in mlp_f32:
        in_spec, rows, blk = _cast_block_specs(a, l, lambda i: i)
        cast_in.append(in_spec)
        cast_out.append(pl.BlockSpec((rows, a.shape[2]), lambda i, blk=blk: (blk(i), 0)))
        cast_shape.append(jax.ShapeDtypeStruct(a.shape[1:], BF16))
    outs = pl.pallas_call(
        functools.partial(_inproj_kernel, tiles.n_ctx),
        grid=(tiles.n,),
        in_specs=[
            pl.BlockSpec((TOK_TILE, D_MODEL), lambda i: (tiles.ctx_idx(i), 0)),
            pl.BlockSpec((TOK_TILE, D_MODEL), lambda i: (tiles.lat_idx(i), 0)),
            pl.BlockSpec((None, None, MOD_CH, D_MODEL), lambda i: (l, tiles.seq_of(i), 0, 0)),
            _layer_spec(norm_g, l),
        ] + [_const_spec(w) for w in ws] + cast_in,
        out_specs=[pl.BlockSpec((TOK_TILE, w), lambda i: (i, 0)) for w in IN_WIDTHS] + cast_out,
        out_shape=[jax.ShapeDtypeStruct((n_tok, w), F32) for w in IN_WIDTHS] + cast_shape,
        compiler_params=_cparams(("arbitrary",)),
        name="in_proj",
    )(x_ctx, x_lat, mod, norm_g, *ws, *mlp_f32)
    return outs[:4], outs[4:]


def _outmlp_kernel(n_ctx_tiles, cast_next, xc_ref, xl_ref, c0, c1, c2, c3, l0, l1, l2, l3, mod_ref,
                   g_ref, wout, w1, w2, *rest):
    if cast_next:
        win_ref, oc_ref, ol_ref = rest[0], rest[1], rest[2]

        @pl.when(pl.program_id(0) < CAST_STEPS)
        def _():
            _win_cast(win_ref, rest[3:])
    else:
        oc_ref, ol_ref = rest
    is_ctx = pl.program_id(0) < n_ctx_tiles
    mix = jnp.concatenate(
        [jnp.where(is_ctx, c[...], l[...]) for c, l in ((c0, l0), (c1, l1), (c2, l2), (c3, l3))],
        axis=-1).astype(BF16)
    m = mod_ref[...]
    g = g_ref[...]
    x = jnp.where(is_ctx, xc_ref[...], xl_ref[...])
    t = jnp.dot(mix, wout[...], preferred_element_type=F32)
    x1 = x + m[2:3, :] * _rms(t, g[1:2, :])
    h2 = (_rms(x1, g[2:3, :]) * (1.0 + m[4:5, :]) + m[3:4, :]).astype(BF16)
    acc = jnp.zeros_like(x)
    for f in range(D_FF // D_MODEL):
        u = jnp.dot(h2, w1[:, f * D_MODEL:(f + 1) * D_MODEL], preferred_element_type=F32)
        u = jnp.square(jnp.maximum(u, 0.0)).astype(BF16)
        acc = acc + jnp.dot(u, w2[f * D_MODEL:(f + 1) * D_MODEL, :], preferred_element_type=F32)
    res = x1 + m[5:6, :] * _rms(acc, g[3:4, :])

    @pl.when(is_ctx)
    def _():
        oc_ref[...] = res

    @pl.when(jnp.logical_not(is_ctx))
    def _():
        ol_ref[...] = res


def _outmlp_call(tiles, l, x_ctx, x_lat, ys_ctx, ys_lat, mod, norm_g, wout, w1, w2, w_in_next):
    ctx_x = pl.BlockSpec((TOK_TILE, D_MODEL), lambda i: (tiles.ctx_idx(i), 0))
    lat_x = pl.BlockSpec((TOK_TILE, D_MODEL), lambda i: (tiles.lat_idx(i), 0))
    ctx_y = pl.BlockSpec((TOK_TILE, GROUP_W), lambda i: (tiles.ctx_idx(i), 0))
    lat_y = pl.BlockSpec((TOK_TILE, GROUP_W), lambda i: (tiles.lat_idx(i), 0))
    single = lambda a: pl.BlockSpec(a.shape, lambda i: (0,) * a.ndim, pipeline_mode=pl.Buffered(1))
    in_specs = [ctx_x, lat_x] + [ctx_y] * 4 + [lat_y] * 4 + [
        pl.BlockSpec((None, None, MOD_CH, D_MODEL), lambda i: (l, tiles.seq_of(i), 0, 0)),
        _layer_spec(norm_g, l), single(wout), single(w1), single(w2)]
    out_specs = [ctx_x, lat_x]
    out_shape = [jax.ShapeDtypeStruct(x_ctx.shape, F32), jax.ShapeDtypeStruct(x_lat.shape, F32)]
    args = [x_ctx, x_lat, *ys_ctx, *ys_lat, mod, norm_g, wout, w1, w2]
    cast_next = w_in_next is not None
    if cast_next:
        assert tiles.n >= CAST_STEPS
        win_spec, wo_specs, wo_shape = _win_cast_specs(w_in_next, l + 1, lambda i: i)
        in_specs.append(win_spec)
        out_specs += wo_specs
        out_shape += wo_shape
        args.append(w_in_next)
    outs = pl.pallas_call(
        functools.partial(_outmlp_kernel, tiles.n_ctx, cast_next),
        grid=(tiles.n,),
        in_specs=in_specs,
        out_specs=out_specs,
        out_shape=out_shape,
        compiler_params=_cparams(("arbitrary",)),
        name="out_mlp",
    )(*args)
    return outs[0], outs[1], tuple(outs[2:])


def _mixer_call(kernel_fn, name, p, width, T, n_seq, blk0, nb, l, params, consts, state, state_blk,
                scratch, single_buffer_p=False):
    assert n_seq % nb == 0 and blk0 % nb == 0
    st_arr, st_in = state
    st_spec = pl.BlockSpec((nb, None) + state_blk, lambda i: (i, l) + (0,) * len(state_blk))
    p_mode = dict(pipeline_mode=pl.Buffered(1)) if single_buffer_p else {}
    in_specs = [pl.BlockSpec((nb * T, width), lambda i: (blk0 // nb + i, 0), **p_mode)]
    in_specs += [_layer_spec(a, l) for a in params] + [_const_spec(a) for a in consts]
    y_spec = pl.BlockSpec((nb * T, GROUP_W), lambda i: (i, 0))
    y_shape = jax.ShapeDtypeStruct((n_seq * T, GROUP_W), F32)
    if st_in:
        in_specs.append(st_spec)
        out_specs, out_shape, aliases = [y_spec], [y_shape], {}
    else:
        in_specs.append(pl.BlockSpec(memory_space=pl.ANY))
        out_specs = [y_spec, st_spec]
        out_shape = [y_shape, jax.ShapeDtypeStruct(st_arr.shape, st_arr.dtype)]
        aliases = {len(in_specs) - 1: 1}
    return pl.pallas_call(
        kernel_fn,
        grid=(n_seq // nb,),
        in_specs=in_specs,
        out_specs=out_specs,
        out_shape=out_shape,
        input_output_aliases=aliases,
        scratch_shapes=scratch,
        compiler_params=_cparams(("arbitrary",)),
        name=name,
    )(p, *params, *consts, st_arr)


def _lru_kernel(T, has_init, p_ref, cw_ref, cb_ref, wcat_ref, bcat_ref, lam_ref, st_ref, y_ref, *rest):
    fin_ref = None if has_init else rest[0]
    a_sc, b_sc, h_sc, al_sc, bl_sc, c_sc = rest[-6:]
    SB = SCAN_BLOCK
    n_blk = T // SB
    n_slab = GROUP_W // PAIR_W
    ybr = p_ref[:, 0:GROUP_W]
    xbr = p_ref[:, GROUP_W:2 * GROUP_W]
    xc = _conv_rows(xbr, cw_ref, cb_ref)
    gates = _sigmoid(_mm(xc, wcat_ref[...]) + bcat_ref[...])
    sp = _softplus(-lam_ref[...])
    chains = [(d, hv) for d in range(2) for hv in range(n_slab)]
    for d in range(2):
        r = gates[:, (2 * d) * GROUP_W:(2 * d + 1) * GROUP_W]
        ig = gates[:, (2 * d + 1) * GROUP_W:(2 * d + 2) * GROUP_W]
        log_a = (-LRU_C) * r * sp[d:d + 1, :]
        a = jnp.exp(log_a)
        b = jnp.sqrt(jnp.maximum(1.0 - a * a, 0.0)) * (ig * xc)
        for hv in range(n_slab):
            a_sc[d, hv] = a[:, hv * PAIR_W:(hv + 1) * PAIR_W]
            b_sc[d, hv] = b[:, hv * PAIR_W:(hv + 1) * PAIR_W]

    def rows_of(ref, r):
        return ref[pl.ds(r, n_blk, stride=SB), :]

    acc, bcc = {}, {}
    for d, hv in chains:
        order = list(range(SB)) if d == 0 else list(range(SB - 1, -1, -1))
        prev = None
        for r in order:
            a_r, b_r = rows_of(a_sc.at[d, hv], r), rows_of(b_sc.at[d, hv], r)
            if prev is None:
                acc[d, hv, r], bcc[d, hv, r] = a_r, b_r
            else:
                acc[d, hv, r] = a_r * acc[d, hv, prev]
                bcc[d, hv, r] = a_r * bcc[d, hv, prev] + b_r
            prev = r
        al_sc[d, hv] = acc[d, hv, prev]
        bl_sc[d, hv] = bcc[d, hv, prev]

    def body(j, carry):
        new = []
        for (d, hv), c in zip(chains, carry):
            jb = j if d == 0 else n_blk - 1 - j
            c_sc[d, hv, pl.ds(jb, 1), :] = c
            new.append(al_sc[d, hv, pl.ds(jb, 1), :] * c + bl_sc[d, hv, pl.ds(jb, 1), :])
        return tuple(new)

    if has_init:
        carry0 = tuple(st_ref[0, d:d + 1, hv * PAIR_W:(hv + 1) * PAIR_W] for d, hv in chains)
    else:
        carry0 = tuple(jnp.zeros((1, PAIR_W), F32) for _ in chains)
    last = lax.fori_loop(0, n_blk, body, carry0, unroll=SB)
    for (d, hv), c in zip(chains, last):
        if not has_init:
            fin_ref[0, d:d + 1, hv * PAIR_W:(hv + 1) * PAIR_W] = c
        cin = c_sc[d, hv]
        for r in range(SB):
            h_sc[d, hv, pl.ds(r, n_blk, stride=SB), :] = acc[d, hv, r] * cin + bcc[d, hv, r]
    total = jnp.concatenate([h_sc[0, hv] + h_sc[1, hv] for hv in range(n_slab)], axis=-1)
    y_ref[...] = total * _gelu_tanh(ybr)


def _lru_call(p, T, n_seq, blk0, l, params, state):
    n_slab = GROUP_W // PAIR_W
    full = pltpu.VMEM((2, n_slab, T, PAIR_W), F32)
    edge = pltpu.VMEM((2, n_slab, T // SCAN_BLOCK, PAIR_W), F32)
    return _mixer_call(functools.partial(_lru_kernel, T, state[1]), "rglru_T%d" % T, p, LR_COLS, T,
                       n_seq, blk0, 1, l, params, (), state, (2, GROUP_W),
                       [full, full, full, edge, edge, edge])


def _ssd_kernel(T, has_init, p_ref, cw_ref, cb_ref, dtb_ref, aneg_ref, dexp_ref, nrm_ref, st_ref,
                y_ref, *rest):
    fin_ref = None if has_init else rest[0]
    Q = SSD_CHUNK
    nq = T // Q
    z = p_ref[:, 0:GROUP_W]
    xbc = _silu(_conv_rows(p_ref[:, GROUP_W:4 * GROUP_W], cw_ref, cb_ref))
    xs = xbc[:, 0:GROUP_W]
    bm = xbc[:, GROUP_W:2 * GROUP_W]
    cm = xbc[:, 2 * GROUP_W:3 * GROUP_W]
    dt = _softplus(p_ref[:, 4 * GROUP_W:4 * GROUP_W + PAIR_W] + dtb_ref[...])
    da = dt * aneg_ref[...]
    da_sc, ac_sc, ct_sc = rest[-3:]
    da_sc[...] = da
    _seg_cumsum_slab(da_sc, ac_sc.at[0], ct_sc, T, Q, rev=False)
    _seg_cumsum_slab(da_sc, ac_sc.at[1], ct_sc, T, Q, rev=True)
    acf = ac_sc[0]
    acb = ac_sc[1]
    ti = _row_iota((Q, Q))
    si = _lane_iota((Q, Q))
    low = si <= ti
    upp = si >= ti
    lane_lo = _lane_iota((Q, PAIR_W)) < HEAD_DIM
    row_lo = _row_iota((PAIR_W, SSD_STATE)) < HEAD_DIM

    def col(a, j):
        return a[:, j:j + 1]

    ydiag, ds, dec, ecol = [], [], [], []
    for q in range(nq):
        sl = slice(q * Q, (q + 1) * Q)
        acf_q, acb_q, dt_q = acf[sl], acb[sl], dt[sl]
        acf_t, acb_t, ldt_t = acf_q.T, acb_q.T, jnp.log(dt_q).T
        y_q, ds_q, dec_q, ec_q = [], [], [], []
        for pr in range(N_PAIRS):
            ls = slice(pr * PAIR_W, (pr + 1) * PAIR_W)
            bg, cg, xp = bm[sl, ls], cm[sl, ls], xs[sl, ls]
            g = _mm_nt(cg, bg)
            outs = []
            wf_cols, wb_cols, ef_cols, eb_cols, decf, decb = [], [], [], [], [], []
            for hh in range(2):
                h = 2 * pr + hh
                hb_ = N_HEADS + h
                lf = jnp.exp(col(acf_q, h) - (acf_t[h:h + 1, :] - ldt_t[h:h + 1, :]))
                lb = jnp.exp(col(acb_q, hb_) - (acb_t[hb_:hb_ + 1, :] - ldt_t[hb_:hb_ + 1, :]))
                mh = g * (jnp.where(low, lf, 0.0) + jnp.where(upp, lb, 0.0))
                outs.append(_mm(mh, xp))
                af_last = acf_q[Q - 1:Q, h:h + 1]
                ab_last = acb_q[0:1, hb_:hb_ + 1]
                wf_cols.append(jnp.exp(af_last - col(acf_q, h)) * col(dt_q, h))
                wb_cols.append(jnp.exp(ab_last - col(acb_q, hb_)) * col(dt_q, hb_))
                ef_cols.append(jnp.exp(col(acf_q, h)))
                eb_cols.append(jnp.exp(col(acb_q, hb_)))
                decf.append(jnp.exp(af_last))
                decb.append(jnp.exp(ab_last))
            y_q.append(jnp.where(lane_lo, outs[0], outs[1]))
            wf = jnp.where(lane_lo, wf_cols[0], wf_cols[1])
            wb = jnp.where(lane_lo, wb_cols[0], wb_cols[1])
            ds_q.append((_mm_tn(xp * wf, bg), _mm_tn(xp * wb, bg)))
            dec_q.append((jnp.where(row_lo, decf[0], decf[1]), jnp.where(row_lo, decb[0], decb[1])))
            ec_q.append((jnp.where(lane_lo, ef_cols[0], ef_cols[1]),
                         jnp.where(lane_lo, eb_cols[0], eb_cols[1])))
        ydiag.append(y_q)
        ds.append(ds_q)
        dec.append(dec_q)
        ecol.append(ec_q)

    prev = [[[None, None] for _ in range(N_PAIRS)] for _ in range(nq)]
    for pr in range(N_PAIRS):
        for d in range(2):
            if has_init:
                s = jnp.concatenate([st_ref[0, d, 2 * pr], st_ref[0, d, 2 * pr + 1]], axis=0)
            else:
                s = jnp.zeros((PAIR_W, SSD_STATE), F32)
            order = range(nq) if d == 0 else range(nq - 1, -1, -1)
            for q in order:
                prev[q][pr][d] = s
                s = dec[q][pr][d] * s + ds[q][pr][d]
            if not has_init:
                fin_ref[0, d, 2 * pr] = s[0:HEAD_DIM, :]
                fin_ref[0, d, 2 * pr + 1] = s[HEAD_DIM:PAIR_W, :]

    ys = []
    for q in range(nq):
        sl = slice(q * Q, (q + 1) * Q)
        parts = []
        for pr in range(N_PAIRS):
            ls = slice(pr * PAIR_W, (pr + 1) * PAIR_W)
            yp = ydiag[q][pr]
            if has_init or nq > 1:
                cg = cm[sl, ls]
                for d in range(2):
                    yp = yp + _mm_nt(cg, prev[q][pr][d]) * ecol[q][pr][d]
            parts.append(yp)
        ys.append(jnp.concatenate(parts, axis=-1))
    y = jnp.concatenate(ys, axis=0) if nq > 1 else ys[0]
    y = y + dexp_ref[...] * xs
    y_ref[...] = _rms(y * _silu(z), nrm_ref[...])


def _ssd_call(p, T, n_seq, blk0, l, params, state):
    return _mixer_call(functools.partial(_ssd_kernel, T, state[1]), "ssd_T%d" % T, p, SS_PAD, T,
                       n_seq, blk0, 1, l, params, (), state, (2, N_HEADS, HEAD_DIM, SSD_STATE),
                       [pltpu.VMEM((T, PAIR_W), F32), pltpu.VMEM((2, T, PAIR_W), F32),
                        pltpu.VMEM((T // SUBLANES, PAIR_W), F32)])


def _hgrn_kernel(T, has_init, p_ref, la_ref, l1_ref, lbd_ref, nrm_ref, bd_ref, st_ref, y_ref, *rest):
    fin_ref = None if has_init else rest[0]
    q_sc, v_sc, k_sc, b_sc, o_sc, t_sc = rest[-6:]
    NB = HG_BLOCK
    n_blk = T // NB

    def put(dst, val):
        for pr in range(N_PAIRS):
            dst[pr] = val[:, pr * PAIR_W:(pr + 1) * PAIR_W]

    put(q_sc, _silu(p_ref[:, 0:GROUP_W]))
    put(v_sc, p_ref[:, GROUP_W:2 * GROUP_W])
    for d in range(2):
        x = p_ref[:, (2 + d) * GROUP_W:(3 + d) * GROUP_W]
        e1 = jnp.exp(-jnp.abs(x))
        lsig = jnp.minimum(x, 0.0) - jnp.log(1.0 + e1)
        a_ = la_ref[d:d + 1, :] + jnp.zeros_like(x)
        b_ = l1_ref[d:d + 1, :] + lsig
        logf = jnp.maximum(a_, b_) + jnp.log(1.0 + jnp.exp(-jnp.abs(a_ - b_)))
        put(k_sc.at[d], jnp.exp(l1_ref[d:d + 1, :]) * (jnp.where(x >= 0.0, e1, 1.0) / (1.0 + e1))
            - lbd_ref[d:d + 1, :])
        put(b_sc.at[d], logf)
        for pr in range(N_PAIRS):
            _seg_cumsum_slab(b_sc.at[d, pr], b_sc.at[d, pr], t_sc, T, NB, rev=(d == 1))
    bd = bd_ref[0:PAIR_W, 0:PAIR_W]
    rows8 = _row_iota((SUBLANES, PAIR_W))
    bdm = (_row_iota((PAIR_W, PAIR_W)) < HEAD_DIM) == (_lane_iota((PAIR_W, PAIR_W)) < HEAD_DIM)
    chains = [(d, pr) for d in range(2) for pr in range(N_PAIRS)]
    n_half = NB // SUBLANES

    def body(j, states):
        new_states = []
        for d, pr in chains:
            jb = j if d == 0 else n_blk - 1 - j
            r0 = pl.multiple_of(jb * NB, NB)
            blk = pl.ds(r0, NB)
            qb, vb, kb, bb = q_sc[pr, blk, :], v_sc[pr, blk, :], k_sc[d, pr, blk, :], b_sc[d, pr, blk, :]

            def row(ref, s):
                return jnp.broadcast_to(ref[pl.ds(r0 + s, 1), :], (SUBLANES, PAIR_W))

            parts, where = [], []
            for s in range(NB):
                b_s, k_s = row(b_sc.at[d, pr], s), row(k_sc.at[d, pr], s)
                for hf in range(n_half):
                    t0 = hf * SUBLANES
                    t1 = t0 + SUBLANES - 1
                    if (t1 < s) if d == 0 else (t0 > s):
                        continue
                    rs = slice(t0, t0 + SUBLANES)
                    pr_ = qb[rs] * jnp.exp(bb[rs] - b_s) * k_s
                    if not ((t0 >= s) if d == 0 else (t1 <= s)):
                        keep = (rows8 + t0 >= s) if d == 0 else (rows8 + t0 <= s)
                        pr_ = jnp.where(keep, pr_, 0.0)
                    parts.append(pr_)
                    where.append((s, hf))
            att = _mm(jnp.concatenate(parts, axis=0), bd)
            halves = [jnp.zeros((SUBLANES, PAIR_W), F32) for _ in range(n_half)]
            for i, (s, hf) in enumerate(where):
                halves[hf] = halves[hf] + att[i * SUBLANES:(i + 1) * SUBLANES, :] * row(v_sc.at[pr], s)
            o = jnp.concatenate(halves, axis=0)
            bl = bb[NB - 1:NB, :] if d == 0 else bb[0:1, :]
            st = states[d * N_PAIRS + pr]
            o_sc[d, blk, pr * PAIR_W:(pr + 1) * PAIR_W] = o + _mm_nt(qb * jnp.exp(bb), st)
            new_states.append(st * jnp.exp(bl)
                              + jnp.where(bdm, _mm_tn(vb, kb * jnp.exp(bl - bb)), 0.0))
        return tuple(new_states)

    if has_init:
        states0 = tuple(_pair_blockdiag(st_ref[0, d, 2 * pr], st_ref[0, d, 2 * pr + 1]).T
                        for d, pr in chains)
    else:
        states0 = tuple(jnp.zeros((PAIR_W, PAIR_W), F32) for _ in chains)
    states = lax.fori_loop(0, n_blk, body, states0, unroll=HG_UNROLL)
    if not has_init:
        for (d, pr), s_ in zip(chains, states):
            kv = s_.T
            for hh in range(2):
                fin_ref[0, d, 2 * pr + hh] = _pair_diag(kv, hh)
    g = p_ref[:, 4 * GROUP_W:5 * GROUP_W]
    y_ref[...] = _rms(o_sc[0] + o_sc[1], nrm_ref[...]) * _silu(g)


def _hgrn_call(p, T, n_seq, blk0, l, params, bd, state):
    pw = pltpu.VMEM((N_PAIRS, T, PAIR_W), F32)
    pw2 = pltpu.VMEM((2, N_PAIRS, T, PAIR_W), F32)
    return _mixer_call(functools.partial(_hgrn_kernel, T, state[1]), "hgrn2_T%d" % T, p, HG_COLS, T,
                       n_seq, blk0, 1, l, params, (bd,), state, (2, N_HEADS, HEAD_DIM, HEAD_DIM),
                       [pw, pw, pw2, pw2, pltpu.VMEM((2, T, GROUP_W), F32),
                        pltpu.VMEM((T // SUBLANES, PAIR_W), F32)])


def _rwkv_kernel(T, nb, lat, p_ref, mu_ref, w0_ref, a0_ref, wl_ref, kk_ref, ka_ref, rk_ref,
                 lnw_ref, lnb_ref, sm_ref, bd_ref, st_ref, y_ref, *rest):
    fin_ref = None if lat else rest[0]
    xa_sc, xr_sc, yb_sc, yk_sc, pc_sc, v_sc, yo_sc, g_sc, rk_sc, cs_sc, ct_sc = rest[-11:]
    C = RW_CHUNK
    n_chunk = T // C
    bd = bd_ref[...]
    rowi = _row_iota((T, RW_PAD))
    if lat:
        gcol = rowi % GRID_W
        shifts = [(-1, gcol >= 1), (1, gcol <= GRID_W - 2), (-GRID_W, rowi >= GRID_W),
                  (GRID_W, rowi < T - GRID_W)]
    else:
        shifts = [(-1, rowi >= 1), (1, rowi <= T - 2)]
    ln = _lane_iota((T, PAIR_W))
    n_t = 2 * RW_LORA_W
    n_i = 4 * RW_LORA_W
    n_s = n_i + RW_LORA_G
    for sq in range(nb):
        rows = slice(sq * T, (sq + 1) * T)
        P = p_ref[rows, :]
        sh = jnp.zeros_like(P)
        for j, (dlt, valid) in enumerate(shifts):
            sh = jnp.where(valid & (sm_ref[j:j + 1, :] > 0.5), _shift_rows(P, dlt), sh)
        P = P + (sh - P) * mu_ref[...]
        r = P[:, 0:GROUP_W]
        k = P[:, GROUP_W:2 * GROUP_W]
        lo = P[:, 3 * GROUP_W:3 * GROUP_W + PAIR_W]
        act = jnp.where(ln < n_t, jnp.tanh(lo),
                        jnp.where(ln < n_i, lo, jnp.where(ln < n_s, _sigmoid(lo), 0.0)))
        lora = _mm(act, wl_ref[...])
        g_sc[rows, :] = lora[:, 4 * GROUP_W:5 * GROUP_W]
        kkv = k * kk_ref[...]
        kkn = kkv / jnp.maximum(jnp.sqrt(_mm_split(kkv * kkv, bd)), 1e-12)
        v_sc[rows, :] = P[:, 2 * GROUP_W:3 * GROUP_W]
        ksum = jnp.zeros_like(k)
        for d in range(2):
            w_lora = lora[:, d * GROUP_W:(d + 1) * GROUP_W]
            a_lora = lora[:, (2 + d) * GROUP_W:(3 + d) * GROUP_W]
            w_log = -_softplus(-(w0_ref[d:d + 1, :] + w_lora)) - 0.5
            logw = -jnp.exp(w_log)
            a_sig = _sigmoid(a0_ref[d:d + 1, :] + a_lora)
            k_d = k * (1.0 + (a_sig - 1.0) * ka_ref[...])
            ksum = ksum + k_d
            edge = C - 1 if d == 0 else 0
            for pr in range(N_PAIRS):
                ls = slice(pr * PAIR_W, (pr + 1) * PAIR_W)
                cs_sc[pr] = logw[:, ls]
                _seg_cumsum_slab(cs_sc.at[pr], cs_sc.at[pr], ct_sc, T, C, rev=(d == 1))
                pc_sc[d, pr, sq * n_chunk:(sq + 1) * n_chunk, :] = jnp.exp(
                    cs_sc[pr, pl.ds(edge, n_chunk, stride=C), :])
            c = jnp.concatenate([cs_sc[pr] for pr in range(N_PAIRS)], axis=-1)
            einv = jnp.exp(-c)
            xa_sc[d, rows, :] = ((-kkn) * jnp.exp(c - logw)).astype(BF16)
            xr_sc[d, rows, :] = (r * jnp.exp(c)).astype(BF16)
            yb_sc[d, rows, :] = ((kkn * a_sig) * einv).astype(BF16)
            yk_sc[d, rows, :] = (k_d * einv).astype(BF16)
        rk_sc[rows, :] = r * ksum * rk_ref[...]

    R2 = 2 * C
    assert R2 % PAIR_W == 0
    lane_lo = _lane_iota((C, PAIR_W)) < HEAD_DIM
    tt2 = _row_iota((R2, 2 * R2)) % C
    ss2 = _lane_iota((R2, 2 * R2)) % C
    n_fac = C.bit_length() - 1

    def stack2(x):
        z = jnp.zeros_like(x)
        return jnp.concatenate([jnp.where(lane_lo, x, z), jnp.where(lane_lo, z, x)], axis=0)

    chains = [(sq, d, pr) for sq in range(nb) for d in range(2) for pr in range(N_PAIRS)]
    each = lambda f, *cols: [f(*a) for a in zip(*cols)]

    def body(j, states):
        lhs, rhs_t, vst, pc, strict2, incl2, dst = [], [], [], [], [], [], []
        for sq, d, pr in chains:
            jc = j if d == 0 else n_chunk - 1 - j
            r0 = pl.multiple_of(sq * T + jc * C, C)
            ls = slice(pr * PAIR_W, (pr + 1) * PAIR_W)
            at = xa_sc[d, pl.ds(r0, C), ls]
            rt = xr_sc[d, pl.ds(r0, C), ls]
            bt = yb_sc[d, pl.ds(r0, C), ls]
            kt = yk_sc[d, pl.ds(r0, C), ls]
            pc.append(pc_sc[d, pr, pl.ds(sq * n_chunk + jc, 1), :])
            lhs.append(jnp.concatenate([stack2(at), stack2(rt)], axis=0))
            rhs_t.append(jnp.concatenate([stack2(bt), stack2(kt)], axis=0))
            vst.append(stack2(v_sc[pl.ds(r0, C), ls]))
            strict2.append((ss2 < tt2) if d == 0 else (ss2 > tt2))
            incl2.append((ss2 <= tt2) if d == 0 else (ss2 >= tt2))
            dst.append((d, r0, ls))
        gg = each(_mm_nt, lhs, rhs_t)
        uu = each(_mm_nt, lhs, states)
        am = each(lambda g, m: jnp.where(m, g[0:R2, :], 0.0), gg, strict2)
        ncat = each(lambda g, m: jnp.where(m, g[R2:2 * R2, :], 0.0), gg, incl2)
        qb = each(lambda a_: a_[:, 0:R2].astype(BF16), am)
        xs = each(lambda u, a_, v_: u[0:R2, :] + _mm(a_[:, R2:2 * R2], v_), uu, am, vst)
        for i in range(n_fac):
            xh = each(lambda x: x.astype(BF16), xs)
            if i < RW_SPLIT_STAGES:
                xl = each(lambda x, h: (x - h.astype(F32)).astype(BF16), xs, xh)
                corr = each(lambda q, l_: jnp.dot(q, l_, preferred_element_type=F32), qb, xl)
            else:
                corr = [0.0] * len(xs)
            if i < n_fac - 1:
                zz = each(lambda q, h: jnp.dot(q, jnp.concatenate([h, q], axis=1),
                                               preferred_element_type=F32), qb, xh)
                xs = each(lambda x, z, c_: x + z[:, 0:PAIR_W] + c_, xs, zz, corr)
                qb = each(lambda z: z[:, PAIR_W:PAIR_W + R2].astype(BF16), zz)
            else:
                xs = each(lambda x, q, h, c_: x + jnp.dot(q, h, preferred_element_type=F32) + c_,
                          xs, qb, xh, corr)
        sv = each(lambda x, v_: jnp.concatenate([x, v_], axis=0), xs, vst)
        yst = each(lambda u, n, s_: u[R2:2 * R2, :] + _mm(n, s_), uu, ncat, sv)
        for (d, r0, ls), y_ in zip(dst, yst):
            yo_sc[d, pl.ds(r0, C), ls] = y_[0:C, :] + y_[C:2 * C, :]
        return tuple(each(lambda s_, p, v_, r_: s_ * p + _mm_tn(v_, r_) * p, states, pc, sv, rhs_t))

    if lat:
        states0 = tuple(_pair_blockdiag(st_ref[sq, d, 2 * pr], st_ref[sq, d, 2 * pr + 1])
                        for sq, d, pr in chains)
    else:
        states0 = tuple(jnp.zeros((PAIR_W, PAIR_W), F32) for _ in chains)
    states = lax.fori_loop(0, n_chunk, body, states0)
    if not lat:
        for (sq, d, pr), s_ in zip(chains, states):
            for hh in range(2):
                fin_ref[sq, d, 2 * pr + hh] = _pair_diag(s_, hh)
    inv_n = 1.0 / HEAD_DIM
    for sq in range(nb):
        rows = slice(sq * T, (sq + 1) * T)
        y = yo_sc[0, rows, :] + yo_sc[1, rows, :]
        mu = _mm_split(y, bd) * inv_n
        yc = y - mu
        var = _mm_split(yc * yc, bd) * inv_n
        yn = yc * lax.rsqrt(var + RW_GN_EPS) * lnw_ref[...] + lnb_ref[...]
        bonus = _mm_split(rk_sc[rows, :], bd) * v_sc[rows, :]
        y_ref[rows, :] = (yn + bonus) * g_sc[rows, :]


def _rwkv_call(p, T, n_seq, blk0, nb, lat, l, params, sm, bd, state):
    tw = pltpu.VMEM((nb * T, GROUP_W), F32)
    tw2 = pltpu.VMEM((2, nb * T, GROUP_W), F32)
    op2 = pltpu.VMEM((2, nb * T, GROUP_W), BF16)
    return _mixer_call(functools.partial(_rwkv_kernel, T, nb, lat), "rwkv7_T%d" % T, p, RW_PAD, T,
                       n_seq, blk0, nb, l, params, (sm, bd), state,
                       (2, N_HEADS, HEAD_DIM, HEAD_DIM),
                       [op2, op2, op2, op2,
                        pltpu.VMEM((2, N_PAIRS, nb * T // RW_CHUNK, PAIR_W), F32),
                        tw, tw2, tw, tw,
                        pltpu.VMEM((N_PAIRS, T, PAIR_W), F32),
                        pltpu.VMEM((T // SUBLANES, PAIR_W), F32)],
                       single_buffer_p=(nb > 1 and lat))


def _pad_last(a, n):
    return jnp.pad(a, ((0, 0),) * (a.ndim - 1) + ((0, n - a.shape[-1]),))


def _row(a):
    return a.reshape(a.shape[0], 1, -1)


def _heads_blockdiag(w):
    eye = jnp.eye(N_HEADS, dtype=w.dtype)
    full = w[..., :, :, None, :] * eye[:, None, :, None]
    return full.reshape(w.shape[:-3] + (GROUP_W, GROUP_W))


def kernel(x_prompt, x_sample, state_rwkv, state_hgrn, state_ssd, state_lru, c, c_ctx, mod_w, mod_b, norm_g, w_in, w_out, rw_mu, rw_w0, rw_w2, rw_a0, rw_a2, rw_g2, rw_kk, rw_ka, rw_rk, rw_ln_w, rw_ln_b, hg_lb, hg_norm, ss_conv_w, ss_conv_b, ss_dt_bias, ss_A_log, ss_D, ss_norm, lr_conv_w, lr_conv_b, lr_wa, lr_ba, lr_wx, lr_bx, lr_lam, mlp_w1, mlp_w2):
    bp, t_ctx, _ = x_prompt.shape
    bl, t_lat, _ = x_sample.shape
    n_ctx_tok = bp * t_ctx
    assert n_ctx_tok % t_lat == 0 and t_lat % GRID_W == 0 and 1 + bl <= 8
    lat_blk0 = n_ctx_tok // t_lat
    tiles = _Tiles(n_ctx_tok, bl * t_lat, t_lat)

    x_ctx = x_prompt.reshape(n_ctx_tok, D_MODEL)
    x_lat = x_sample.reshape(bl * t_lat, D_MODEL)
    cond8 = jnp.zeros((8, D_MODEL), F32).at[0].set(c_ctx).at[1:1 + bl].set(c)
    mod, ws_in = _mod_call(cond8, mod_w, mod_b, w_in)
    mod = mod.reshape(DEPTH, 8, MOD_CH, D_MODEL)

    lane = jnp.arange(GROUP_W)
    bd_ones = (lane[:, None] // HEAD_DIM == lane[None, :] // HEAD_DIM).astype(BF16)

    col = jnp.arange(RW_PAD)
    quarter = col // RW_SHIFT_SPLIT
    real = col < RW_COLS
    sm_lat = jnp.stack([(quarter == i) & real for i in range(4)]).astype(F32)
    sm_ctx = jnp.stack([(quarter < 2) & real, (quarter >= 2) & real,
                        jnp.zeros_like(real), jnp.zeros_like(real)]).astype(F32)
    lora_blocks = (rw_w2[:, 0], rw_w2[:, 1], rw_a2[:, 0], rw_a2[:, 1], rw_g2)
    wl = jnp.concatenate(
        [jnp.pad(blk, ((0, 0), (0, 0), (j * GROUP_W, (len(lora_blocks) - 1 - j) * GROUP_W)))
         for j, blk in enumerate(lora_blocks)], axis=1)
    wl = jnp.pad(wl, ((0, 0), (0, PAIR_W - wl.shape[1]), (0, 0))).astype(BF16)
    rw_params = (_pad_last(_row(rw_mu), RW_PAD), rw_w0, rw_a0, wl, _row(rw_kk), _row(rw_ka),
                 _row(rw_rk), _row(rw_ln_w), _row(rw_ln_b))

    lb_soft = jax.nn.softmax(hg_lb.astype(F32), axis=0)
    lower = jnp.cumsum(lb_soft, axis=0) - lb_soft[0]
    lb_floor = jnp.maximum(lower, LB_FLOOR)
    hg_params = (jnp.log(lb_floor), jnp.log1p(-lower), lb_floor - lower, _row(hg_norm))

    ss_params = (ss_conv_w, _row(ss_conv_b), _pad_last(_row(ss_dt_bias), PAIR_W),
                 _pad_last(_row(-jnp.exp(ss_A_log.astype(F32))), PAIR_W),
                 _row(jnp.repeat(ss_D, HEAD_DIM, axis=-1)), _row(ss_norm))

    wcat = jnp.concatenate([_heads_blockdiag(lr_wa[:, 0]), _heads_blockdiag(lr_wx[:, 0]),
                            _heads_blockdiag(lr_wa[:, 1]), _heads_blockdiag(lr_wx[:, 1])],
                           axis=-1).astype(BF16)
    bcat = _row(jnp.stack([lr_ba[:, 0], lr_bx[:, 0], lr_ba[:, 1], lr_bx[:, 1]], axis=1))
    lr_params = (lr_conv_w, _row(lr_conv_b), wcat, bcat, lr_lam)

    new_rw = jnp.zeros((bp, DEPTH, 2, N_HEADS, HEAD_DIM, HEAD_DIM), F32)
    new_hg = jnp.zeros((bp, DEPTH, 2, N_HEADS, HEAD_DIM, HEAD_DIM), F32)
    new_ss = jnp.zeros((bp, DEPTH, 2, N_HEADS, HEAD_DIM, SSD_STATE), F32)
    new_lr = jnp.zeros((bp, DEPTH, 2, GROUP_W), F32)

    for l in range(DEPTH):
        (p_rw, p_hg, p_ss, p_lr), mlp_b = _inproj_call(tiles, l, x_ctx, x_lat, mod, norm_g, ws_in,
                                                       (w_out, mlp_w1, mlp_w2))

        y_rw_c, new_rw = _rwkv_call(p_rw, t_ctx, bp, 0, RW_CTX_SEQS, False, l, rw_params, sm_ctx,
                                    bd_ones, (new_rw, False))
        y_rw_l, = _rwkv_call(p_rw, t_lat, bl, lat_blk0, RW_LAT_SEQS, True, l, rw_params, sm_lat,
                             bd_ones, (state_rwkv, True))
        y_hg_c, new_hg = _hgrn_call(p_hg, t_ctx, bp, 0, l, hg_params, bd_ones, (new_hg, False))
        y_hg_l, = _hgrn_call(p_hg, t_lat, bl, lat_blk0, l, hg_params, bd_ones, (state_hgrn, True))
        y_ss_c, new_ss = _ssd_call(p_ss, t_ctx, bp, 0, l, ss_params, (new_ss, False))
        y_ss_l, = _ssd_call(p_ss, t_lat, bl, lat_blk0, l, ss_params, (state_ssd, True))
        y_lr_c, new_lr = _lru_call(p_lr, t_ctx, bp, 0, l, lr_params, (new_lr, False))
        y_lr_l, = _lru_call(p_lr, t_lat, bl, lat_blk0, l, lr_params, (state_lru, True))

        x_ctx, x_lat, ws_in = _outmlp_call(tiles, l, x_ctx, x_lat, (y_rw_c, y_hg_c, y_ss_c, y_lr_c),
                                           (y_rw_l, y_hg_l, y_ss_l, y_lr_l), mod, norm_g, *mlp_b,
                                           w_in if l + 1 < DEPTH else None)

    return (x_ctx.reshape(bp, t_ctx, D_MODEL), x_lat.reshape(bl, t_lat, D_MODEL),
            new_rw, new_hg, new_ss, new_lr)
```

```python
import functools

import jax
import jax.numpy as jnp
from jax import lax
from jax.experimental import pallas as pl
from jax.experimental.pallas import tpu as pltpu

F32 = jnp.float32
BF16 = jnp.bfloat16

D_MODEL = 1024
DEPTH = 4
MOD_CH = 6
GROUP_W = 256
HEAD_DIM = 64
N_HEADS = 4
N_PAIRS = 2
PAIR_W = 128
D_FF = 4096
GRID_W = 64
NORM_EPS = 1e-6
LB_FLOOR = 1e-30
RW_GN_EPS = 64e-5
LRU_C = 8.0
SSD_STATE = 128
CONV_TAPS = (-2, -1, 0, 1)

RW_COLS = 864
RW_PAD = 896
RW_SHIFT_SPLIT = RW_COLS // 4
RW_LORA_W = 16
RW_LORA_G = 32
HG_COLS = 1280
SS_COLS = 1032
SS_PAD = 1152
LR_COLS = 512
RW_CHUNK = 64
RW_SPLIT_STAGES = 4
RW_CTX_SEQS = 4
RW_LAT_SEQS = 2
HG_BLOCK = 16
HG_UNROLL = 8
SSD_CHUNK = 256
SUBLANES = 8
SCAN_BLOCK = SUBLANES

TOK_TILE = 512
VMEM_LIMIT = 56 * 1024 * 1024


def _cparams(sem):
    return pltpu.CompilerParams(dimension_semantics=sem, vmem_limit_bytes=VMEM_LIMIT)


def _layer_spec(a, l, **kw):
    return pl.BlockSpec((None,) + a.shape[1:], lambda *_: (l,) + (0,) * (a.ndim - 1), **kw)


def _const_spec(a):
    return pl.BlockSpec(a.shape, lambda *_: (0,) * a.ndim)


def _mm(a, b):
    return jnp.dot(a.astype(BF16), b.astype(BF16), preferred_element_type=F32)


def _mm_nt(a, b):
    return lax.dot_general(a.astype(BF16), b.astype(BF16), (((1,), (1,)), ((), ())),
                           preferred_element_type=F32)


def _mm_tn(a, b):
    return lax.dot_general(a.astype(BF16), b.astype(BF16), (((0,), (0,)), ((), ())),
                           preferred_element_type=F32)


def _mm_split(a, b_bf16):
    hi = a.astype(BF16)
    lo = (a - hi.astype(F32)).astype(BF16)
    return (jnp.dot(hi, b_bf16, preferred_element_type=F32)
            + jnp.dot(lo, b_bf16, preferred_element_type=F32))


def _row_iota(shape):
    return lax.broadcasted_iota(jnp.int32, shape, 0)


def _lane_iota(shape):
    return lax.broadcasted_iota(jnp.int32, shape, 1)


def _shift_rows(x, d):
    n = x.shape[0]
    s = (-d) % n
    if s == 0:
        return x
    return pltpu.roll(x, s, 0)


def _seg_cumsum_slab(src, dst, tot, n, seg, rev):
    sb = SUBLANES
    nb = n // sb
    g = seg // sb
    order = list(range(sb))[::-1] if rev else list(range(sb))
    part, prev = {}, None
    for r in order:
        x_r = src[pl.ds(r, nb, stride=sb), :]
        part[r] = x_r if prev is None else part[prev] + x_r
        prev = r
    off = None
    if g > 1:
        tot[...] = part[prev]
        ng = nb // g
        qorder = list(range(g))[::-1] if rev else list(range(g))
        run, excl = None, {}
        for q in qorder:
            excl[q] = run
            t_q = tot[pl.ds(q, ng, stride=g), :]
            run = t_q if run is None else run + t_q
        for q in qorder:
            tot[pl.ds(q, ng, stride=g), :] = jnp.zeros((ng, PAIR_W), F32) if excl[q] is None else excl[q]
        off = tot[...]
    for r in range(sb):
        dst[pl.ds(r, nb, stride=sb), :] = part[r] if off is None else part[r] + off


def _sigmoid(x):
    return jax.nn.sigmoid(x)


def _silu(x):
    return x * _sigmoid(x)


def _softplus(x):
    return jnp.maximum(x, 0.0) + jnp.log(1.0 + jnp.exp(-jnp.abs(x)))


def _gelu_tanh(x):
    c = 0.7978845608028654
    return 0.5 * x * (1.0 + jnp.tanh(c * (x + 0.044715 * (x * x * x))))


def _rms(x, g):
    return x * lax.rsqrt(jnp.mean(x * x, axis=-1, keepdims=True) + NORM_EPS) * g


def _conv_rows(x, w_ref, b_ref):
    t = x.shape[0]
    rows = _row_iota(x.shape)
    y = b_ref[...] + jnp.zeros_like(x)
    for j, d in enumerate(CONV_TAPS):
        valid = (rows + d >= 0) & (rows + d < t)
        y = y + jnp.where(valid, _shift_rows(x, d), 0.0) * w_ref[j:j + 1, :]
    return y


def _pair_blockdiag(a, b):
    z = jnp.zeros_like(a)
    return jnp.concatenate([jnp.concatenate([a, z], axis=1), jnp.concatenate([z, b], axis=1)], axis=0)


def _pair_diag(m, hh):
    return m[hh * HEAD_DIM:(hh + 1) * HEAD_DIM, hh * HEAD_DIM:(hh + 1) * HEAD_DIM]


CAST_STEPS = 16
IN_SPLITS = (0, RW_COLS, RW_COLS + HG_COLS, RW_COLS + HG_COLS + SS_COLS, RW_COLS + HG_COLS + SS_COLS + LR_COLS)
IN_WIDTHS = (RW_PAD, HG_COLS, SS_PAD, LR_COLS)


def _cast_block_specs(a, l, step_of):
    rows = a.shape[1] // CAST_STEPS
    blk = lambda *i: jnp.minimum(step_of(*i), CAST_STEPS - 1)
    in_spec = pl.BlockSpec((None, rows) + a.shape[2:], lambda *i: (l, blk(*i), 0))
    return in_spec, rows, blk


def _win_cast_specs(w_in, l, step_of):
    in_spec, rows, blk = _cast_block_specs(w_in, l, step_of)
    out_specs = [pl.BlockSpec((rows, n), lambda *i: (blk(*i), 0)) for n in IN_WIDTHS]
    out_shape = [jax.ShapeDtypeStruct((w_in.shape[1], n), BF16) for n in IN_WIDTHS]
    return in_spec, out_specs, out_shape


def _win_cast(w_ref, o_refs):
    w = w_ref[...]
    for j, o in enumerate(o_refs):
        piece = w[:, IN_SPLITS[j]:IN_SPLITS[j + 1]]
        pad = IN_WIDTHS[j] - piece.shape[1]
        if pad:
            piece = jnp.concatenate([piece, jnp.zeros((piece.shape[0], pad), F32)], axis=1)
        o[...] = piece.astype(BF16)


def _mod_kernel(c_ref, w_ref, b_ref, win_ref, o_ref, *w_out_refs):
    c = c_ref[...]
    o_ref[...] = _mm(_silu(c), w_ref[...]) + b_ref[...]

    @pl.when(pl.program_id(0) * MOD_CH + pl.program_id(1) < CAST_STEPS)
    def _():
        _win_cast(win_ref, w_out_refs)


def _mod_call(cond8, mod_w, mod_b, w_in):
    assert DEPTH * MOD_CH >= CAST_STEPS
    win_spec, wo_specs, wo_shape = _win_cast_specs(w_in, 0, lambda l, j: l * MOD_CH + j)
    outs = pl.pallas_call(
        _mod_kernel,
        grid=(DEPTH, MOD_CH),
        in_specs=[
            pl.BlockSpec((8, D_MODEL), lambda l, j: (0, 0)),
            pl.BlockSpec((None, D_MODEL, D_MODEL), lambda l, j: (l, 0, j)),
            pl.BlockSpec((None, 1, D_MODEL), lambda l, j: (l, 0, j)),
            win_spec,
        ],
        out_specs=[pl.BlockSpec((None, 8, D_MODEL), lambda l, j: (l, 0, j))] + wo_specs,
        out_shape=[jax.ShapeDtypeStruct((DEPTH, 8, MOD_CH * D_MODEL), F32)] + wo_shape,
        compiler_params=_cparams(("arbitrary", "arbitrary")),
        name="modulation",
    )(cond8, mod_w, mod_b.reshape(DEPTH, 1, MOD_CH * D_MODEL), w_in)
    return outs[0], tuple(outs[1:])


class _Tiles:
    def __init__(self, n_ctx_tok, n_lat_tok, t_lat):
        assert n_ctx_tok % TOK_TILE == 0 and t_lat % TOK_TILE == 0
        self.n_ctx = n_ctx_tok // TOK_TILE
        self.n_lat = n_lat_tok // TOK_TILE
        self.per_seq = t_lat // TOK_TILE
        self.n = self.n_ctx + self.n_lat

    def ctx_idx(self, i):
        return jnp.minimum(i, self.n_ctx - 1)

    def lat_idx(self, i):
        return jnp.clip(i - self.n_ctx, 0, self.n_lat - 1)

    def seq_of(self, i):
        return jnp.where(i < self.n_ctx, 0, 1 + (i - self.n_ctx) // self.per_seq)


def _inproj_kernel(n_ctx_tiles, xc_ref, xl_ref, mod_ref, g_ref, wrw, whg, wss, wlr, f0, f1, f2,
                   orw, ohg, oss, olr, b0, b1, b2):
    x = jnp.where(pl.program_id(0) < n_ctx_tiles, xc_ref[...], xl_ref[...])
    m = mod_ref[...]
    h = _rms(x, g_ref[0:1, :]) * (1.0 + m[1:2, :]) + m[0:1, :]
    hb = h.astype(BF16)
    orw[...] = jnp.dot(hb, wrw[...], preferred_element_type=F32)
    ohg[...] = jnp.dot(hb, whg[...], preferred_element_type=F32)
    oss[...] = jnp.dot(hb, wss[...], preferred_element_type=F32)
    olr[...] = jnp.dot(hb, wlr[...], preferred_element_type=F32)

    @pl.when(pl.program_id(0) < CAST_STEPS)
    def _():
        for f, b in ((f0, b0), (f1, b1), (f2, b2)):
            b[...] = f[...].astype(BF16)


def _inproj_call(tiles, l, x_ctx, x_lat, mod, norm_g, ws, mlp_f32):
    assert tiles.n >= CAST_STEPS
    n_tok = tiles.n * TOK_TILE
    cast_in, cast_out, cast_shape = [], [], []
    for a in mlp_f32:
        in_spec, rows, blk = _cast_block_specs(a, l, lambda i: i)
        cast_in.append(in_spec)
        cast_out.append(pl.BlockSpec((rows, a.shape[2]), lambda i, blk=blk: (blk(i), 0)))
        cast_shape.append(jax.ShapeDtypeStruct(a.shape[1:], BF16))
    outs = pl.pallas_call(
        functools.partial(_inproj_kernel, tiles.n_ctx),
        grid=(tiles.n,),
        in_specs=[
            pl.BlockSpec((TOK_TILE, D_MODEL), lambda i: (tiles.ctx_idx(i), 0)),
            pl.BlockSpec((TOK_TILE, D_MODEL), lambda i: (tiles.lat_idx(i), 0)),
            pl.BlockSpec((None, None, MOD_CH, D_MODEL), lambda i: (l, tiles.seq_of(i), 0, 0)),
            _layer_spec(norm_g, l),
        ] + [_const_spec(w) for w in ws] + cast_in,
        out_specs=[pl.BlockSpec((TOK_TILE, w), lambda i: (i, 0)) for w in IN_WIDTHS] + cast_out,
        out_shape=[jax.ShapeDtypeStruct((n_tok, w), F32) for w in IN_WIDTHS] + cast_shape,
        compiler_params=_cparams(("arbitrary",)),
        name="in_proj",
    )(x_ctx, x_lat, mod, norm_g, *ws, *mlp_f32)
    return outs[:4], outs[4:]


def _outmlp_kernel(n_ctx_tiles, cast_next, xc_ref, xl_ref, c0, c1, c2, c3, l0, l1, l2, l3, mod_ref,
                   g_ref, wout, w1, w2, *rest):
    if cast_next:
        win_ref, oc_ref, ol_ref = rest[0], rest[1], rest[2]

        @pl.when(pl.program_id(0) < CAST_STEPS)
        def _():
            _win_cast(win_ref, rest[3:])
    else:
        oc_ref, ol_ref = rest
    is_ctx = pl.program_id(0) < n_ctx_tiles
    mix = jnp.concatenate(
        [jnp.where(is_ctx, c[...], l[...]) for c, l in ((c0, l0), (c1, l1), (c2, l2), (c3, l3))],
        axis=-1).astype(BF16)
    m = mod_ref[...]
    g = g_ref[...]
    x = jnp.where(is_ctx, xc_ref[...], xl_ref[...])
    t = jnp.dot(mix, wout[...], preferred_element_type=F32)
    x1 = x + m[2:3, :] * _rms(t, g[1:2, :])
    h2 = (_rms(x1, g[2:3, :]) * (1.0 + m[4:5, :]) + m[3:4, :]).astype(BF16)
    acc = jnp.zeros_like(x)
    for f in range(D_FF // D_MODEL):
        u = jnp.dot(h2, w1[:, f * D_MODEL:(f + 1) * D_MODEL], preferred_element_type=F32)
        u = jnp.square(jnp.maximum(u, 0.0)).astype(BF16)
        acc = acc + jnp.dot(u, w2[f * D_MODEL:(f + 1) * D_MODEL, :], preferred_element_type=F32)
    res = x1 + m[5:6, :] * _rms(acc, g[3:4, :])

    @pl.when(is_ctx)
    def _():
        oc_ref[...] = res

    @pl.when(jnp.logical_not(is_ctx))
    def _():
        ol_ref[...] = res


def _outmlp_call(tiles, l, x_ctx, x_lat, ys_ctx, ys_lat, mod, norm_g, wout, w1, w2, w_in_next):
    ctx_x = pl.BlockSpec((TOK_TILE, D_MODEL), lambda i: (tiles.ctx_idx(i), 0))
    lat_x = pl.BlockSpec((TOK_TILE, D_MODEL), lambda i: (tiles.lat_idx(i), 0))
    ctx_y = pl.BlockSpec((TOK_TILE, GROUP_W), lambda i: (tiles.ctx_idx(i), 0))
    lat_y = pl.BlockSpec((TOK_TILE, GROUP_W), lambda i: (tiles.lat_idx(i), 0))
    single = lambda a: pl.BlockSpec(a.shape, lambda i: (0,) * a.ndim, pipeline_mode=pl.Buffered(1))
    in_specs = [ctx_x, lat_x] + [ctx_y] * 4 + [lat_y] * 4 + [
        pl.BlockSpec((None, None, MOD_CH, D_MODEL), lambda i: (l, tiles.seq_of(i), 0, 0)),
        _layer_spec(norm_g, l), single(wout), single(w1), single(w2)]
    out_specs = [ctx_x, lat_x]
    out_shape = [jax.ShapeDtypeStruct(x_ctx.shape, F32), jax.ShapeDtypeStruct(x_lat.shape, F32)]
    args = [x_ctx, x_lat, *ys_ctx, *ys_lat, mod, norm_g, wout, w1, w2]
    cast_next = w_in_next is not None
    if cast_next:
        assert tiles.n >= CAST_STEPS
        win_spec, wo_specs, wo_shape = _win_cast_specs(w_in_next, l + 1, lambda i: i)
        in_specs.append(win_spec)
        out_specs += wo_specs
        out_shape += wo_shape
        args.append(w_in_next)
    outs = pl.pallas_call(
        functools.partial(_outmlp_kernel, tiles.n_ctx, cast_next),
        grid=(tiles.n,),
        in_specs=in_specs,
        out_specs=out_specs,
        out_shape=out_shape,
        compiler_params=_cparams(("arbitrary",)),
        name="out_mlp",
    )(*args)
    return outs[0], outs[1], tuple(outs[2:])


def _mixer_call(kernel_fn, name, p, width, T, n_seq, blk0, nb, l, params, consts, state, state_blk,
                scratch, single_buffer_p=False):
    assert n_seq % nb == 0 and blk0 % nb == 0
    st_arr, st_in = state
    st_spec = pl.BlockSpec((nb, None) + state_blk, lambda i: (i, l) + (0,) * len(state_blk))
    p_mode = dict(pipeline_mode=pl.Buffered(1)) if single_buffer_p else {}
    in_specs = [pl.BlockSpec((nb * T, width), lambda i: (blk0 // nb + i, 0), **p_mode)]
    in_specs += [_layer_spec(a, l) for a in params] + [_const_spec(a) for a in consts]
    y_spec = pl.BlockSpec((nb * T, GROUP_W), lambda i: (i, 0))
    y_shape = jax.ShapeDtypeStruct((n_seq * T, GROUP_W), F32)
    if st_in:
        in_specs.append(st_spec)
        out_specs, out_shape, aliases = [y_spec], [y_shape], {}
    else:
        in_specs.append(pl.BlockSpec(memory_space=pl.ANY))
        out_specs = [y_spec, st_spec]
        out_shape = [y_shape, jax.ShapeDtypeStruct(st_arr.shape, st_arr.dtype)]
        aliases = {len(in_specs) - 1: 1}
    return pl.pallas_call(
        kernel_fn,
        grid=(n_seq // nb,),
        in_specs=in_specs,
        out_specs=out_specs,
        out_shape=out_shape,
        input_output_aliases=aliases,
        scratch_shapes=scratch,
        compiler_params=_cparams(("arbitrary",)),
        name=name,
    )(p, *params, *consts, st_arr)


def _lru_kernel(T, has_init, p_ref, cw_ref, cb_ref, wcat_ref, bcat_ref, lam_ref, st_ref, y_ref, *rest):
    fin_ref = None if has_init else rest[0]
    a_sc, b_sc, h_sc, al_sc, bl_sc, c_sc = rest[-6:]
    SB = SCAN_BLOCK
    n_blk = T // SB
    n_slab = GROUP_W // PAIR_W
    ybr = p_ref[:, 0:GROUP_W]
    xbr = p_ref[:, GROUP_W:2 * GROUP_W]
    xc = _conv_rows(xbr, cw_ref, cb_ref)
    gates = _sigmoid(_mm(xc, wcat_ref[...]) + bcat_ref[...])
    sp = _softplus(-lam_ref[...])
    chains = [(d, hv) for d in range(2) for hv in range(n_slab)]
    for d in range(2):
        r = gates[:, (2 * d) * GROUP_W:(2 * d + 1) * GROUP_W]
        ig = gates[:, (2 * d + 1) * GROUP_W:(2 * d + 2) * GROUP_W]
        log_a = (-LRU_C) * r * sp[d:d + 1, :]
        a = jnp.exp(log_a)
        b = jnp.sqrt(jnp.maximum(1.0 - a * a, 0.0)) * (ig * xc)
        for hv in range(n_slab):
            a_sc[d, hv] = a[:, hv * PAIR_W:(hv + 1) * PAIR_W]
            b_sc[d, hv] = b[:, hv * PAIR_W:(hv + 1) * PAIR_W]

    def rows_of(ref, r):
        return ref[pl.ds(r, n_blk, stride=SB), :]

    acc, bcc = {}, {}
    for d, hv in chains:
        order = list(range(SB)) if d == 0 else list(range(SB - 1, -1, -1))
        prev = None
        for r in order:
            a_r, b_r = rows_of(a_sc.at[d, hv], r), rows_of(b_sc.at[d, hv], r)
            if prev is None:
                acc[d, hv, r], bcc[d, hv, r] = a_r, b_r
            else:
                acc[d, hv, r] = a_r * acc[d, hv, prev]
                bcc[d, hv, r] = a_r * bcc[d, hv, prev] + b_r
            prev = r
        al_sc[d, hv] = acc[d, hv, prev]
        bl_sc[d, hv] = bcc[d, hv, prev]

    def body(j, carry):
        new = []
        for (d, hv), c in zip(chains, carry):
            jb = j if d == 0 else n_blk - 1 - j
            c_sc[d, hv, pl.ds(jb, 1), :] = c
            new.append(al_sc[d, hv, pl.ds(jb, 1), :] * c + bl_sc[d, hv, pl.ds(jb, 1), :])
        return tuple(new)

    if has_init:
        carry0 = tuple(st_ref[0, d:d + 1, hv * PAIR_W:(hv + 1) * PAIR_W] for d, hv in chains)
    else:
        carry0 = tuple(jnp.zeros((1, PAIR_W), F32) for _ in chains)
    last = lax.fori_loop(0, n_blk, body, carry0, unroll=SB)
    for (d, hv), c in zip(chains, last):
        if not has_init:
            fin_ref[0, d:d + 1, hv * PAIR_W:(hv + 1) * PAIR_W] = c
        cin = c_sc[d, hv]
        for r in range(SB):
            h_sc[d, hv, pl.ds(r, n_blk, stride=SB), :] = acc[d, hv, r] * cin + bcc[d, hv, r]
    total = jnp.concatenate([h_sc[0, hv] + h_sc[1, hv] for hv in range(n_slab)], axis=-1)
    y_ref[...] = total * _gelu_tanh(ybr)


def _lru_call(p, T, n_seq, blk0, l, params, state):
    n_slab = GROUP_W // PAIR_W
    full = pltpu.VMEM((2, n_slab, T, PAIR_W), F32)
    edge = pltpu.VMEM((2, n_slab, T // SCAN_BLOCK, PAIR_W), F32)
    return _mixer_call(functools.partial(_lru_kernel, T, state[1]), "rglru_T%d" % T, p, LR_COLS, T,
                       n_seq, blk0, 1, l, params, (), state, (2, GROUP_W),
                       [full, full, full, edge, edge, edge])


def _ssd_kernel(T, has_init, p_ref, cw_ref, cb_ref, dtb_ref, aneg_ref, dexp_ref, nrm_ref, st_ref,
                y_ref, *rest):
    fin_ref = None if has_init else rest[0]
    Q = SSD_CHUNK
    nq = T // Q
    z = p_ref[:, 0:GROUP_W]
    xbc = _silu(_conv_rows(p_ref[:, GROUP_W:4 * GROUP_W], cw_ref, cb_ref))
    xs = xbc[:, 0:GROUP_W]
    bm = xbc[:, GROUP_W:2 * GROUP_W]
    cm = xbc[:, 2 * GROUP_W:3 * GROUP_W]
    dt = _softplus(p_ref[:, 4 * GROUP_W:4 * GROUP_W + PAIR_W] + dtb_ref[...])
    da = dt * aneg_ref[...]
    da_sc, ac_sc, ct_sc = rest[-3:]
    da_sc[...] = da
    _seg_cumsum_slab(da_sc, ac_sc.at[0], ct_sc, T, Q, rev=False)
    _seg_cumsum_slab(da_sc, ac_sc.at[1], ct_sc, T, Q, rev=True)
    acf = ac_sc[0]
    acb = ac_sc[1]
    ti = _row_iota((Q, Q))
    si = _lane_iota((Q, Q))
    low = si <= ti
    upp = si >= ti
    lane_lo = _lane_iota((Q, PAIR_W)) < HEAD_DIM
    row_lo = _row_iota((PAIR_W, SSD_STATE)) < HEAD_DIM

    def col(a, j):
        return a[:, j:j + 1]

    ydiag, ds, dec, ecol = [], [], [], []
    for q in range(nq):
        sl = slice(q * Q, (q + 1) * Q)
        acf_q, acb_q, dt_q = acf[sl], acb[sl], dt[sl]
        acf_t, acb_t, ldt_t = acf_q.T, acb_q.T, jnp.log(dt_q).T
        y_q, ds_q, dec_q, ec_q = [], [], [], []
        for pr in range(N_PAIRS):
            ls = slice(pr * PAIR_W, (pr + 1) * PAIR_W)
            bg, cg, xp = bm[sl, ls], cm[sl, ls], xs[sl, ls]
            g = _mm_nt(cg, bg)
            outs = []
            wf_cols, wb_cols, ef_cols, eb_cols, decf, decb = [], [], [], [], [], []
            for hh in range(2):
                h = 2 * pr + hh
                hb_ = N_HEADS + h
                lf = jnp.exp(col(acf_q, h) - (acf_t[h:h + 1, :] - ldt_t[h:h + 1, :]))
                lb = jnp.exp(col(acb_q, hb_) - (acb_t[hb_:hb_ + 1, :] - ldt_t[hb_:hb_ + 1, :]))
                mh = g * (jnp.where(low, lf, 0.0) + jnp.where(upp, lb, 0.0))
                outs.append(_mm(mh, xp))
                af_last = acf_q[Q - 1:Q, h:h + 1]
                ab_last = acb_q[0:1, hb_:hb_ + 1]
                wf_cols.append(jnp.exp(af_last - col(acf_q, h)) * col(dt_q, h))
                wb_cols.append(jnp.exp(ab_last - col(acb_q, hb_)) * col(dt_q, hb_))
                ef_cols.append(jnp.exp(col(acf_q, h)))
                eb_cols.append(jnp.exp(col(acb_q, hb_)))
                decf.append(jnp.exp(af_last))
                decb.append(jnp.exp(ab_last))
            y_q.append(jnp.where(lane_lo, outs[0], outs[1]))
            wf = jnp.where(lane_lo, wf_cols[0], wf_cols[1])
            wb = jnp.where(lane_lo, wb_cols[0], wb_cols[1])
            ds_q.append((_mm_tn(xp * wf, bg), _mm_tn(xp * wb, bg)))
            dec_q.append((jnp.where(row_lo, decf[0], decf[1]), jnp.where(row_lo, decb[0], decb[1])))
            ec_q.append((jnp.where(lane_lo, ef_cols[0], ef_cols[1]),
                         jnp.where(lane_lo, eb_cols[0], eb_cols[1])))
        ydiag.append(y_q)
        ds.append(ds_q)
        dec.append(dec_q)
        ecol.append(ec_q)

    prev = [[[None, None] for _ in range(N_PAIRS)] for _ in range(nq)]
    for pr in range(N_PAIRS):
        for d in range(2):
            if has_init:
                s = jnp.concatenate([st_ref[0, d, 2 * pr], st_ref[0, d, 2 * pr + 1]], axis=0)
            else:
                s = jnp.zeros((PAIR_W, SSD_STATE), F32)
            order = range(nq) if d == 0 else range(nq - 1, -1, -1)
            for q in order:
                prev[q][pr][d] = s
                s = dec[q][pr][d] * s + ds[q][pr][d]
            if not has_init:
                fin_ref[0, d, 2 * pr] = s[0:HEAD_DIM, :]
                fin_ref[0, d, 2 * pr + 1] = s[HEAD_DIM:PAIR_W, :]

    ys = []
    for q in range(nq):
        sl = slice(q * Q, (q + 1) * Q)
        parts = []
        for pr in range(N_PAIRS):
            ls = slice(pr * PAIR_W, (pr + 1) * PAIR_W)
            yp = ydiag[q][pr]
            if has_init or nq > 1:
                cg = cm[sl, ls]
                for d in range(2):
                    yp = yp + _mm_nt(cg, prev[q][pr][d]) * ecol[q][pr][d]
            parts.append(yp)
        ys.append(jnp.concatenate(parts, axis=-1))
    y = jnp.concatenate(ys, axis=0) if nq > 1 else ys[0]
    y = y + dexp_ref[...] * xs
    y_ref[...] = _rms(y * _silu(z), nrm_ref[...])


def _ssd_call(p, T, n_seq, blk0, l, params, state):
    return _mixer_call(functools.partial(_ssd_kernel, T, state[1]), "ssd_T%d" % T, p, SS_PAD, T,
                       n_seq, blk0, 1, l, params, (), state, (2, N_HEADS, HEAD_DIM, SSD_STATE),
                       [pltpu.VMEM((T, PAIR_W), F32), pltpu.VMEM((2, T, PAIR_W), F32),
                        pltpu.VMEM((T // SUBLANES, PAIR_W), F32)])


def _hgrn_kernel(T, has_init, p_ref, la_ref, l1_ref, lbd_ref, nrm_ref, bd_ref, st_ref, y_ref, *rest):
    fin_ref = None if has_init else rest[0]
    q_sc, v_sc, k_sc, b_sc, o_sc, t_sc = rest[-6:]
    NB = HG_BLOCK
    n_blk = T // NB

    def put(dst, val):
        for pr in range(N_PAIRS):
            dst[pr] = val[:, pr * PAIR_W:(pr + 1) * PAIR_W]

    put(q_sc, _silu(p_ref[:, 0:GROUP_W]))
    put(v_sc, p_ref[:, GROUP_W:2 * GROUP_W])
    for d in range(2):
        x = p_ref[:, (2 + d) * GROUP_W:(3 + d) * GROUP_W]
        e1 = jnp.exp(-jnp.abs(x))
        lsig = jnp.minimum(x, 0.0) - jnp.log(1.0 + e1)
        a_ = la_ref[d:d + 1, :] + jnp.zeros_like(x)
        b_ = l1_ref[d:d + 1, :] + lsig
        logf = jnp.maximum(a_, b_) + jnp.log(1.0 + jnp.exp(-jnp.abs(a_ - b_)))
        put(k_sc.at[d], jnp.exp(l1_ref[d:d + 1, :]) * (jnp.where(x >= 0.0, e1, 1.0) / (1.0 + e1))
            - lbd_ref[d:d + 1, :])
        put(b_sc.at[d], logf)
        for pr in range(N_PAIRS):
            _seg_cumsum_slab(b_sc.at[d, pr], b_sc.at[d, pr], t_sc, T, NB, rev=(d == 1))
    bd = bd_ref[0:PAIR_W, 0:PAIR_W]
    rows8 = _row_iota((SUBLANES, PAIR_W))
    bdm = (_row_iota((PAIR_W, PAIR_W)) < HEAD_DIM) == (_lane_iota((PAIR_W, PAIR_W)) < HEAD_DIM)
    chains = [(d, pr) for d in range(2) for pr in range(N_PAIRS)]
    n_half = NB // SUBLANES

    def body(j, states):
        new_states = []
        for d, pr in chains:
            jb = j if d == 0 else n_blk - 1 - j
            r0 = pl.multiple_of(jb * NB, NB)
            blk = pl.ds(r0, NB)
            qb, vb, kb, bb = q_sc[pr, blk, :], v_sc[pr, blk, :], k_sc[d, pr, blk, :], b_sc[d, pr, blk, :]

            def row(ref, s):
                return jnp.broadcast_to(ref[pl.ds(r0 + s, 1), :], (SUBLANES, PAIR_W))

            parts, where = [], []
            for s in range(NB):
                b_s, k_s = row(b_sc.at[d, pr], s), row(k_sc.at[d, pr], s)
                for hf in range(n_half):
                    t0 = hf * SUBLANES
                    t1 = t0 + SUBLANES - 1
                    if (t1 < s) if d == 0 else (t0 > s):
                        continue
                    rs = slice(t0, t0 + SUBLANES)
                    pr_ = qb[rs] * jnp.exp(bb[rs] - b_s) * k_s
                    if not ((t0 >= s) if d == 0 else (t1 <= s)):
                        keep = (rows8 + t0 >= s) if d == 0 else (rows8 + t0 <= s)
                        pr_ = jnp.where(keep, pr_, 0.0)
                    parts.append(pr_)
                    where.append((s, hf))
            att = _mm(jnp.concatenate(parts, axis=0), bd)
            halves = [jnp.zeros((SUBLANES, PAIR_W), F32) for _ in range(n_half)]
            for i, (s, hf) in enumerate(where):
                halves[hf] = halves[hf] + att[i * SUBLANES:(i + 1) * SUBLANES, :] * row(v_sc.at[pr], s)
            o = jnp.concatenate(halves, axis=0)
            bl = bb[NB - 1:NB, :] if d == 0 else bb[0:1, :]
            st = states[d * N_PAIRS + pr]
            o_sc[d, blk, pr * PAIR_W:(pr + 1) * PAIR_W] = o + _mm_nt(qb * jnp.exp(bb), st)
            new_states.append(st * jnp.exp(bl)
                              + jnp.where(bdm, _mm_tn(vb, kb * jnp.exp(bl - bb)), 0.0))
        return tuple(new_states)

    if has_init:
        states0 = tuple(_pair_blockdiag(st_ref[0, d, 2 * pr], st_ref[0, d, 2 * pr + 1]).T
                        for d, pr in chains)
    else:
        states0 = tuple(jnp.zeros((PAIR_W, PAIR_W), F32) for _ in chains)
    states = lax.fori_loop(0, n_blk, body, states0, unroll=HG_UNROLL)
    if not has_init:
        for (d, pr), s_ in zip(chains, states):
            kv = s_.T
            for hh in range(2):
                fin_ref[0, d, 2 * pr + hh] = _pair_diag(kv, hh)
    g = p_ref[:, 4 * GROUP_W:5 * GROUP_W]
    y_ref[...] = _rms(o_sc[0] + o_sc[1], nrm_ref[...]) * _silu(g)


def _hgrn_call(p, T, n_seq, blk0, l, params, bd, state):
    pw = pltpu.VMEM((N_PAIRS, T, PAIR_W), F32)
    pw2 = pltpu.VMEM((2, N_PAIRS, T, PAIR_W), F32)
    return _mixer_call(functools.partial(_hgrn_kernel, T, state[1]), "hgrn2_T%d" % T, p, HG_COLS, T,
                       n_seq, blk0, 1, l, params, (bd,), state, (2, N_HEADS, HEAD_DIM, HEAD_DIM),
                       [pw, pw, pw2, pw2, pltpu.VMEM((2, T, GROUP_W), F32),
                        pltpu.VMEM((T // SUBLANES, PAIR_W), F32)])


def _rwkv_kernel(T, nb, lat, p_ref, mu_ref, w0_ref, a0_ref, wl_ref, kk_ref, ka_ref, rk_ref,
                 lnw_ref, lnb_ref, sm_ref, bd_ref, st_ref, y_ref, *rest):
    fin_ref = None if lat else rest[0]
    xa_sc, xr_sc, yb_sc, yk_sc, pc_sc, v_sc, yo_sc, g_sc, rk_sc, cs_sc, ct_sc = rest[-11:]
    C = RW_CHUNK
    n_chunk = T // C
    bd = bd_ref[...]
    rowi = _row_iota((T, RW_PAD))
    if lat:
        gcol = rowi % GRID_W
        shifts = [(-1, gcol >= 1), (1, gcol <= GRID_W - 2), (-GRID_W, rowi >= GRID_W),
                  (GRID_W, rowi < T - GRID_W)]
    else:
        shifts = [(-1, rowi >= 1), (1, rowi <= T - 2)]
    ln = _lane_iota((T, PAIR_W))
    n_t = 2 * RW_LORA_W
    n_i = 4 * RW_LORA_W
    n_s = n_i + RW_LORA_G
    for sq in range(nb):
        rows = slice(sq * T, (sq + 1) * T)
        P = p_ref[rows, :]
        sh = jnp.zeros_like(P)
        for j, (dlt, valid) in enumerate(shifts):
            sh = jnp.where(valid & (sm_ref[j:j + 1, :] > 0.5), _shift_rows(P, dlt), sh)
        P = P + (sh - P) * mu_ref[...]
        r = P[:, 0:GROUP_W]
        k = P[:, GROUP_W:2 * GROUP_W]
        lo = P[:, 3 * GROUP_W:3 * GROUP_W + PAIR_W]
        act = jnp.where(ln < n_t, jnp.tanh(lo),
                        jnp.where(ln < n_i, lo, jnp.where(ln < n_s, _sigmoid(lo), 0.0)))
        lora = _mm(act, wl_ref[...])
        g_sc[rows, :] = lora[:, 4 * GROUP_W:5 * GROUP_W]
        kkv = k * kk_ref[...]
        kkn = kkv / jnp.maximum(jnp.sqrt(_mm_split(kkv * kkv, bd)), 1e-12)
        v_sc[rows, :] = P[:, 2 * GROUP_W:3 * GROUP_W]
        ksum = jnp.zeros_like(k)
        for d in range(2):
            w_lora = lora[:, d * GROUP_W:(d + 1) * GROUP_W]
            a_lora = lora[:, (2 + d) * GROUP_W:(3 + d) * GROUP_W]
            w_log = -_softplus(-(w0_ref[d:d + 1, :] + w_lora)) - 0.5
            logw = -jnp.exp(w_log)
            a_sig = _sigmoid(a0_ref[d:d + 1, :] + a_lora)
            k_d = k * (1.0 + (a_sig - 1.0) * ka_ref[...])
            ksum = ksum + k_d
            edge = C - 1 if d == 0 else 0
            for pr in range(N_PAIRS):
                ls = slice(pr * PAIR_W, (pr + 1) * PAIR_W)
                cs_sc[pr] = logw[:, ls]
                _seg_cumsum_slab(cs_sc.at[pr], cs_sc.at[pr], ct_sc, T, C, rev=(d == 1))
                pc_sc[d, pr, sq * n_chunk:(sq + 1) * n_chunk, :] = jnp.exp(
                    cs_sc[pr, pl.ds(edge, n_chunk, stride=C), :])
            c = jnp.concatenate([cs_sc[pr] for pr in range(N_PAIRS)], axis=-1)
            einv = jnp.exp(-c)
            xa_sc[d, rows, :] = ((-kkn) * jnp.exp(c - logw)).astype(BF16)
            xr_sc[d, rows, :] = (r * jnp.exp(c)).astype(BF16)
            yb_sc[d, rows, :] = ((kkn * a_sig) * einv).astype(BF16)
            yk_sc[d, rows, :] = (k_d * einv).astype(BF16)
        rk_sc[rows, :] = r * ksum * rk_ref[...]

    R2 = 2 * C
    assert R2 % PAIR_W == 0
    lane_lo = _lane_iota((C, PAIR_W)) < HEAD_DIM
    tt2 = _row_iota((R2, 2 * R2)) % C
    ss2 = _lane_iota((R2, 2 * R2)) % C
    n_fac = C.bit_length() - 1

    def stack2(x):
        z = jnp.zeros_like(x)
        return jnp.concatenate([jnp.where(lane_lo, x, z), jnp.where(lane_lo, z, x)], axis=0)

    chains = [(sq, d, pr) for sq in range(nb) for d in range(2) for pr in range(N_PAIRS)]
    each = lambda f, *cols: [f(*a) for a in zip(*cols)]

    def body(j, states):
        lhs, rhs_t, vst, pc, strict2, incl2, dst = [], [], [], [], [], [], []
        for sq, d, pr in chains:
            jc = j if d == 0 else n_chunk - 1 - j
            r0 = pl.multiple_of(sq * T + jc * C, C)
            ls = slice(pr * PAIR_W, (pr + 1) * PAIR_W)
            at = xa_sc[d, pl.ds(r0, C), ls]
            rt = xr_sc[d, pl.ds(r0, C), ls]
            bt = yb_sc[d, pl.ds(r0, C), ls]
            kt = yk_sc[d, pl.ds(r0, C), ls]
            pc.append(pc_sc[d, pr, pl.ds(sq * n_chunk + jc, 1), :])
            lhs.append(jnp.concatenate([stack2(at), stack2(rt)], axis=0))
            rhs_t.append(jnp.concatenate([stack2(bt), stack2(kt)], axis=0))
            vst.append(stack2(v_sc[pl.ds(r0, C), ls]))
            strict2.append((ss2 < tt2) if d == 0 else (ss2 > tt2))
            incl2.append((ss2 <= tt2) if d == 0 else (ss2 >= tt2))
            dst.append((d, r0, ls))
        gg = each(_mm_nt, lhs, rhs_t)
        uu = each(_mm_nt, lhs, states)
        am = each(lambda g, m: jnp.where(m, g[0:R2, :], 0.0), gg, strict2)
        ncat = each(lambda g, m: jnp.where(m, g[R2:2 * R2, :], 0.0), gg, incl2)
        qb = each(lambda a_: a_[:, 0:R2].astype(BF16), am)
        xs = each(lambda u, a_, v_: u[0:R2, :] + _mm(a_[:, R2:2 * R2], v_), uu, am, vst)
        for i in range(n_fac):
            xh = each(lambda x: x.astype(BF16), xs)
            if i < RW_SPLIT_STAGES:
                xl = each(lambda x, h: (x - h.astype(F32)).astype(BF16), xs, xh)
                corr = each(lambda q, l_: jnp.dot(q, l_, preferred_element_type=F32), qb, xl)
            else:
                corr = [0.0] * len(xs)
            if i < n_fac - 1:
                zz = each(lambda q, h: jnp.dot(q, jnp.concatenate([h, q], axis=1),
                                               preferred_element_type=F32), qb, xh)
                xs = each(lambda x, z, c_: x + z[:, 0:PAIR_W] + c_, xs, zz, corr)
                qb = each(lambda z: z[:, PAIR_W:PAIR_W + R2].astype(BF16), zz)
            else:
                xs = each(lambda x, q, h, c_: x + jnp.dot(q, h, preferred_element_type=F32) + c_,
                          xs, qb, xh, corr)
        sv = each(lambda x, v_: jnp.concatenate([x, v_], axis=0), xs, vst)
        yst = each(lambda u, n, s_: u[R2:2 * R2, :] + _mm(n, s_), uu, ncat, sv)
        for (d, r0, ls), y_ in zip(dst, yst):
            yo_sc[d, pl.ds(r0, C), ls] = y_[0:C, :] + y_[C:2 * C, :]
        return tuple(each(lambda s_, p, v_, r_: s_ * p + _mm_tn(v_, r_) * p, states, pc, sv, rhs_t))

    if lat:
        states0 = tuple(_pair_blockdiag(st_ref[sq, d, 2 * pr], st_ref[sq, d, 2 * pr + 1])
                        for sq, d, pr in chains)
    else:
        states0 = tuple(jnp.zeros((PAIR_W, PAIR_W), F32) for _ in chains)
    states = lax.fori_loop(0, n_chunk, body, states0)
    if not lat:
        for (sq, d, pr), s_ in zip(chains, states):
            for hh in range(2):
                fin_ref[sq, d, 2 * pr + hh] = _pair_diag(s_, hh)
    inv_n = 1.0 / HEAD_DIM
    for sq in range(nb):
        rows = slice(sq * T, (sq + 1) * T)
        y = yo_sc[0, rows, :] + yo_sc[1, rows, :]
        mu = _mm_split(y, bd) * inv_n
        yc = y - mu
        var = _mm_split(yc * yc, bd) * inv_n
        yn = yc * lax.rsqrt(var + RW_GN_EPS) * lnw_ref[...] + lnb_ref[...]
        bonus = _mm_split(rk_sc[rows, :], bd) * v_sc[rows, :]
        y_ref[rows, :] = (yn + bonus) * g_sc[rows, :]


def _rwkv_call(p, T, n_seq, blk0, nb, lat, l, params, sm, bd, state):
    tw = pltpu.VMEM((nb * T, GROUP_W), F32)
    tw2 = pltpu.VMEM((2, nb * T, GROUP_W), F32)
    op2 = pltpu.VMEM((2, nb * T, GROUP_W), BF16)
    return _mixer_call(functools.partial(_rwkv_kernel, T, nb, lat), "rwkv7_T%d" % T, p, RW_PAD, T,
                       n_seq, blk0, nb, l, params, (sm, bd), state,
                       (2, N_HEADS, HEAD_DIM, HEAD_DIM),
                       [op2, op2, op2, op2,
                        pltpu.VMEM((2, N_PAIRS, nb * T // RW_CHUNK, PAIR_W), F32),
                        tw, tw2, tw, tw,
                        pltpu.VMEM((N_PAIRS, T, PAIR_W), F32),
                        pltpu.VMEM((T // SUBLANES, PAIR_W), F32)],
                       single_buffer_p=(nb > 1 and lat))


def _pad_last(a, n):
    return jnp.pad(a, ((0, 0),) * (a.ndim - 1) + ((0, n - a.shape[-1]),))


def _row(a):
    return a.reshape(a.shape[0], 1, -1)


def _heads_blockdiag(w):
    eye = jnp.eye(N_HEADS, dtype=w.dtype)
    full = w[..., :, :, None, :] * eye[:, None, :, None]
    return full.reshape(w.shape[:-3] + (GROUP_W, GROUP_W))


def kernel(x_prompt, x_sample, state_rwkv, state_hgrn, state_ssd, state_lru, c, c_ctx, mod_w, mod_b, norm_g, w_in, w_out, rw_mu, rw_w0, rw_w2, rw_a0, rw_a2, rw_g2, rw_kk, rw_ka, rw_rk, rw_ln_w, rw_ln_b, hg_lb, hg_norm, ss_conv_w, ss_conv_b, ss_dt_bias, ss_A_log, ss_D, ss_norm, lr_conv_w, lr_conv_b, lr_wa, lr_ba, lr_wx, lr_bx, lr_lam, mlp_w1, mlp_w2):
    bp, t_ctx, _ = x_prompt.shape
    bl, t_lat, _ = x_sample.shape
    n_ctx_tok = bp * t_ctx
    assert n_ctx_tok % t_lat == 0 and t_lat % GRID_W == 0 and 1 + bl <= 8
    lat_blk0 = n_ctx_tok // t_lat
    tiles = _Tiles(n_ctx_tok, bl * t_lat, t_lat)

    x_ctx = x_prompt.reshape(n_ctx_tok, D_MODEL)
    x_lat = x_sample.reshape(bl * t_lat, D_MODEL)
    cond8 = jnp.zeros((8, D_MODEL), F32).at[0].set(c_ctx).at[1:1 + bl].set(c)
    mod, ws_in = _mod_call(cond8, mod_w, mod_b, w_in)
    mod = mod.reshape(DEPTH, 8, MOD_CH, D_MODEL)

    lane = jnp.arange(GROUP_W)
    bd_ones = (lane[:, None] // HEAD_DIM == lane[None, :] // HEAD_DIM).astype(BF16)

    col = jnp.arange(RW_PAD)
    quarter = col // RW_SHIFT_SPLIT
    real = col < RW_COLS
    sm_lat = jnp.stack([(quarter == i) & real for i in range(4)]).astype(F32)
    sm_ctx = jnp.stack([(quarter < 2) & real, (quarter >= 2) & real,
                        jnp.zeros_like(real), jnp.zeros_like(real)]).astype(F32)
    lora_blocks = (rw_w2[:, 0], rw_w2[:, 1], rw_a2[:, 0], rw_a2[:, 1], rw_g2)
    wl = jnp.concatenate(
        [jnp.pad(blk, ((0, 0), (0, 0), (j * GROUP_W, (len(lora_blocks) - 1 - j) * GROUP_W)))
         for j, blk in enumerate(lora_blocks)], axis=1)
    wl = jnp.pad(wl, ((0, 0), (0, PAIR_W - wl.shape[1]), (0, 0))).astype(BF16)
    rw_params = (_pad_last(_row(rw_mu), RW_PAD), rw_w0, rw_a0, wl, _row(rw_kk), _row(rw_ka),
                 _row(rw_rk), _row(rw_ln_w), _row(rw_ln_b))

    lb_soft = jax.nn.softmax(hg_lb.astype(F32), axis=0)
    lower = jnp.cumsum(lb_soft, axis=0) - lb_soft[0]
    lb_floor = jnp.maximum(lower, LB_FLOOR)
    hg_params = (jnp.log(lb_floor), jnp.log1p(-lower), lb_floor - lower, _row(hg_norm))

    ss_params = (ss_conv_w, _row(ss_conv_b), _pad_last(_row(ss_dt_bias), PAIR_W),
                 _pad_last(_row(-jnp.exp(ss_A_log.astype(F32))), PAIR_W),
                 _row(jnp.repeat(ss_D, HEAD_DIM, axis=-1)), _row(ss_norm))

    wcat = jnp.concatenate([_heads_blockdiag(lr_wa[:, 0]), _heads_blockdiag(lr_wx[:, 0]),
                            _heads_blockdiag(lr_wa[:, 1]), _heads_blockdiag(lr_wx[:, 1])],
                           axis=-1).astype(BF16)
    bcat = _row(jnp.stack([lr_ba[:, 0], lr_bx[:, 0], lr_ba[:, 1], lr_bx[:, 1]], axis=1))
    lr_params = (lr_conv_w, _row(lr_conv_b), wcat, bcat, lr_lam)

    new_rw = jnp.zeros((bp, DEPTH, 2, N_HEADS, HEAD_DIM, HEAD_DIM), F32)
    new_hg = jnp.zeros((bp, DEPTH, 2, N_HEADS, HEAD_DIM, HEAD_DIM), F32)
    new_ss = jnp.zeros((bp, DEPTH, 2, N_HEADS, HEAD_DIM, SSD_STATE), F32)
    new_lr = jnp.zeros((bp, DEPTH, 2, GROUP_W), F32)

    for l in range(DEPTH):
        (p_rw, p_hg, p_ss, p_lr), mlp_b = _inproj_call(tiles, l, x_ctx, x_lat, mod, norm_g, ws_in,
                                                       (w_out, mlp_w1, mlp_w2))

        y_rw_c, new_rw = _rwkv_call(p_rw, t_ctx, bp, 0, RW_CTX_SEQS, False, l, rw_params, sm_ctx,
                                    bd_ones, (new_rw, False))
        y_rw_l, = _rwkv_call(p_rw, t_lat, bl, lat_blk0, RW_LAT_SEQS, True, l, rw_params, sm_lat,
                             bd_ones, (state_rwkv, True))
        y_hg_c, new_hg = _hgrn_call(p_hg, t_ctx, bp, 0, l, hg_params, bd_ones, (new_hg, False))
        y_hg_l, = _hgrn_call(p_hg, t_lat, bl, lat_blk0, l, hg_params, bd_ones, (state_hgrn, True))
        y_ss_c, new_ss = _ssd_call(p_ss, t_ctx, bp, 0, l, ss_params, (new_ss, False))
        y_ss_l, = _ssd_call(p_ss, t_lat, bl, lat_blk0, l, ss_params, (state_ssd, True))
        y_lr_c, new_lr = _lru_call(p_lr, t_ctx, bp, 0, l, lr_params, (new_lr, False))
        y_lr_l, = _lru_call(p_lr, t_lat, bl, lat_blk0, l, lr_params, (state_lru, True))

        x_ctx, x_lat, ws_in = _outmlp_call(tiles, l, x_ctx, x_lat, (y_rw_c, y_hg_c, y_ss_c, y_lr_c),
                                           (y_rw_l, y_hg_l, y_ss_l, y_lr_l), mod, norm_g, *mlp_b,
                                           w_in if l + 1 < DEPTH else None)

    return (x_ctx.reshape(bp, t_ctx, D_MODEL), x_lat.reshape(bl, t_lat, D_MODEL),
            new_rw, new_hg, new_ss, new_lr)
```

```python
import functools

import jax
import jax.numpy as jnp
from jax import lax
from jax.experimental import pallas as pl
from jax.experimental.pallas import tpu as pltpu

F32 = jnp.float32
BF16 = jnp.bfloat16

D_MODEL = 1024
DEPTH = 4
MOD_CH = 6
GROUP_W = 256
HEAD_DIM = 64
N_HEADS = 4
N_PAIRS = 2
PAIR_W = 128
D_FF = 4096
GRID_W = 64
NORM_EPS = 1e-6
LB_FLOOR = 1e-30
RW_GN_EPS = 64e-5
LRU_C = 8.0
SSD_STATE = 128
CONV_TAPS = (-2, -1, 0, 1)

RW_COLS = 864
RW_PAD = 896
RW_SHIFT_SPLIT = RW_COLS // 4
RW_LORA_W = 16
RW_LORA_G = 32
HG_COLS = 1280
SS_COLS = 1032
SS_PAD = 1152
LR_COLS = 512
RW_CHUNK = 64
RW_SPLIT_STAGES = 4
RW_SINGLE_BUF_ROWS = 2048
RW_CTX_SEQS = 4
RW_LAT_SEQS = 2
HG_BLOCK = 16
HG_UNROLL = 8
SSD_CHUNK = 256
SUBLANES = 8
SCAN_BLOCK = SUBLANES

TOK_TILE = 512
VMEM_LIMIT = 56 * 1024 * 1024


def _cparams(sem):
    return pltpu.CompilerParams(dimension_semantics=sem, vmem_limit_bytes=VMEM_LIMIT)


def _layer_spec(a, l, **kw):
    return pl.BlockSpec((None,) + a.shape[1:], lambda *_: (l,) + (0,) * (a.ndim - 1), **kw)


def _const_spec(a):
    return pl.BlockSpec(a.shape, lambda *_: (0,) * a.ndim)


def _mm(a, b):
    return jnp.dot(a.astype(BF16), b.astype(BF16), preferred_element_type=F32)


def _mm_nt(a, b):
    return lax.dot_general(a.astype(BF16), b.astype(BF16), (((1,), (1,)), ((), ())),
                           preferred_element_type=F32)


def _mm_tn(a, b):
    return lax.dot_general(a.astype(BF16), b.astype(BF16), (((0,), (0,)), ((), ())),
                           preferred_element_type=F32)


def _mm_split(a, b_bf16):
    hi = a.astype(BF16)
    lo = (a - hi.astype(F32)).astype(BF16)
    return (jnp.dot(hi, b_bf16, preferred_element_type=F32)
            + jnp.dot(lo, b_bf16, preferred_element_type=F32))


def _row_iota(shape):
    return lax.broadcasted_iota(jnp.int32, shape, 0)


def _lane_iota(shape):
    return lax.broadcasted_iota(jnp.int32, shape, 1)


def _shift_rows(x, d):
    n = x.shape[0]
    s = (-d) % n
    if s == 0:
        return x
    return pltpu.roll(x, s, 0)


def _seg_cumsum_slab(src, dst, tot, n, seg, rev):
    sb = SUBLANES
    nb = n // sb
    g = seg // sb
    order = list(range(sb))[::-1] if rev else list(range(sb))
    part, prev = {}, None
    for r in order:
        x_r = src[pl.ds(r, nb, stride=sb), :]
        part[r] = x_r if prev is None else part[prev] + x_r
        prev = r
    off = None
    if g > 1:
        tot[...] = part[prev]
        ng = nb // g
        qorder = list(range(g))[::-1] if rev else list(range(g))
        run, excl = None, {}
        for q in qorder:
            excl[q] = run
            t_q = tot[pl.ds(q, ng, stride=g), :]
            run = t_q if run is None else run + t_q
        for q in qorder:
            tot[pl.ds(q, ng, stride=g), :] = jnp.zeros((ng, PAIR_W), F32) if excl[q] is None else excl[q]
        off = tot[...]
    for r in range(sb):
        dst[pl.ds(r, nb, stride=sb), :] = part[r] if off is None else part[r] + off


def _sigmoid(x):
    return jax.nn.sigmoid(x)


def _silu(x):
    return x * _sigmoid(x)


def _softplus(x):
    return jnp.maximum(x, 0.0) + jnp.log(1.0 + jnp.exp(-jnp.abs(x)))


def _gelu_tanh(x):
    c = 0.7978845608028654
    return 0.5 * x * (1.0 + jnp.tanh(c * (x + 0.044715 * (x * x * x))))


def _rms(x, g):
    return x * lax.rsqrt(jnp.mean(x * x, axis=-1, keepdims=True) + NORM_EPS) * g


def _conv_rows(x, w_ref, b_ref):
    t = x.shape[0]
    rows = _row_iota(x.shape)
    y = b_ref[...] + jnp.zeros_like(x)
    for j, d in enumerate(CONV_TAPS):
        valid = (rows + d >= 0) & (rows + d < t)
        y = y + jnp.where(valid, _shift_rows(x, d), 0.0) * w_ref[j:j + 1, :]
    return y


def _pair_blockdiag(a, b):
    z = jnp.zeros_like(a)
    return jnp.concatenate([jnp.concatenate([a, z], axis=1), jnp.concatenate([z, b], axis=1)], axis=0)


def _pair_diag(m, hh):
    return m[hh * HEAD_DIM:(hh + 1) * HEAD_DIM, hh * HEAD_DIM:(hh + 1) * HEAD_DIM]


CAST_STEPS = 16
IN_SPLITS = (0, RW_COLS, RW_COLS + HG_COLS, RW_COLS + HG_COLS + SS_COLS, RW_COLS + HG_COLS + SS_COLS + LR_COLS)
IN_WIDTHS = (RW_PAD, HG_COLS, SS_PAD, LR_COLS)


def _cast_block_specs(a, l, step_of):
    rows = a.shape[1] // CAST_STEPS
    blk = lambda *i: jnp.minimum(step_of(*i), CAST_STEPS - 1)
    in_spec = pl.BlockSpec((None, rows) + a.shape[2:], lambda *i: (l, blk(*i), 0))
    return in_spec, rows, blk


def _win_cast_specs(w_in, l, step_of):
    in_spec, rows, blk = _cast_block_specs(w_in, l, step_of)
    out_specs = [pl.BlockSpec((rows, n), lambda *i: (blk(*i), 0)) for n in IN_WIDTHS]
    out_shape = [jax.ShapeDtypeStruct((w_in.shape[1], n), BF16) for n in IN_WIDTHS]
    return in_spec, out_specs, out_shape


def _win_cast(w_ref, o_refs):
    w = w_ref[...]
    for j, o in enumerate(o_refs):
        piece = w[:, IN_SPLITS[j]:IN_SPLITS[j + 1]]
        pad = IN_WIDTHS[j] - piece.shape[1]
        if pad:
            piece = jnp.concatenate([piece, jnp.zeros((piece.shape[0], pad), F32)], axis=1)
        o[...] = piece.astype(BF16)


def _mod_kernel(c_ref, w_ref, b_ref, win_ref, o_ref, *w_out_refs):
    c = c_ref[...]
    o_ref[...] = _mm(_silu(c), w_ref[...]) + b_ref[...]

    @pl.when(pl.program_id(0) * MOD_CH + pl.program_id(1) < CAST_STEPS)
    def _():
        _win_cast(win_ref, w_out_refs)


def _mod_call(cond8, mod_w, mod_b, w_in):
    assert DEPTH * MOD_CH >= CAST_STEPS
    win_spec, wo_specs, wo_shape = _win_cast_specs(w_in, 0, lambda l, j: l * MOD_CH + j)
    outs = pl.pallas_call(
        _mod_kernel,
        grid=(DEPTH, MOD_CH),
        in_specs=[
            pl.BlockSpec((8, D_MODEL), lambda l, j: (0, 0)),
            pl.BlockSpec((None, D_MODEL, D_MODEL), lambda l, j: (l, 0, j)),
            pl.BlockSpec((None, 1, D_MODEL), lambda l, j: (l, 0, j)),
            win_spec,
        ],
        out_specs=[pl.BlockSpec((None, 8, D_MODEL), lambda l, j: (l, 0, j))] + wo_specs,
        out_shape=[jax.ShapeDtypeStruct((DEPTH, 8, MOD_CH * D_MODEL), F32)] + wo_shape,
        compiler_params=_cparams(("arbitrary", "arbitrary")),
        name="modulation",
    )(cond8, mod_w, mod_b.reshape(DEPTH, 1, MOD_CH * D_MODEL), w_in)
    return outs[0], tuple(outs[1:])


class _Tiles:
    def __init__(self, n_ctx_tok, n_lat_tok, t_lat):
        assert n_ctx_tok % TOK_TILE == 0 and t_lat % TOK_TILE == 0
        self.n_ctx = n_ctx_tok // TOK_TILE
        self.n_lat = n_lat_tok // TOK_TILE
        self.per_seq = t_lat // TOK_TILE
        self.n = self.n_ctx + self.n_lat

    def ctx_idx(self, i):
        return jnp.minimum(i, self.n_ctx - 1)

    def lat_idx(self, i):
        return jnp.clip(i - self.n_ctx, 0, self.n_lat - 1)

    def seq_of(self, i):
        return jnp.where(i < self.n_ctx, 0, 1 + (i - self.n_ctx) // self.per_seq)


def _inproj_kernel(n_ctx_tiles, xc_ref, xl_ref, mod_ref, g_ref, wrw, whg, wss, wlr, f0, f1, f2,
                   orw, ohg, oss, olr, b0, b1, b2):
    x = jnp.where(pl.program_id(0) < n_ctx_tiles, xc_ref[...], xl_ref[...])
    m = mod_ref[...]
    h = _rms(x, g_ref[0:1, :]) * (1.0 + m[1:2, :]) + m[0:1, :]
    hb = h.astype(BF16)
    orw[...] = jnp.dot(hb, wrw[...], preferred_element_type=F32)
    ohg[...] = jnp.dot(hb, whg[...], preferred_element_type=F32)
    oss[...] = jnp.dot(hb, wss[...], preferred_element_type=F32)
    olr[...] = jnp.dot(hb, wlr[...], preferred_element_type=F32)

    @pl.when(pl.program_id(0) < CAST_STEPS)
    def _():
        for f, b in ((f0, b0), (f1, b1), (f2, b2)):
            b[...] = f[...].astype(BF16)


def _inproj_call(tiles, l, x_ctx, x_lat, mod, norm_g, ws, mlp_f32):
    assert tiles.n >= CAST_STEPS
    n_tok = tiles.n * TOK_TILE
    cast_in, cast_out, cast_shape = [], [], []
    for a in mlp_f32:
        in_spec, rows, blk = _cast_block_specs(a, l, lambda i: i)
        cast_in.append(in_spec)
        cast_out.append(pl.BlockSpec((rows, a.shape[2]), lambda i, blk=blk: (blk(i), 0)))
        cast_shape.append(jax.ShapeDtypeStruct(a.shape[1:], BF16))
    outs = pl.pallas_call(
        functools.partial(_inproj_kernel, tiles.n_ctx),
        grid=(tiles.n,),
        in_specs=[
            pl.BlockSpec((TOK_TILE, D_MODEL), lambda i: (tiles.ctx_idx(i), 0)),
            pl.BlockSpec((TOK_TILE, D_MODEL), lambda i: (tiles.lat_idx(i), 0)),
            pl.BlockSpec((None, None, MOD_CH, D_MODEL), lambda i: (l, tiles.seq_of(i), 0, 0)),
            _layer_spec(norm_g, l),
        ] + [_const_spec(w) for w in ws] + cast_in,
        out_specs=[pl.BlockSpec((TOK_TILE, w), lambda i: (i, 0)) for w in IN_WIDTHS] + cast_out,
        out_shape=[jax.ShapeDtypeStruct((n_tok, w), F32) for w in IN_WIDTHS] + cast_shape,
        compiler_params=_cparams(("arbitrary",)),
        name="in_proj",
    )(x_ctx, x_lat, mod, norm_g, *ws, *mlp_f32)
    return outs[:4], outs[4:]


def _outmlp_kernel(n_ctx_tiles, cast_next, xc_ref, xl_ref, c0, c1, c2, c3, l0, l1, l2, l3, mod_ref,
                   g_ref, wout, w1, w2, *rest):
    if cast_next:
        win_ref, oc_ref, ol_ref = rest[0], rest[1], rest[2]

        @pl.when(pl.program_id(0) < CAST_STEPS)
        def _():
            _win_cast(win_ref, rest[3:])
    else:
        oc_ref, ol_ref = rest
    is_ctx = pl.program_id(0) < n_ctx_tiles
    mix = jnp.concatenate(
        [jnp.where(is_ctx, c[...], l[...]) for c, l in ((c0, l0), (c1, l1), (c2, l2), (c3, l3))],
        axis=-1).astype(BF16)
    m = mod_ref[...]
    g = g_ref[...]
    x = jnp.where(is_ctx, xc_ref[...], xl_ref[...])
    t = jnp.dot(mix, wout[...], preferred_element_type=F32)
    x1 = x + m[2:3, :] * _rms(t, g[1:2, :])
    h2 = (_rms(x1, g[2:3, :]) * (1.0 + m[4:5, :]) + m[3:4, :]).astype(BF16)
    acc = jnp.zeros_like(x)
    for f in range(D_FF // D_MODEL):
        u = jnp.dot(h2, w1[:, f * D_MODEL:(f + 1) * D_MODEL], preferred_element_type=F32)
        u = jnp.square(jnp.maximum(u, 0.0)).astype(BF16)
        acc = acc + jnp.dot(u, w2[f * D_MODEL:(f + 1) * D_MODEL, :], preferred_element_type=F32)
    res = x1 + m[5:6, :] * _rms(acc, g[3:4, :])

    @pl.when(is_ctx)
    def _():
        oc_ref[...] = res

    @pl.when(jnp.logical_not(is_ctx))
    def _():
        ol_ref[...] = res


def _outmlp_call(tiles, l, x_ctx, x_lat, ys_ctx, ys_lat, mod, norm_g, wout, w1, w2, w_in_next):
    ctx_x = pl.BlockSpec((TOK_TILE, D_MODEL), lambda i: (tiles.ctx_idx(i), 0))
    lat_x = pl.BlockSpec((TOK_TILE, D_MODEL), lambda i: (tiles.lat_idx(i), 0))
    ctx_y = pl.BlockSpec((TOK_TILE, GROUP_W), lambda i: (tiles.ctx_idx(i), 0))
    lat_y = pl.BlockSpec((TOK_TILE, GROUP_W), lambda i: (tiles.lat_idx(i), 0))
    single = lambda a: pl.BlockSpec(a.shape, lambda i: (0,) * a.ndim, pipeline_mode=pl.Buffered(1))
    in_specs = [ctx_x, lat_x] + [ctx_y] * 4 + [lat_y] * 4 + [
        pl.BlockSpec((None, None, MOD_CH, D_MODEL), lambda i: (l, tiles.seq_of(i), 0, 0)),
        _layer_spec(norm_g, l), single(wout), single(w1), single(w2)]
    out_specs = [ctx_x, lat_x]
    out_shape = [jax.ShapeDtypeStruct(x_ctx.shape, F32), jax.ShapeDtypeStruct(x_lat.shape, F32)]
    args = [x_ctx, x_lat, *ys_ctx, *ys_lat, mod, norm_g, wout, w1, w2]
    cast_next = w_in_next is not None
    if cast_next:
        assert tiles.n >= CAST_STEPS
        win_spec, wo_specs, wo_shape = _win_cast_specs(w_in_next, l + 1, lambda i: i)
        in_specs.append(win_spec)
        out_specs += wo_specs
        out_shape += wo_shape
        args.append(w_in_next)
    outs = pl.pallas_call(
        functools.partial(_outmlp_kernel, tiles.n_ctx, cast_next),
        grid=(tiles.n,),
        in_specs=in_specs,
        out_specs=out_specs,
        out_shape=out_shape,
        compiler_params=_cparams(("arbitrary",)),
        name="out_mlp",
    )(*args)
    return outs[0], outs[1], tuple(outs[2:])


def _mixer_call(kernel_fn, name, p, width, T, n_seq, blk0, nb, l, params, consts, state, state_blk,
                scratch, single_buffer_p=False):
    assert n_seq % nb == 0 and blk0 % nb == 0
    st_arr, st_in = state
    st_spec = pl.BlockSpec((nb, None) + state_blk, lambda i: (i, l) + (0,) * len(state_blk))
    p_mode = dict(pipeline_mode=pl.Buffered(1)) if single_buffer_p else {}
    in_specs = [pl.BlockSpec((nb * T, width), lambda i: (blk0 // nb + i, 0), **p_mode)]
    in_specs += [_layer_spec(a, l) for a in params] + [_const_spec(a) for a in consts]
    y_spec = pl.BlockSpec((nb * T, GROUP_W), lambda i: (i, 0))
    y_shape = jax.ShapeDtypeStruct((n_seq * T, GROUP_W), F32)
    if st_in:
        in_specs.append(st_spec)
        out_specs, out_shape, aliases = [y_spec], [y_shape], {}
    else:
        fresh = isinstance(st_arr, jax.ShapeDtypeStruct)
        in_specs.append(pl.BlockSpec(memory_space=pl.ANY))
        out_specs = [y_spec, st_spec]
        out_shape = [y_shape, jax.ShapeDtypeStruct(st_arr.shape, st_arr.dtype)]
        aliases = {} if fresh else {len(in_specs) - 1: 1}
        if fresh:
            st_arr = p
    return pl.pallas_call(
        kernel_fn,
        grid=(n_seq // nb,),
        in_specs=in_specs,
        out_specs=out_specs,
        out_shape=out_shape,
        input_output_aliases=aliases,
        scratch_shapes=scratch,
        compiler_params=_cparams(("arbitrary",)),
        name=name,
    )(p, *params, *consts, st_arr)


def _lru_kernel(T, has_init, p_ref, cw_ref, cb_ref, wcat_ref, bcat_ref, lam_ref, st_ref, y_ref, *rest):
    fin_ref = None if has_init else rest[0]
    a_sc, b_sc, h_sc, al_sc, bl_sc, c_sc = rest[-6:]
    SB = SCAN_BLOCK
    n_blk = T // SB
    n_slab = GROUP_W // PAIR_W
    ybr = p_ref[:, 0:GROUP_W]
    xbr = p_ref[:, GROUP_W:2 * GROUP_W]
    xc = _conv_rows(xbr, cw_ref, cb_ref)
    gates = _sigmoid(_mm(xc, wcat_ref[...]) + bcat_ref[...])
    sp = _softplus(-lam_ref[...])
    chains = [(d, hv) for d in range(2) for hv in range(n_slab)]
    for d in range(2):
        r = gates[:, (2 * d) * GROUP_W:(2 * d + 1) * GROUP_W]
        ig = gates[:, (2 * d + 1) * GROUP_W:(2 * d + 2) * GROUP_W]
        log_a = (-LRU_C) * r * sp[d:d + 1, :]
        a = jnp.exp(log_a)
        b = jnp.sqrt(jnp.maximum(1.0 - a * a, 0.0)) * (ig * xc)
        for hv in range(n_slab):
            a_sc[d, hv] = a[:, hv * PAIR_W:(hv + 1) * PAIR_W]
            b_sc[d, hv] = b[:, hv * PAIR_W:(hv + 1) * PAIR_W]

    def rows_of(ref, r):
        return ref[pl.ds(r, n_blk, stride=SB), :]

    acc, bcc = {}, {}
    for d, hv in chains:
        order = list(range(SB)) if d == 0 else list(range(SB - 1, -1, -1))
        prev = None
        for r in order:
            a_r, b_r = rows_of(a_sc.at[d, hv], r), rows_of(b_sc.at[d, hv], r)
            if prev is None:
                acc[d, hv, r], bcc[d, hv, r] = a_r, b_r
            else:
                acc[d, hv, r] = a_r * acc[d, hv, prev]
                bcc[d, hv, r] = a_r * bcc[d, hv, prev] + b_r
            prev = r
        al_sc[d, hv] = acc[d, hv, prev]
        bl_sc[d, hv] = bcc[d, hv, prev]

    def body(j, carry):
        new = []
        for (d, hv), c in zip(chains, carry):
            jb = j if d == 0 else n_blk - 1 - j
            c_sc[d, hv, pl.ds(jb, 1), :] = c
            new.append(al_sc[d, hv, pl.ds(jb, 1), :] * c + bl_sc[d, hv, pl.ds(jb, 1), :])
        return tuple(new)

    if has_init:
        carry0 = tuple(st_ref[0, d:d + 1, hv * PAIR_W:(hv + 1) * PAIR_W] for d, hv in chains)
    else:
        carry0 = tuple(jnp.zeros((1, PAIR_W), F32) for _ in chains)
    last = lax.fori_loop(0, n_blk, body, carry0, unroll=SB)
    for (d, hv), c in zip(chains, last):
        if not has_init:
            fin_ref[0, d:d + 1, hv * PAIR_W:(hv + 1) * PAIR_W] = c
        cin = c_sc[d, hv]
        for r in range(SB):
            h_sc[d, hv, pl.ds(r, n_blk, stride=SB), :] = acc[d, hv, r] * cin + bcc[d, hv, r]
    total = jnp.concatenate([h_sc[0, hv] + h_sc[1, hv] for hv in range(n_slab)], axis=-1)
    y_ref[...] = total * _gelu_tanh(ybr)


def _lru_call(p, T, n_seq, blk0, l, params, state):
    n_slab = GROUP_W // PAIR_W
    full = pltpu.VMEM((2, n_slab, T, PAIR_W), F32)
    edge = pltpu.VMEM((2, n_slab, T // SCAN_BLOCK, PAIR_W), F32)
    return _mixer_call(functools.partial(_lru_kernel, T, state[1]), "rglru_T%d" % T, p, LR_COLS, T,
                       n_seq, blk0, 1, l, params, (), state, (2, GROUP_W),
                       [full, full, full, edge, edge, edge])


def _ssd_kernel(T, has_init, p_ref, cw_ref, cb_ref, dtb_ref, aneg_ref, dexp_ref, nrm_ref, st_ref,
                y_ref, *rest):
    fin_ref = None if has_init else rest[0]
    Q = SSD_CHUNK
    nq = T // Q
    z = p_ref[:, 0:GROUP_W]
    xbc = _silu(_conv_rows(p_ref[:, GROUP_W:4 * GROUP_W], cw_ref, cb_ref))
    xs = xbc[:, 0:GROUP_W]
    bm = xbc[:, GROUP_W:2 * GROUP_W]
    cm = xbc[:, 2 * GROUP_W:3 * GROUP_W]
    dt = _softplus(p_ref[:, 4 * GROUP_W:4 * GROUP_W + PAIR_W] + dtb_ref[...])
    da = dt * aneg_ref[...]
    da_sc, ac_sc, ct_sc = rest[-3:]
    da_sc[...] = da
    _seg_cumsum_slab(da_sc, ac_sc.at[0], ct_sc, T, Q, rev=False)
    _seg_cumsum_slab(da_sc, ac_sc.at[1], ct_sc, T, Q, rev=True)
    acf = ac_sc[0]
    acb = ac_sc[1]
    ti = _row_iota((Q, Q))
    si = _lane_iota((Q, Q))
    low = si <= ti
    upp = si >= ti
    lane_lo = _lane_iota((Q, PAIR_W)) < HEAD_DIM
    row_lo = _row_iota((PAIR_W, SSD_STATE)) < HEAD_DIM

    def col(a, j):
        return a[:, j:j + 1]

    ydiag, ds, dec, ecol = [], [], [], []
    for q in range(nq):
        sl = slice(q * Q, (q + 1) * Q)
        acf_q, acb_q, dt_q = acf[sl], acb[sl], dt[sl]
        acf_t, acb_t, ldt_t = acf_q.T, acb_q.T, jnp.log(dt_q).T
        y_q, ds_q, dec_q, ec_q = [], [], [], []
        for pr in range(N_PAIRS):
            ls = slice(pr * PAIR_W, (pr + 1) * PAIR_W)
            bg, cg, xp = bm[sl, ls], cm[sl, ls], xs[sl, ls]
            g = _mm_nt(cg, bg)
            outs = []
            wf_cols, wb_cols, ef_cols, eb_cols, decf, decb = [], [], [], [], [], []
            for hh in range(2):
                h = 2 * pr + hh
                hb_ = N_HEADS + h
                lf = jnp.exp(col(acf_q, h) - (acf_t[h:h + 1, :] - ldt_t[h:h + 1, :]))
                lb = jnp.exp(col(acb_q, hb_) - (acb_t[hb_:hb_ + 1, :] - ldt_t[hb_:hb_ + 1, :]))
                mh = g * (jnp.where(low, lf, 0.0) + jnp.where(upp, lb, 0.0))
                outs.append(_mm(mh, xp))
                af_last = acf_q[Q - 1:Q, h:h + 1]
                ab_last = acb_q[0:1, hb_:hb_ + 1]
                wf_cols.append(jnp.exp(af_last - col(acf_q, h)) * col(dt_q, h))
                wb_cols.append(jnp.exp(ab_last - col(acb_q, hb_)) * col(dt_q, hb_))
                ef_cols.append(jnp.exp(col(acf_q, h)))
                eb_cols.append(jnp.exp(col(acb_q, hb_)))
                decf.append(jnp.exp(af_last))
                decb.append(jnp.exp(ab_last))
            y_q.append(jnp.where(lane_lo, outs[0], outs[1]))
            wf = jnp.where(lane_lo, wf_cols[0], wf_cols[1])
            wb = jnp.where(lane_lo, wb_cols[0], wb_cols[1])
            ds_q.append((_mm_tn(xp * wf, bg), _mm_tn(xp * wb, bg)))
            dec_q.append((jnp.where(row_lo, decf[0], decf[1]), jnp.where(row_lo, decb[0], decb[1])))
            ec_q.append((jnp.where(lane_lo, ef_cols[0], ef_cols[1]),
                         jnp.where(lane_lo, eb_cols[0], eb_cols[1])))
        ydiag.append(y_q)
        ds.append(ds_q)
        dec.append(dec_q)
        ecol.append(ec_q)

    prev = [[[None, None] for _ in range(N_PAIRS)] for _ in range(nq)]
    for pr in range(N_PAIRS):
        for d in range(2):
            if has_init:
                s = jnp.concatenate([st_ref[0, d, 2 * pr], st_ref[0, d, 2 * pr + 1]], axis=0)
            else:
                s = jnp.zeros((PAIR_W, SSD_STATE), F32)
            order = range(nq) if d == 0 else range(nq - 1, -1, -1)
            for q in order:
                prev[q][pr][d] = s
                s = dec[q][pr][d] * s + ds[q][pr][d]
            if not has_init:
                fin_ref[0, d, 2 * pr] = s[0:HEAD_DIM, :]
                fin_ref[0, d, 2 * pr + 1] = s[HEAD_DIM:PAIR_W, :]

    ys = []
    for q in range(nq):
        sl = slice(q * Q, (q + 1) * Q)
        parts = []
        for pr in range(N_PAIRS):
            ls = slice(pr * PAIR_W, (pr + 1) * PAIR_W)
            yp = ydiag[q][pr]
            if has_init or nq > 1:
                cg = cm[sl, ls]
                for d in range(2):
                    yp = yp + _mm_nt(cg, prev[q][pr][d]) * ecol[q][pr][d]
            parts.append(yp)
        ys.append(jnp.concatenate(parts, axis=-1))
    y = jnp.concatenate(ys, axis=0) if nq > 1 else ys[0]
    y = y + dexp_ref[...] * xs
    y_ref[...] = _rms(y * _silu(z), nrm_ref[...])


def _ssd_call(p, T, n_seq, blk0, l, params, state):
    return _mixer_call(functools.partial(_ssd_kernel, T, state[1]), "ssd_T%d" % T, p, SS_PAD, T,
                       n_seq, blk0, 1, l, params, (), state, (2, N_HEADS, HEAD_DIM, SSD_STATE),
                       [pltpu.VMEM((T, PAIR_W), F32), pltpu.VMEM((2, T, PAIR_W), F32),
                        pltpu.VMEM((T // SUBLANES, PAIR_W), F32)])


def _hgrn_kernel(T, has_init, p_ref, la_ref, l1_ref, lbd_ref, nrm_ref, bd_ref, st_ref, y_ref, *rest):
    fin_ref = None if has_init else rest[0]
    q_sc, v_sc, k_sc, b_sc, o_sc, t_sc = rest[-6:]
    NB = HG_BLOCK
    n_blk = T // NB

    def put(dst, val):
        for pr in range(N_PAIRS):
            dst[pr] = val[:, pr * PAIR_W:(pr + 1) * PAIR_W]

    put(q_sc, _silu(p_ref[:, 0:GROUP_W]))
    put(v_sc, p_ref[:, GROUP_W:2 * GROUP_W])
    for d in range(2):
        x = p_ref[:, (2 + d) * GROUP_W:(3 + d) * GROUP_W]
        e1 = jnp.exp(-jnp.abs(x))
        lsig = jnp.minimum(x, 0.0) - jnp.log(1.0 + e1)
        a_ = la_ref[d:d + 1, :] + jnp.zeros_like(x)
        b_ = l1_ref[d:d + 1, :] + lsig
        logf = jnp.maximum(a_, b_) + jnp.log(1.0 + jnp.exp(-jnp.abs(a_ - b_)))
        put(k_sc.at[d], jnp.exp(l1_ref[d:d + 1, :]) * (jnp.where(x >= 0.0, e1, 1.0) / (1.0 + e1))
            - lbd_ref[d:d + 1, :])
        put(b_sc.at[d], logf)
        for pr in range(N_PAIRS):
            _seg_cumsum_slab(b_sc.at[d, pr], b_sc.at[d, pr], t_sc, T, NB, rev=(d == 1))
    bd = bd_ref[0:PAIR_W, 0:PAIR_W]
    rows8 = _row_iota((SUBLANES, PAIR_W))
    bdm = (_row_iota((PAIR_W, PAIR_W)) < HEAD_DIM) == (_lane_iota((PAIR_W, PAIR_W)) < HEAD_DIM)
    chains = [(d, pr) for d in range(2) for pr in range(N_PAIRS)]
    n_half = NB // SUBLANES

    def body(j, states):
        new_states = []
        for d, pr in chains:
            jb = j if d == 0 else n_blk - 1 - j
            r0 = pl.multiple_of(jb * NB, NB)
            blk = pl.ds(r0, NB)
            qb, vb, kb, bb = q_sc[pr, blk, :], v_sc[pr, blk, :], k_sc[d, pr, blk, :], b_sc[d, pr, blk, :]

            def row(ref, s):
                return jnp.broadcast_to(ref[pl.ds(r0 + s, 1), :], (SUBLANES, PAIR_W))

            parts, where = [], []
            for s in range(NB):
                b_s, k_s = row(b_sc.at[d, pr], s), row(k_sc.at[d, pr], s)
                for hf in range(n_half):
                    t0 = hf * SUBLANES
                    t1 = t0 + SUBLANES - 1
                    if (t1 < s) if d == 0 else (t0 > s):
                        continue
                    rs = slice(t0, t0 + SUBLANES)
                    pr_ = qb[rs] * jnp.exp(bb[rs] - b_s) * k_s
                    if not ((t0 >= s) if d == 0 else (t1 <= s)):
                        keep = (rows8 + t0 >= s) if d == 0 else (rows8 + t0 <= s)
                        pr_ = jnp.where(keep, pr_, 0.0)
                    parts.append(pr_)
                    where.append((s, hf))
            att = _mm(jnp.concatenate(parts, axis=0), bd)
            halves = [jnp.zeros((SUBLANES, PAIR_W), F32) for _ in range(n_half)]
            for i, (s, hf) in enumerate(where):
                halves[hf] = halves[hf] + att[i * SUBLANES:(i + 1) * SUBLANES, :] * row(v_sc.at[pr], s)
            o = jnp.concatenate(halves, axis=0)
            bl = bb[NB - 1:NB, :] if d == 0 else bb[0:1, :]
            st = states[d * N_PAIRS + pr]
            o_sc[d, blk, pr * PAIR_W:(pr + 1) * PAIR_W] = o + _mm_nt(qb * jnp.exp(bb), st)
            new_states.append(st * jnp.exp(bl)
                              + jnp.where(bdm, _mm_tn(vb, kb * jnp.exp(bl - bb)), 0.0))
        return tuple(new_states)

    if has_init:
        states0 = tuple(_pair_blockdiag(st_ref[0, d, 2 * pr], st_ref[0, d, 2 * pr + 1]).T
                        for d, pr in chains)
    else:
        states0 = tuple(jnp.zeros((PAIR_W, PAIR_W), F32) for _ in chains)
    states = lax.fori_loop(0, n_blk, body, states0, unroll=HG_UNROLL)
    if not has_init:
        for (d, pr), s_ in zip(chains, states):
            kv = s_.T
            for hh in range(2):
                fin_ref[0, d, 2 * pr + hh] = _pair_diag(kv, hh)
    g = p_ref[:, 4 * GROUP_W:5 * GROUP_W]
    y_ref[...] = _rms(o_sc[0] + o_sc[1], nrm_ref[...]) * _silu(g)


def _hgrn_call(p, T, n_seq, blk0, l, params, bd, state):
    pw = pltpu.VMEM((N_PAIRS, T, PAIR_W), F32)
    pw2 = pltpu.VMEM((2, N_PAIRS, T, PAIR_W), F32)
    return _mixer_call(functools.partial(_hgrn_kernel, T, state[1]), "hgrn2_T%d" % T, p, HG_COLS, T,
                       n_seq, blk0, 1, l, params, (bd,), state, (2, N_HEADS, HEAD_DIM, HEAD_DIM),
                       [pw, pw, pw2, pw2, pltpu.VMEM((2, T, GROUP_W), F32),
                        pltpu.VMEM((T // SUBLANES, PAIR_W), F32)])


def _rwkv_kernel(T, nb, lat, p_ref, mu_ref, w0_ref, a0_ref, wl_ref, kk_ref, ka_ref, rk_ref,
                 lnw_ref, lnb_ref, sm_ref, bd_ref, st_ref, y_ref, *rest):
    fin_ref = None if lat else rest[0]
    xa_sc, xr_sc, yb_sc, yk_sc, pc_sc, v_sc, yo_sc, g_sc, rk_sc, cs_sc, ct_sc = rest[-11:]
    C = RW_CHUNK
    n_chunk = T // C
    bd = bd_ref[...]
    rowi = _row_iota((T, RW_PAD))
    if lat:
        gcol = rowi % GRID_W
        shifts = [(-1, gcol >= 1), (1, gcol <= GRID_W - 2), (-GRID_W, rowi >= GRID_W),
                  (GRID_W, rowi < T - GRID_W)]
    else:
        shifts = [(-1, rowi >= 1), (1, rowi <= T - 2)]
    ln = _lane_iota((T, PAIR_W))
    n_t = 2 * RW_LORA_W
    n_i = 4 * RW_LORA_W
    n_s = n_i + RW_LORA_G
    for sq in range(nb):
        rows = slice(sq * T, (sq + 1) * T)
        P = p_ref[rows, :]
        sh = jnp.zeros_like(P)
        for j, (dlt, valid) in enumerate(shifts):
            sh = jnp.where(valid & (sm_ref[j:j + 1, :] > 0.5), _shift_rows(P, dlt), sh)
        P = P + (sh - P) * mu_ref[...]
        r = P[:, 0:GROUP_W]
        k = P[:, GROUP_W:2 * GROUP_W]
        lo = P[:, 3 * GROUP_W:3 * GROUP_W + PAIR_W]
        act = jnp.where(ln < n_t, jnp.tanh(lo),
                        jnp.where(ln < n_i, lo, jnp.where(ln < n_s, _sigmoid(lo), 0.0)))
        lora = _mm(act, wl_ref[...])
        g_sc[rows, :] = lora[:, 4 * GROUP_W:5 * GROUP_W]
        kkv = k * kk_ref[...]
        kkn = kkv / jnp.maximum(jnp.sqrt(_mm_split(kkv * kkv, bd)), 1e-12)
        v_sc[rows, :] = P[:, 2 * GROUP_W:3 * GROUP_W]
        ksum = jnp.zeros_like(k)
        for d in range(2):
            w_lora = lora[:, d * GROUP_W:(d + 1) * GROUP_W]
            a_lora = lora[:, (2 + d) * GROUP_W:(3 + d) * GROUP_W]
            w_log = -_softplus(-(w0_ref[d:d + 1, :] + w_lora)) - 0.5
            logw = -jnp.exp(w_log)
            a_sig = _sigmoid(a0_ref[d:d + 1, :] + a_lora)
            k_d = k * (1.0 + (a_sig - 1.0) * ka_ref[...])
            ksum = ksum + k_d
            edge = C - 1 if d == 0 else 0
            for pr in range(N_PAIRS):
                ls = slice(pr * PAIR_W, (pr + 1) * PAIR_W)
                cs_sc[pr] = logw[:, ls]
                _seg_cumsum_slab(cs_sc.at[pr], cs_sc.at[pr], ct_sc, T, C, rev=(d == 1))
                pc_sc[d, pr, sq * n_chunk:(sq + 1) * n_chunk, :] = jnp.exp(
                    cs_sc[pr, pl.ds(edge, n_chunk, stride=C), :])
            c = jnp.concatenate([cs_sc[pr] for pr in range(N_PAIRS)], axis=-1)
            einv = jnp.exp(-c)
            xa_sc[d, rows, :] = ((-kkn) * jnp.exp(c - logw)).astype(BF16)
            xr_sc[d, rows, :] = (r * jnp.exp(c)).astype(BF16)
            yb_sc[d, rows, :] = ((kkn * a_sig) * einv).astype(BF16)
            yk_sc[d, rows, :] = (k_d * einv).astype(BF16)
        rk_sc[rows, :] = r * ksum * rk_ref[...]

    R2 = 2 * C
    assert R2 % PAIR_W == 0
    lane_lo = _lane_iota((C, PAIR_W)) < HEAD_DIM
    tt2 = _row_iota((R2, 2 * R2)) % C
    ss2 = _lane_iota((R2, 2 * R2)) % C
    n_fac = C.bit_length() - 1

    def stack2(x):
        z = jnp.zeros_like(x)
        return jnp.concatenate([jnp.where(lane_lo, x, z), jnp.where(lane_lo, z, x)], axis=0)

    chains = [(sq, d, pr) for sq in range(nb) for d in range(2) for pr in range(N_PAIRS)]
    each = lambda f, *cols: [f(*a) for a in zip(*cols)]

    def body(j, states):
        lhs, rhs_t, vst, pc, strict2, incl2, dst = [], [], [], [], [], [], []
        for sq, d, pr in chains:
            jc = j if d == 0 else n_chunk - 1 - j
            r0 = pl.multiple_of(sq * T + jc * C, C)
            ls = slice(pr * PAIR_W, (pr + 1) * PAIR_W)
            at = xa_sc[d, pl.ds(r0, C), ls]
            rt = xr_sc[d, pl.ds(r0, C), ls]
            bt = yb_sc[d, pl.ds(r0, C), ls]
            kt = yk_sc[d, pl.ds(r0, C), ls]
            pc.append(pc_sc[d, pr, pl.ds(sq * n_chunk + jc, 1), :])
            lhs.append(jnp.concatenate([stack2(at), stack2(rt)], axis=0))
            rhs_t.append(jnp.concatenate([stack2(bt), stack2(kt)], axis=0))
            vst.append(stack2(v_sc[pl.ds(r0, C), ls]))
            strict2.append((ss2 < tt2) if d == 0 else (ss2 > tt2))
            incl2.append((ss2 <= tt2) if d == 0 else (ss2 >= tt2))
            dst.append((d, r0, ls))
        gg = each(_mm_nt, lhs, rhs_t)
        uu = each(_mm_nt, lhs, states)
        am = each(lambda g, m: jnp.where(m, g[0:R2, :], 0.0), gg, strict2)
        ncat = each(lambda g, m: jnp.where(m, g[R2:2 * R2, :], 0.0), gg, incl2)
        qb = each(lambda a_: a_[:, 0:R2].astype(BF16), am)
        xs = each(lambda u, a_, v_: u[0:R2, :] + _mm(a_[:, R2:2 * R2], v_), uu, am, vst)
        for i in range(n_fac):
            xh = each(lambda x: x.astype(BF16), xs)
            if i < RW_SPLIT_STAGES:
                xl = each(lambda x, h: (x - h.astype(F32)).astype(BF16), xs, xh)
                corr = each(lambda q, l_: jnp.dot(q, l_, preferred_element_type=F32), qb, xl)
            else:
                corr = [0.0] * len(xs)
            if i < n_fac - 1:
                zz = each(lambda q, h: jnp.dot(q, jnp.concatenate([h, q], axis=1),
                                               preferred_element_type=F32), qb, xh)
                xs = each(lambda x, z, c_: x + z[:, 0:PAIR_W] + c_, xs, zz, corr)
                qb = each(lambda z: z[:, PAIR_W:PAIR_W + R2].astype(BF16), zz)
            else:
                xs = each(lambda x, q, h, c_: x + jnp.dot(q, h, preferred_element_type=F32) + c_,
                          xs, qb, xh, corr)
        sv = each(lambda x, v_: jnp.concatenate([x, v_], axis=0), xs, vst)
        yst = each(lambda u, n, s_: u[R2:2 * R2, :] + _mm(n, s_), uu, ncat, sv)
        for (d, r0, ls), y_ in zip(dst, yst):
            yo_sc[d, pl.ds(r0, C), ls] = y_[0:C, :] + y_[C:2 * C, :]
        return tuple(each(lambda s_, p, v_, r_: s_ * p + _mm_tn(v_, r_) * p, states, pc, sv, rhs_t))

    if lat:
        states0 = tuple(_pair_blockdiag(st_ref[sq, d, 2 * pr], st_ref[sq, d, 2 * pr + 1])
                        for sq, d, pr in chains)
    else:
        states0 = tuple(jnp.zeros((PAIR_W, PAIR_W), F32) for _ in chains)
    states = lax.fori_loop(0, n_chunk, body, states0)
    if not lat:
        for (sq, d, pr), s_ in zip(chains, states):
            for hh in range(2):
                fin_ref[sq, d, 2 * pr + hh] = _pair_diag(s_, hh)
    inv_n = 1.0 / HEAD_DIM
    for sq in range(nb):
        rows = slice(sq * T, (sq + 1) * T)
        y = yo_sc[0, rows, :] + yo_sc[1, rows, :]
        mu = _mm_split(y, bd) * inv_n
        yc = y - mu
        var = _mm_split(yc * yc, bd) * inv_n
        yn = yc * lax.rsqrt(var + RW_GN_EPS) * lnw_ref[...] + lnb_ref[...]
        bonus = _mm_split(rk_sc[rows, :], bd) * v_sc[rows, :]
        y_ref[rows, :] = (yn + bonus) * g_sc[rows, :]


def _rwkv_call(p, T, n_seq, blk0, nb, lat, l, params, sm, bd, state):
    tw = pltpu.VMEM((nb * T, GROUP_W), F32)
    tw2 = pltpu.VMEM((2, nb * T, GROUP_W), F32)
    op2 = pltpu.VMEM((2, nb * T, GROUP_W), BF16)
    return _mixer_call(functools.partial(_rwkv_kernel, T, nb, lat), "rwkv7_T%d" % T, p, RW_PAD, T,
                       n_seq, blk0, nb, l, params, (sm, bd), state,
                       (2, N_HEADS, HEAD_DIM, HEAD_DIM),
                       [op2, op2, op2, op2,
                        pltpu.VMEM((2, N_PAIRS, nb * T // RW_CHUNK, PAIR_W), F32),
                        tw, tw2, tw, tw,
                        pltpu.VMEM((N_PAIRS, T, PAIR_W), F32),
                        pltpu.VMEM((T // SUBLANES, PAIR_W), F32)],
                       single_buffer_p=(nb * T >= RW_SINGLE_BUF_ROWS))


def _pad_last(a, n):
    return jnp.pad(a, ((0, 0),) * (a.ndim - 1) + ((0, n - a.shape[-1]),))


def _row(a):
    return a.reshape(a.shape[0], 1, -1)


def _heads_blockdiag(w):
    eye = jnp.eye(N_HEADS, dtype=w.dtype)
    full = w[..., :, :, None, :] * eye[:, None, :, None]
    return full.reshape(w.shape[:-3] + (GROUP_W, GROUP_W))


def kernel(x_prompt, x_sample, state_rwkv, state_hgrn, state_ssd, state_lru, c, c_ctx, mod_w, mod_b, norm_g, w_in, w_out, rw_mu, rw_w0, rw_w2, rw_a0, rw_a2, rw_g2, rw_kk, rw_ka, rw_rk, rw_ln_w, rw_ln_b, hg_lb, hg_norm, ss_conv_w, ss_conv_b, ss_dt_bias, ss_A_log, ss_D, ss_norm, lr_conv_w, lr_conv_b, lr_wa, lr_ba, lr_wx, lr_bx, lr_lam, mlp_w1, mlp_w2):
    bp, t_ctx, _ = x_prompt.shape
    bl, t_lat, _ = x_sample.shape
    n_ctx_tok = bp * t_ctx
    assert n_ctx_tok % t_lat == 0 and t_lat % GRID_W == 0 and 1 + bl <= 8
    lat_blk0 = n_ctx_tok // t_lat
    tiles = _Tiles(n_ctx_tok, bl * t_lat, t_lat)

    x_ctx = x_prompt.reshape(n_ctx_tok, D_MODEL)
    x_lat = x_sample.reshape(bl * t_lat, D_MODEL)
    cond8 = jnp.zeros((8, D_MODEL), F32).at[0].set(c_ctx).at[1:1 + bl].set(c)
    mod, ws_in = _mod_call(cond8, mod_w, mod_b, w_in)
    mod = mod.reshape(DEPTH, 8, MOD_CH, D_MODEL)

    lane = jnp.arange(GROUP_W)
    bd_ones = (lane[:, None] // HEAD_DIM == lane[None, :] // HEAD_DIM).astype(BF16)

    col = jnp.arange(RW_PAD)
    quarter = col // RW_SHIFT_SPLIT
    real = col < RW_COLS
    sm_lat = jnp.stack([(quarter == i) & real for i in range(4)]).astype(F32)
    sm_ctx = jnp.stack([(quarter < 2) & real, (quarter >= 2) & real,
                        jnp.zeros_like(real), jnp.zeros_like(real)]).astype(F32)
    lora_blocks = (rw_w2[:, 0], rw_w2[:, 1], rw_a2[:, 0], rw_a2[:, 1], rw_g2)
    wl = jnp.concatenate(
        [jnp.pad(blk, ((0, 0), (0, 0), (j * GROUP_W, (len(lora_blocks) - 1 - j) * GROUP_W)))
         for j, blk in enumerate(lora_blocks)], axis=1)
    wl = jnp.pad(wl, ((0, 0), (0, PAIR_W - wl.shape[1]), (0, 0))).astype(BF16)
    rw_params = (_pad_last(_row(rw_mu), RW_PAD), rw_w0, rw_a0, wl, _row(rw_kk), _row(rw_ka),
                 _row(rw_rk), _row(rw_ln_w), _row(rw_ln_b))

    lb_soft = jax.nn.softmax(hg_lb.astype(F32), axis=0)
    lower = jnp.cumsum(lb_soft, axis=0) - lb_soft[0]
    lb_floor = jnp.maximum(lower, LB_FLOOR)
    hg_params = (jnp.log(lb_floor), jnp.log1p(-lower), lb_floor - lower, _row(hg_norm))

    ss_params = (ss_conv_w, _row(ss_conv_b), _pad_last(_row(ss_dt_bias), PAIR_W),
                 _pad_last(_row(-jnp.exp(ss_A_log.astype(F32))), PAIR_W),
                 _row(jnp.repeat(ss_D, HEAD_DIM, axis=-1)), _row(ss_norm))

    wcat = jnp.concatenate([_heads_blockdiag(lr_wa[:, 0]), _heads_blockdiag(lr_wx[:, 0]),
                            _heads_blockdiag(lr_wa[:, 1]), _heads_blockdiag(lr_wx[:, 1])],
                           axis=-1).astype(BF16)
    bcat = _row(jnp.stack([lr_ba[:, 0], lr_bx[:, 0], lr_ba[:, 1], lr_bx[:, 1]], axis=1))
    lr_params = (lr_conv_w, _row(lr_conv_b), wcat, bcat, lr_lam)

    new_rw = jax.ShapeDtypeStruct((bp, DEPTH, 2, N_HEADS, HEAD_DIM, HEAD_DIM), F32)
    new_hg = jax.ShapeDtypeStruct((bp, DEPTH, 2, N_HEADS, HEAD_DIM, HEAD_DIM), F32)
    new_ss = jax.ShapeDtypeStruct((bp, DEPTH, 2, N_HEADS, HEAD_DIM, SSD_STATE), F32)
    new_lr = jax.ShapeDtypeStruct((bp, DEPTH, 2, GROUP_W), F32)

    for l in range(DEPTH):
        (p_rw, p_hg, p_ss, p_lr), mlp_b = _inproj_call(tiles, l, x_ctx, x_lat, mod, norm_g, ws_in,
                                                       (w_out, mlp_w1, mlp_w2))

        y_rw_c, new_rw = _rwkv_call(p_rw, t_ctx, bp, 0, RW_CTX_SEQS, False, l, rw_params, sm_ctx,
                                    bd_ones, (new_rw, False))
        y_rw_l, = _rwkv_call(p_rw, t_lat, bl, lat_blk0, RW_LAT_SEQS, True, l, rw_params, sm_lat,
                             bd_ones, (state_rwkv, True))
        y_hg_c, new_hg = _hgrn_call(p_hg, t_ctx, bp, 0, l, hg_params, bd_ones, (new_hg, False))
        y_hg_l, = _hgrn_call(p_hg, t_lat, bl, lat_blk0, l, hg_params, bd_ones, (state_hgrn, True))
        y_ss_c, new_ss = _ssd_call(p_ss, t_ctx, bp, 0, l, ss_params, (new_ss, False))
        y_ss_l, = _ssd_call(p_ss, t_lat, bl, lat_blk0, l, ss_params, (state_ssd, True))
        y_lr_c, new_lr = _lru_call(p_lr, t_ctx, bp, 0, l, lr_params, (new_lr, False))
        y_lr_l, = _lru_call(p_lr, t_lat, bl, lat_blk0, l, lr_params, (state_lru, True))

        x_ctx, x_lat, ws_in = _outmlp_call(tiles, l, x_ctx, x_lat, (y_rw_c, y_hg_c, y_ss_c, y_lr_c),
                                           (y_rw_l, y_hg_l, y_ss_l, y_lr_l), mod, norm_g, *mlp_b,
                                           w_in if l + 1 < DEPTH else None)

    return (x_ctx.reshape(bp, t_ctx, D_MODEL), x_lat.reshape(bl, t_lat, D_MODEL),
            new_rw, new_hg, new_ss, new_lr)
```

```python
import functools

import jax
import jax.numpy as jnp
from jax import lax
from jax.experimental import pallas as pl
from jax.experimental.pallas import tpu as pltpu

F32 = jnp.float32
BF16 = jnp.bfloat16

D_MODEL = 1024
DEPTH = 4
MOD_CH = 6
GROUP_W = 256
HEAD_DIM = 64
N_HEADS = 4
N_PAIRS = 2
PAIR_W = 128
D_FF = 4096
GRID_W = 64
NORM_EPS = 1e-6
LB_FLOOR = 1e-30
RW_GN_EPS = 64e-5
LRU_C = 8.0
SSD_STATE = 128
CONV_TAPS = (-2, -1, 0, 1)

RW_COLS = 864
RW_PAD = 896
RW_SHIFT_SPLIT = RW_COLS // 4
RW_LORA_W = 16
RW_LORA_G = 32
HG_COLS = 1280
SS_COLS = 1032
SS_PAD = 1152
LR_COLS = 512
RW_CHUNK = 64
RW_SPLIT_STAGES = 4
RW_SINGLE_BUF_ROWS = 2048
RW_CTX_SEQS = 4
RW_LAT_SEQS = 2
HG_BLOCK = 16
HG_UNROLL = 16
SSD_CHUNK = 256
SUBLANES = 8
SCAN_BLOCK = SUBLANES

TOK_TILE = 512
VMEM_LIMIT = 56 * 1024 * 1024


def _cparams(sem):
    return pltpu.CompilerParams(dimension_semantics=sem, vmem_limit_bytes=VMEM_LIMIT)


def _layer_spec(a, l, **kw):
    return pl.BlockSpec((None,) + a.shape[1:], lambda *_: (l,) + (0,) * (a.ndim - 1), **kw)


def _const_spec(a):
    return pl.BlockSpec(a.shape, lambda *_: (0,) * a.ndim)


def _mm(a, b):
    return jnp.dot(a.astype(BF16), b.astype(BF16), preferred_element_type=F32)


def _mm_nt(a, b):
    return lax.dot_general(a.astype(BF16), b.astype(BF16), (((1,), (1,)), ((), ())),
                           preferred_element_type=F32)


def _mm_tn(a, b):
    return lax.dot_general(a.astype(BF16), b.astype(BF16), (((0,), (0,)), ((), ())),
                           preferred_element_type=F32)


def _mm_split(a, b_bf16):
    hi = a.astype(BF16)
    lo = (a - hi.astype(F32)).astype(BF16)
    return (jnp.dot(hi, b_bf16, preferred_element_type=F32)
            + jnp.dot(lo, b_bf16, preferred_element_type=F32))


def _row_iota(shape):
    return lax.broadcasted_iota(jnp.int32, shape, 0)


def _lane_iota(shape):
    return lax.broadcasted_iota(jnp.int32, shape, 1)


def _shift_rows(x, d):
    n = x.shape[0]
    s = (-d) % n
    if s == 0:
        return x
    return pltpu.roll(x, s, 0)


def _seg_cumsum_slab(src, dst, tot, n, seg, rev):
    sb = SUBLANES
    nb = n // sb
    g = seg // sb
    order = list(range(sb))[::-1] if rev else list(range(sb))
    part, prev = {}, None
    for r in order:
        x_r = src[pl.ds(r, nb, stride=sb), :]
        part[r] = x_r if prev is None else part[prev] + x_r
        prev = r
    off = None
    if g > 1:
        tot[...] = part[prev]
        ng = nb // g
        qorder = list(range(g))[::-1] if rev else list(range(g))
        run, excl = None, {}
        for q in qorder:
            excl[q] = run
            t_q = tot[pl.ds(q, ng, stride=g), :]
            run = t_q if run is None else run + t_q
        for q in qorder:
            tot[pl.ds(q, ng, stride=g), :] = jnp.zeros((ng, PAIR_W), F32) if excl[q] is None else excl[q]
        off = tot[...]
    for r in range(sb):
        dst[pl.ds(r, nb, stride=sb), :] = part[r] if off is None else part[r] + off


def _sigmoid(x):
    return jax.nn.sigmoid(x)


def _silu(x):
    return x * _sigmoid(x)


def _softplus(x):
    return jnp.maximum(x, 0.0) + jnp.log(1.0 + jnp.exp(-jnp.abs(x)))


def _gelu_tanh(x):
    c = 0.7978845608028654
    return 0.5 * x * (1.0 + jnp.tanh(c * (x + 0.044715 * (x * x * x))))


def _rms(x, g):
    return x * lax.rsqrt(jnp.mean(x * x, axis=-1, keepdims=True) + NORM_EPS) * g


def _conv_rows(x, w_ref, b_ref):
    t = x.shape[0]
    rows = _row_iota(x.shape)
    y = b_ref[...] + jnp.zeros_like(x)
    for j, d in enumerate(CONV_TAPS):
        valid = (rows + d >= 0) & (rows + d < t)
        y = y + jnp.where(valid, _shift_rows(x, d), 0.0) * w_ref[j:j + 1, :]
    return y


def _pair_blockdiag(a, b):
    z = jnp.zeros_like(a)
    return jnp.concatenate([jnp.concatenate([a, z], axis=1), jnp.concatenate([z, b], axis=1)], axis=0)


def _pair_diag(m, hh):
    return m[hh * HEAD_DIM:(hh + 1) * HEAD_DIM, hh * HEAD_DIM:(hh + 1) * HEAD_DIM]


CAST_STEPS = 16
IN_SPLITS = (0, RW_COLS, RW_COLS + HG_COLS, RW_COLS + HG_COLS + SS_COLS, RW_COLS + HG_COLS + SS_COLS + LR_COLS)
IN_WIDTHS = (RW_PAD, HG_COLS, SS_PAD, LR_COLS)


def _cast_block_specs(a, l, step_of):
    rows = a.shape[1] // CAST_STEPS
    blk = lambda *i: jnp.minimum(step_of(*i), CAST_STEPS - 1)
    in_spec = pl.BlockSpec((None, rows) + a.shape[2:], lambda *i: (l, blk(*i), 0))
    return in_spec, rows, blk


def _win_cast_specs(w_in, l, step_of):
    in_spec, rows, blk = _cast_block_specs(w_in, l, step_of)
    out_specs = [pl.BlockSpec((rows, n), lambda *i: (blk(*i), 0)) for n in IN_WIDTHS]
    out_shape = [jax.ShapeDtypeStruct((w_in.shape[1], n), BF16) for n in IN_WIDTHS]
    return in_spec, out_specs, out_shape


def _win_cast(w_ref, o_refs):
    w = w_ref[...]
    for j, o in enumerate(o_refs):
        piece = w[:, IN_SPLITS[j]:IN_SPLITS[j + 1]]
        pad = IN_WIDTHS[j] - piece.shape[1]
        if pad:
            piece = jnp.concatenate([piece, jnp.zeros((piece.shape[0], pad), F32)], axis=1)
        o[...] = piece.astype(BF16)


def _mod_kernel(c_ref, w_ref, b_ref, win_ref, o_ref, *w_out_refs):
    c = c_ref[...]
    o_ref[...] = _mm(_silu(c), w_ref[...]) + b_ref[...]

    @pl.when(pl.program_id(0) * MOD_CH + pl.program_id(1) < CAST_STEPS)
    def _():
        _win_cast(win_ref, w_out_refs)


def _mod_call(cond8, mod_w, mod_b, w_in):
    assert DEPTH * MOD_CH >= CAST_STEPS
    win_spec, wo_specs, wo_shape = _win_cast_specs(w_in, 0, lambda l, j: l * MOD_CH + j)
    outs = pl.pallas_call(
        _mod_kernel,
        grid=(DEPTH, MOD_CH),
        in_specs=[
            pl.BlockSpec((8, D_MODEL), lambda l, j: (0, 0)),
            pl.BlockSpec((None, D_MODEL, D_MODEL), lambda l, j: (l, 0, j)),
            pl.BlockSpec((None, 1, D_MODEL), lambda l, j: (l, 0, j)),
            win_spec,
        ],
        out_specs=[pl.BlockSpec((None, 8, D_MODEL), lambda l, j: (l, 0, j))] + wo_specs,
        out_shape=[jax.ShapeDtypeStruct((DEPTH, 8, MOD_CH * D_MODEL), F32)] + wo_shape,
        compiler_params=_cparams(("arbitrary", "arbitrary")),
        name="modulation",
    )(cond8, mod_w, mod_b.reshape(DEPTH, 1, MOD_CH * D_MODEL), w_in)
    return outs[0], tuple(outs[1:])


class _Tiles:
    def __init__(self, n_ctx_tok, n_lat_tok, t_lat):
        assert n_ctx_tok % TOK_TILE == 0 and t_lat % TOK_TILE == 0
        self.n_ctx = n_ctx_tok // TOK_TILE
        self.n_lat = n_lat_tok // TOK_TILE
        self.per_seq = t_lat // TOK_TILE
        self.n = self.n_ctx + self.n_lat

    def ctx_idx(self, i):
        return jnp.minimum(i, self.n_ctx - 1)

    def lat_idx(self, i):
        return jnp.clip(i - self.n_ctx, 0, self.n_lat - 1)

    def seq_of(self, i):
        return jnp.where(i < self.n_ctx, 0, 1 + (i - self.n_ctx) // self.per_seq)


def _inproj_kernel(n_ctx_tiles, xc_ref, xl_ref, mod_ref, g_ref, wrw, whg, wss, wlr, f0, f1, f2,
                   orw, ohg, oss, olr, b0, b1, b2):
    x = jnp.where(pl.program_id(0) < n_ctx_tiles, xc_ref[...], xl_ref[...])
    m = mod_ref[...]
    h = _rms(x, g_ref[0:1, :]) * (1.0 + m[1:2, :]) + m[0:1, :]
    hb = h.astype(BF16)
    orw[...] = jnp.dot(hb, wrw[...], preferred_element_type=F32)
    ohg[...] = jnp.dot(hb, whg[...], preferred_element_type=F32)
    oss[...] = jnp.dot(hb, wss[...], preferred_element_type=F32)
    olr[...] = jnp.dot(hb, wlr[...], preferred_element_type=F32)

    @pl.when(pl.program_id(0) < CAST_STEPS)
    def _():
        for f, b in ((f0, b0), (f1, b1), (f2, b2)):
            b[...] = f[...].astype(BF16)


def _inproj_call(tiles, l, x_ctx, x_lat, mod, norm_g, ws, mlp_f32):
    assert tiles.n >= CAST_STEPS
    n_tok = tiles.n * TOK_TILE
    cast_in, cast_out, cast_shape = [], [], []
    for a in mlp_f32:
        in_spec, rows, blk = _cast_block_specs(a, l, lambda i: i)
        cast_in.append(in_spec)
        cast_out.append(pl.BlockSpec((rows, a.shape[2]), lambda i, blk=blk: (blk(i), 0)))
        cast_shape.append(jax.ShapeDtypeStruct(a.shape[1:], BF16))
    outs = pl.pallas_call(
        functools.partial(_inproj_kernel, tiles.n_ctx),
        grid=(tiles.n,),
        in_specs=[
            pl.BlockSpec((TOK_TILE, D_MODEL), lambda i: (tiles.ctx_idx(i), 0)),
            pl.BlockSpec((TOK_TILE, D_MODEL), lambda i: (tiles.lat_idx(i), 0)),
            pl.BlockSpec((None, None, MOD_CH, D_MODEL), lambda i: (l, tiles.seq_of(i), 0, 0)),
            _layer_spec(norm_g, l),
        ] + [_const_spec(w) for w in ws] + cast_in,
        out_specs=[pl.BlockSpec((TOK_TILE, w), lambda i: (i, 0)) for w in IN_WIDTHS] + cast_out,
        out_shape=[jax.ShapeDtypeStruct((n_tok, w), F32) for w in IN_WIDTHS] + cast_shape,
        compiler_params=_cparams(("arbitrary",)),
        name="in_proj",
    )(x_ctx, x_lat, mod, norm_g, *ws, *mlp_f32)
    return outs[:4], outs[4:]


def _outmlp_kernel(n_ctx_tiles, cast_next, xc_ref, xl_ref, c0, c1, c2, c3, l0, l1, l2, l3, mod_ref,
                   g_ref, wout, w1, w2, *rest):
    if cast_next:
        win_ref, oc_ref, ol_ref = rest[0], rest[1], rest[2]

        @pl.when(pl.program_id(0) < CAST_STEPS)
        def _():
            _win_cast(win_ref, rest[3:])
    else:
        oc_ref, ol_ref = rest
    is_ctx = pl.program_id(0) < n_ctx_tiles
    mix = jnp.concatenate(
        [jnp.where(is_ctx, c[...], l[...]) for c, l in ((c0, l0), (c1, l1), (c2, l2), (c3, l3))],
        axis=-1).astype(BF16)
    m = mod_ref[...]
    g = g_ref[...]
    x = jnp.where(is_ctx, xc_ref[...], xl_ref[...])
    t = jnp.dot(mix, wout[...], preferred_element_type=F32)
    x1 = x + m[2:3, :] * _rms(t, g[1:2, :])
    h2 = (_rms(x1, g[2:3, :]) * (1.0 + m[4:5, :]) + m[3:4, :]).astype(BF16)
    acc = jnp.zeros_like(x)
    for f in range(D_FF // D_MODEL):
        u = jnp.dot(h2, w1[:, f * D_MODEL:(f + 1) * D_MODEL], preferred_element_type=F32)
        u = jnp.square(jnp.maximum(u, 0.0)).astype(BF16)
        acc = acc + jnp.dot(u, w2[f * D_MODEL:(f + 1) * D_MODEL, :], preferred_element_type=F32)
    res = x1 + m[5:6, :] * _rms(acc, g[3:4, :])

    @pl.when(is_ctx)
    def _():
        oc_ref[...] = res

    @pl.when(jnp.logical_not(is_ctx))
    def _():
        ol_ref[...] = res


def _outmlp_call(tiles, l, x_ctx, x_lat, ys_ctx, ys_lat, mod, norm_g, wout, w1, w2, w_in_next):
    ctx_x = pl.BlockSpec((TOK_TILE, D_MODEL), lambda i: (tiles.ctx_idx(i), 0))
    lat_x = pl.BlockSpec((TOK_TILE, D_MODEL), lambda i: (tiles.lat_idx(i), 0))
    ctx_y = pl.BlockSpec((TOK_TILE, GROUP_W), lambda i: (tiles.ctx_idx(i), 0))
    lat_y = pl.BlockSpec((TOK_TILE, GROUP_W), lambda i: (tiles.lat_idx(i), 0))
    single = lambda a: pl.BlockSpec(a.shape, lambda i: (0,) * a.ndim, pipeline_mode=pl.Buffered(1))
    in_specs = [ctx_x, lat_x] + [ctx_y] * 4 + [lat_y] * 4 + [
        pl.BlockSpec((None, None, MOD_CH, D_MODEL), lambda i: (l, tiles.seq_of(i), 0, 0)),
        _layer_spec(norm_g, l), single(wout), single(w1), single(w2)]
    out_specs = [ctx_x, lat_x]
    out_shape = [jax.ShapeDtypeStruct(x_ctx.shape, F32), jax.ShapeDtypeStruct(x_lat.shape, F32)]
    args = [x_ctx, x_lat, *ys_ctx, *ys_lat, mod, norm_g, wout, w1, w2]
    cast_next = w_in_next is not None
    if cast_next:
        assert tiles.n >= CAST_STEPS
        win_spec, wo_specs, wo_shape = _win_cast_specs(w_in_next, l + 1, lambda i: i)
        in_specs.append(win_spec)
        out_specs += wo_specs
        out_shape += wo_shape
        args.append(w_in_next)
    outs = pl.pallas_call(
        functools.partial(_outmlp_kernel, tiles.n_ctx, cast_next),
        grid=(tiles.n,),
        in_specs=in_specs,
        out_specs=out_specs,
        out_shape=out_shape,
        compiler_params=_cparams(("arbitrary",)),
        name="out_mlp",
    )(*args)
    return outs[0], outs[1], tuple(outs[2:])


def _mixer_call(kernel_fn, name, p, width, T, n_seq, blk0, nb, l, params, consts, state, state_blk,
                scratch, single_buffer_p=False):
    assert n_seq % nb == 0 and blk0 % nb == 0
    st_arr, st_in = state
    st_spec = pl.BlockSpec((nb, None) + state_blk, lambda i: (i, l) + (0,) * len(state_blk))
    p_mode = dict(pipeline_mode=pl.Buffered(1)) if single_buffer_p else {}
    in_specs = [pl.BlockSpec((nb * T, width), lambda i: (blk0 // nb + i, 0), **p_mode)]
    in_specs += [_layer_spec(a, l) for a in params] + [_const_spec(a) for a in consts]
    y_spec = pl.BlockSpec((nb * T, GROUP_W), lambda i: (i, 0))
    y_shape = jax.ShapeDtypeStruct((n_seq * T, GROUP_W), F32)
    if st_in:
        in_specs.append(st_spec)
        out_specs, out_shape, aliases = [y_spec], [y_shape], {}
    else:
        fresh = isinstance(st_arr, jax.ShapeDtypeStruct)
        in_specs.append(pl.BlockSpec(memory_space=pl.ANY))
        out_specs = [y_spec, st_spec]
        out_shape = [y_shape, jax.ShapeDtypeStruct(st_arr.shape, st_arr.dtype)]
        aliases = {} if fresh else {len(in_specs) - 1: 1}
        if fresh:
            st_arr = p
    return pl.pallas_call(
        kernel_fn,
        grid=(n_seq // nb,),
        in_specs=in_specs,
        out_specs=out_specs,
        out_shape=out_shape,
        input_output_aliases=aliases,
        scratch_shapes=scratch,
        compiler_params=_cparams(("arbitrary",)),
        name=name,
    )(p, *params, *consts, st_arr)


def _lru_kernel(T, has_init, p_ref, cw_ref, cb_ref, wcat_ref, bcat_ref, lam_ref, st_ref, y_ref, *rest):
    fin_ref = None if has_init else rest[0]
    a_sc, b_sc, h_sc, al_sc, bl_sc, c_sc = rest[-6:]
    SB = SCAN_BLOCK
    n_blk = T // SB
    n_slab = GROUP_W // PAIR_W
    ybr = p_ref[:, 0:GROUP_W]
    xbr = p_ref[:, GROUP_W:2 * GROUP_W]
    xc = _conv_rows(xbr, cw_ref, cb_ref)
    gates = _sigmoid(_mm(xc, wcat_ref[...]) + bcat_ref[...])
    sp = _softplus(-lam_ref[...])
    chains = [(d, hv) for d in range(2) for hv in range(n_slab)]
    for d in range(2):
        r = gates[:, (2 * d) * GROUP_W:(2 * d + 1) * GROUP_W]
        ig = gates[:, (2 * d + 1) * GROUP_W:(2 * d + 2) * GROUP_W]
        log_a = (-LRU_C) * r * sp[d:d + 1, :]
        a = jnp.exp(log_a)
        b = jnp.sqrt(jnp.maximum(1.0 - a * a, 0.0)) * (ig * xc)
        for hv in range(n_slab):
            a_sc[d, hv] = a[:, hv * PAIR_W:(hv + 1) * PAIR_W]
            b_sc[d, hv] = b[:, hv * PAIR_W:(hv + 1) * PAIR_W]

    def rows_of(ref, r):
        return ref[pl.ds(r, n_blk, stride=SB), :]

    acc, bcc = {}, {}
    for d, hv in chains:
        order = list(range(SB)) if d == 0 else list(range(SB - 1, -1, -1))
        prev = None
        for r in order:
            a_r, b_r = rows_of(a_sc.at[d, hv], r), rows_of(b_sc.at[d, hv], r)
            if prev is None:
                acc[d, hv, r], bcc[d, hv, r] = a_r, b_r
            else:
                acc[d, hv, r] = a_r * acc[d, hv, prev]
                bcc[d, hv, r] = a_r * bcc[d, hv, prev] + b_r
            prev = r
        al_sc[d, hv] = acc[d, hv, prev]
        bl_sc[d, hv] = bcc[d, hv, prev]

    def body(j, carry):
        new = []
        for (d, hv), c in zip(chains, carry):
            jb = j if d == 0 else n_blk - 1 - j
            c_sc[d, hv, pl.ds(jb, 1), :] = c
            new.append(al_sc[d, hv, pl.ds(jb, 1), :] * c + bl_sc[d, hv, pl.ds(jb, 1), :])
        return tuple(new)

    if has_init:
        carry0 = tuple(st_ref[0, d:d + 1, hv * PAIR_W:(hv + 1) * PAIR_W] for d, hv in chains)
    else:
        carry0 = tuple(jnp.zeros((1, PAIR_W), F32) for _ in chains)
    last = lax.fori_loop(0, n_blk, body, carry0, unroll=SB)
    for (d, hv), c in zip(chains, last):
        if not has_init:
            fin_ref[0, d:d + 1, hv * PAIR_W:(hv + 1) * PAIR_W] = c
        cin = c_sc[d, hv]
        for r in range(SB):
            h_sc[d, hv, pl.ds(r, n_blk, stride=SB), :] = acc[d, hv, r] * cin + bcc[d, hv, r]
    total = jnp.concatenate([h_sc[0, hv] + h_sc[1, hv] for hv in range(n_slab)], axis=-1)
    y_ref[...] = total * _gelu_tanh(ybr)


def _lru_call(p, T, n_seq, blk0, l, params, state):
    n_slab = GROUP_W // PAIR_W
    full = pltpu.VMEM((2, n_slab, T, PAIR_W), F32)
    edge = pltpu.VMEM((2, n_slab, T // SCAN_BLOCK, PAIR_W), F32)
    return _mixer_call(functools.partial(_lru_kernel, T, state[1]), "rglru_T%d" % T, p, LR_COLS, T,
                       n_seq, blk0, 1, l, params, (), state, (2, GROUP_W),
                       [full, full, full, edge, edge, edge])


def _ssd_kernel(T, has_init, p_ref, cw_ref, cb_ref, dtb_ref, aneg_ref, dexp_ref, nrm_ref, st_ref,
                y_ref, *rest):
    fin_ref = None if has_init else rest[0]
    Q = SSD_CHUNK
    nq = T // Q
    z = p_ref[:, 0:GROUP_W]
    xbc = _silu(_conv_rows(p_ref[:, GROUP_W:4 * GROUP_W], cw_ref, cb_ref))
    xs = xbc[:, 0:GROUP_W]
    bm = xbc[:, GROUP_W:2 * GROUP_W]
    cm = xbc[:, 2 * GROUP_W:3 * GROUP_W]
    dt = _softplus(p_ref[:, 4 * GROUP_W:4 * GROUP_W + PAIR_W] + dtb_ref[...])
    da = dt * aneg_ref[...]
    da_sc, ac_sc, ct_sc = rest[-3:]
    da_sc[...] = da
    _seg_cumsum_slab(da_sc, ac_sc.at[0], ct_sc, T, Q, rev=False)
    _seg_cumsum_slab(da_sc, ac_sc.at[1], ct_sc, T, Q, rev=True)
    acf = ac_sc[0]
    acb = ac_sc[1]
    ti = _row_iota((Q, Q))
    si = _lane_iota((Q, Q))
    low = si <= ti
    upp = si >= ti
    lane_lo = _lane_iota((Q, PAIR_W)) < HEAD_DIM
    row_lo = _row_iota((PAIR_W, SSD_STATE)) < HEAD_DIM

    def col(a, j):
        return a[:, j:j + 1]

    ydiag, ds, dec, ecol = [], [], [], []
    for q in range(nq):
        sl = slice(q * Q, (q + 1) * Q)
        acf_q, acb_q, dt_q = acf[sl], acb[sl], dt[sl]
        acf_t, acb_t, ldt_t = acf_q.T, acb_q.T, jnp.log(dt_q).T
        y_q, ds_q, dec_q, ec_q = [], [], [], []
        for pr in range(N_PAIRS):
            ls = slice(pr * PAIR_W, (pr + 1) * PAIR_W)
            bg, cg, xp = bm[sl, ls], cm[sl, ls], xs[sl, ls]
            g = _mm_nt(cg, bg)
            outs = []
            wf_cols, wb_cols, ef_cols, eb_cols, decf, decb = [], [], [], [], [], []
            for hh in range(2):
                h = 2 * pr + hh
                hb_ = N_HEADS + h
                lf = jnp.exp(col(acf_q, h) - (acf_t[h:h + 1, :] - ldt_t[h:h + 1, :]))
                lb = jnp.exp(col(acb_q, hb_) - (acb_t[hb_:hb_ + 1, :] - ldt_t[hb_:hb_ + 1, :]))
                mh = g * (jnp.where(low, lf, 0.0) + jnp.where(upp, lb, 0.0))
                outs.append(_mm(mh, xp))
                af_last = acf_q[Q - 1:Q, h:h + 1]
                ab_last = acb_q[0:1, hb_:hb_ + 1]
                wf_cols.append(jnp.exp(af_last - col(acf_q, h)) * col(dt_q, h))
                wb_cols.append(jnp.exp(ab_last - col(acb_q, hb_)) * col(dt_q, hb_))
                ef_cols.append(jnp.exp(col(acf_q, h)))
                eb_cols.append(jnp.exp(col(acb_q, hb_)))
                decf.append(jnp.exp(af_last))
                decb.append(jnp.exp(ab_last))
            y_q.append(jnp.where(lane_lo, outs[0], outs[1]))
            wf = jnp.where(lane_lo, wf_cols[0], wf_cols[1])
            wb = jnp.where(lane_lo, wb_cols[0], wb_cols[1])
            ds_q.append((_mm_tn(xp * wf, bg), _mm_tn(xp * wb, bg)))
            dec_q.append((jnp.where(row_lo, decf[0], decf[1]), jnp.where(row_lo, decb[0], decb[1])))
            ec_q.append((jnp.where(lane_lo, ef_cols[0], ef_cols[1]),
                         jnp.where(lane_lo, eb_cols[0], eb_cols[1])))
        ydiag.append(y_q)
        ds.append(ds_q)
        dec.append(dec_q)
        ecol.append(ec_q)

    prev = [[[None, None] for _ in range(N_PAIRS)] for _ in range(nq)]
    for pr in range(N_PAIRS):
        for d in range(2):
            if has_init:
                s = jnp.concatenate([st_ref[0, d, 2 * pr], st_ref[0, d, 2 * pr + 1]], axis=0)
            else:
                s = jnp.zeros((PAIR_W, SSD_STATE), F32)
            order = range(nq) if d == 0 else range(nq - 1, -1, -1)
            for q in order:
                prev[q][pr][d] = s
                s = dec[q][pr][d] * s + ds[q][pr][d]
            if not has_init:
                fin_ref[0, d, 2 * pr] = s[0:HEAD_DIM, :]
                fin_ref[0, d, 2 * pr + 1] = s[HEAD_DIM:PAIR_W, :]

    ys = []
    for q in range(nq):
        sl = slice(q * Q, (q + 1) * Q)
        parts = []
        for pr in range(N_PAIRS):
            ls = slice(pr * PAIR_W, (pr + 1) * PAIR_W)
            yp = ydiag[q][pr]
            if has_init or nq > 1:
                cg = cm[sl, ls]
                for d in range(2):
                    yp = yp + _mm_nt(cg, prev[q][pr][d]) * ecol[q][pr][d]
            parts.append(yp)
        ys.append(jnp.concatenate(parts, axis=-1))
    y = jnp.concatenate(ys, axis=0) if nq > 1 else ys[0]
    y = y + dexp_ref[...] * xs
    y_ref[...] = _rms(y * _silu(z), nrm_ref[...])


def _ssd_call(p, T, n_seq, blk0, l, params, state):
    return _mixer_call(functools.partial(_ssd_kernel, T, state[1]), "ssd_T%d" % T, p, SS_PAD, T,
                       n_seq, blk0, 1, l, params, (), state, (2, N_HEADS, HEAD_DIM, SSD_STATE),
                       [pltpu.VMEM((T, PAIR_W), F32), pltpu.VMEM((2, T, PAIR_W), F32),
                        pltpu.VMEM((T // SUBLANES, PAIR_W), F32)])


def _hgrn_kernel(T, has_init, p_ref, la_ref, l1_ref, lbd_ref, nrm_ref, bd_ref, st_ref, y_ref, *rest):
    fin_ref = None if has_init else rest[0]
    q_sc, v_sc, k_sc, b_sc, o_sc, t_sc = rest[-6:]
    NB = HG_BLOCK
    n_blk = T // NB

    def put(dst, val):
        for pr in range(N_PAIRS):
            dst[pr] = val[:, pr * PAIR_W:(pr + 1) * PAIR_W]

    put(q_sc, _silu(p_ref[:, 0:GROUP_W]))
    put(v_sc, p_ref[:, GROUP_W:2 * GROUP_W])
    for d in range(2):
        x = p_ref[:, (2 + d) * GROUP_W:(3 + d) * GROUP_W]
        e1 = jnp.exp(-jnp.abs(x))
        lsig = jnp.minimum(x, 0.0) - jnp.log(1.0 + e1)
        a_ = la_ref[d:d + 1, :] + jnp.zeros_like(x)
        b_ = l1_ref[d:d + 1, :] + lsig
        logf = jnp.maximum(a_, b_) + jnp.log(1.0 + jnp.exp(-jnp.abs(a_ - b_)))
        put(k_sc.at[d], jnp.exp(l1_ref[d:d + 1, :]) * (jnp.where(x >= 0.0, e1, 1.0) / (1.0 + e1))
            - lbd_ref[d:d + 1, :])
        put(b_sc.at[d], logf)
        for pr in range(N_PAIRS):
            _seg_cumsum_slab(b_sc.at[d, pr], b_sc.at[d, pr], t_sc, T, NB, rev=(d == 1))
    bd = bd_ref[0:PAIR_W, 0:PAIR_W]
    rows8 = _row_iota((SUBLANES, PAIR_W))
    bdm = (_row_iota((PAIR_W, PAIR_W)) < HEAD_DIM) == (_lane_iota((PAIR_W, PAIR_W)) < HEAD_DIM)
    chains = [(d, pr) for d in range(2) for pr in range(N_PAIRS)]
    n_half = NB // SUBLANES

    def body(j, states):
        new_states = []
        for d, pr in chains:
            jb = j if d == 0 else n_blk - 1 - j
            r0 = pl.multiple_of(jb * NB, NB)
            blk = pl.ds(r0, NB)
            qb, vb, kb, bb = q_sc[pr, blk, :], v_sc[pr, blk, :], k_sc[d, pr, blk, :], b_sc[d, pr, blk, :]

            def row(ref, s):
                return jnp.broadcast_to(ref[pl.ds(r0 + s, 1), :], (SUBLANES, PAIR_W))

            parts, where = [], []
            for s in range(NB):
                b_s, k_s = row(b_sc.at[d, pr], s), row(k_sc.at[d, pr], s)
                for hf in range(n_half):
                    t0 = hf * SUBLANES
                    t1 = t0 + SUBLANES - 1
                    if (t1 < s) if d == 0 else (t0 > s):
                        continue
                    rs = slice(t0, t0 + SUBLANES)
                    pr_ = qb[rs] * jnp.exp(bb[rs] - b_s) * k_s
                    if not ((t0 >= s) if d == 0 else (t1 <= s)):
                        keep = (rows8 + t0 >= s) if d == 0 else (rows8 + t0 <= s)
                        pr_ = jnp.where(keep, pr_, 0.0)
                    parts.append(pr_)
                    where.append((s, hf))
            att = _mm(jnp.concatenate(parts, axis=0), bd)
            halves = [jnp.zeros((SUBLANES, PAIR_W), F32) for _ in range(n_half)]
            for i, (s, hf) in enumerate(where):
                halves[hf] = halves[hf] + att[i * SUBLANES:(i + 1) * SUBLANES, :] * row(v_sc.at[pr], s)
            o = jnp.concatenate(halves, axis=0)
            bl = bb[NB - 1:NB, :] if d == 0 else bb[0:1, :]
            st = states[d * N_PAIRS + pr]
            o_sc[d, blk, pr * PAIR_W:(pr + 1) * PAIR_W] = o + _mm_nt(qb * jnp.exp(bb), st)
            new_states.append(st * jnp.exp(bl)
                              + jnp.where(bdm, _mm_tn(vb, kb * jnp.exp(bl - bb)), 0.0))
        return tuple(new_states)

    if has_init:
        states0 = tuple(_pair_blockdiag(st_ref[0, d, 2 * pr], st_ref[0, d, 2 * pr + 1]).T
                        for d, pr in chains)
    else:
        states0 = tuple(jnp.zeros((PAIR_W, PAIR_W), F32) for _ in chains)
    states = lax.fori_loop(0, n_blk, body, states0, unroll=HG_UNROLL)
    if not has_init:
        for (d, pr), s_ in zip(chains, states):
            kv = s_.T
            for hh in range(2):
                fin_ref[0, d, 2 * pr + hh] = _pair_diag(kv, hh)
    g = p_ref[:, 4 * GROUP_W:5 * GROUP_W]
    y_ref[...] = _rms(o_sc[0] + o_sc[1], nrm_ref[...]) * _silu(g)


def _hgrn_call(p, T, n_seq, blk0, l, params, bd, state):
    pw = pltpu.VMEM((N_PAIRS, T, PAIR_W), F32)
    pw2 = pltpu.VMEM((2, N_PAIRS, T, PAIR_W), F32)
    return _mixer_call(functools.partial(_hgrn_kernel, T, state[1]), "hgrn2_T%d" % T, p, HG_COLS, T,
                       n_seq, blk0, 1, l, params, (bd,), state, (2, N_HEADS, HEAD_DIM, HEAD_DIM),
                       [pw, pw, pw2, pw2, pltpu.VMEM((2, T, GROUP_W), F32),
                        pltpu.VMEM((T // SUBLANES, PAIR_W), F32)])


def _rwkv_kernel(T, nb, lat, p_ref, mu_ref, w0_ref, a0_ref, wl_ref, kk_ref, ka_ref, rk_ref,
                 lnw_ref, lnb_ref, sm_ref, bd_ref, st_ref, y_ref, *rest):
    fin_ref = None if lat else rest[0]
    xa_sc, xr_sc, yb_sc, yk_sc, pc_sc, v_sc, yo_sc, g_sc, rk_sc, cs_sc, ct_sc = rest[-11:]
    C = RW_CHUNK
    n_chunk = T // C
    bd = bd_ref[...]
    rowi = _row_iota((T, RW_PAD))
    if lat:
        gcol = rowi % GRID_W
        shifts = [(-1, gcol >= 1), (1, gcol <= GRID_W - 2), (-GRID_W, rowi >= GRID_W),
                  (GRID_W, rowi < T - GRID_W)]
    else:
        shifts = [(-1, rowi >= 1), (1, rowi <= T - 2)]
    ln = _lane_iota((T, PAIR_W))
    n_t = 2 * RW_LORA_W
    n_i = 4 * RW_LORA_W
    n_s = n_i + RW_LORA_G
    for sq in range(nb):
        rows = slice(sq * T, (sq + 1) * T)
        P = p_ref[rows, :]
        sh = jnp.zeros_like(P)
        for j, (dlt, valid) in enumerate(shifts):
            sh = jnp.where(valid & (sm_ref[j:j + 1, :] > 0.5), _shift_rows(P, dlt), sh)
        P = P + (sh - P) * mu_ref[...]
        r = P[:, 0:GROUP_W]
        k = P[:, GROUP_W:2 * GROUP_W]
        lo = P[:, 3 * GROUP_W:3 * GROUP_W + PAIR_W]
        act = jnp.where(ln < n_t, jnp.tanh(lo),
                        jnp.where(ln < n_i, lo, jnp.where(ln < n_s, _sigmoid(lo), 0.0)))
        lora = _mm(act, wl_ref[...])
        g_sc[rows, :] = lora[:, 4 * GROUP_W:5 * GROUP_W]
        kkv = k * kk_ref[...]
        kkn = kkv / jnp.maximum(jnp.sqrt(_mm_split(kkv * kkv, bd)), 1e-12)
        v_sc[rows, :] = P[:, 2 * GROUP_W:3 * GROUP_W]
        ksum = jnp.zeros_like(k)
        for d in range(2):
            w_lora = lora[:, d * GROUP_W:(d + 1) * GROUP_W]
            a_lora = lora[:, (2 + d) * GROUP_W:(3 + d) * GROUP_W]
            w_log = -_softplus(-(w0_ref[d:d + 1, :] + w_lora)) - 0.5
            logw = -jnp.exp(w_log)
            a_sig = _sigmoid(a0_ref[d:d + 1, :] + a_lora)
            k_d = k * (1.0 + (a_sig - 1.0) * ka_ref[...])
            ksum = ksum + k_d
            edge = C - 1 if d == 0 else 0
            for pr in range(N_PAIRS):
                ls = slice(pr * PAIR_W, (pr + 1) * PAIR_W)
                cs_sc[pr] = logw[:, ls]
                _seg_cumsum_slab(cs_sc.at[pr], cs_sc.at[pr], ct_sc, T, C, rev=(d == 1))
                pc_sc[d, pr, sq * n_chunk:(sq + 1) * n_chunk, :] = jnp.exp(
                    cs_sc[pr, pl.ds(edge, n_chunk, stride=C), :])
            c = jnp.concatenate([cs_sc[pr] for pr in range(N_PAIRS)], axis=-1)
            einv = jnp.exp(-c)
            xa_sc[d, rows, :] = ((-kkn) * jnp.exp(c - logw)).astype(BF16)
            xr_sc[d, rows, :] = (r * jnp.exp(c)).astype(BF16)
            yb_sc[d, rows, :] = ((kkn * a_sig) * einv).astype(BF16)
            yk_sc[d, rows, :] = (k_d * einv).astype(BF16)
        rk_sc[rows, :] = r * ksum * rk_ref[...]

    R2 = 2 * C
    assert R2 % PAIR_W == 0
    lane_lo = _lane_iota((C, PAIR_W)) < HEAD_DIM
    tt2 = _row_iota((R2, 2 * R2)) % C
    ss2 = _lane_iota((R2, 2 * R2)) % C
    n_fac = C.bit_length() - 1

    def stack2(x):
        z = jnp.zeros_like(x)
        return jnp.concatenate([jnp.where(lane_lo, x, z), jnp.where(lane_lo, z, x)], axis=0)

    chains = [(sq, d, pr) for sq in range(nb) for d in range(2) for pr in range(N_PAIRS)]
    each = lambda f, *cols: [f(*a) for a in zip(*cols)]

    def body(j, states):
        lhs, rhs_t, vst, pc, strict2, incl2, dst = [], [], [], [], [], [], []
        for sq, d, pr in chains:
            jc = j if d == 0 else n_chunk - 1 - j
            r0 = pl.multiple_of(sq * T + jc * C, C)
            ls = slice(pr * PAIR_W, (pr + 1) * PAIR_W)
            at = xa_sc[d, pl.ds(r0, C), ls]
            rt = xr_sc[d, pl.ds(r0, C), ls]
            bt = yb_sc[d, pl.ds(r0, C), ls]
            kt = yk_sc[d, pl.ds(r0, C), ls]
            pc.append(pc_sc[d, pr, pl.ds(sq * n_chunk + jc, 1), :])
            lhs.append(jnp.concatenate([stack2(at), stack2(rt)], axis=0))
            rhs_t.append(jnp.concatenate([stack2(bt), stack2(kt)], axis=0))
            vst.append(stack2(v_sc[pl.ds(r0, C), ls]))
            strict2.append((ss2 < tt2) if d == 0 else (ss2 > tt2))
            incl2.append((ss2 <= tt2) if d == 0 else (ss2 >= tt2))
            dst.append((d, r0, ls))
        gg = each(_mm_nt, lhs, rhs_t)
        uu = each(_mm_nt, lhs, states)
        am = each(lambda g, m: jnp.where(m, g[0:R2, :], 0.0), gg, strict2)
        ncat = each(lambda g, m: jnp.where(m, g[R2:2 * R2, :], 0.0), gg, incl2)
        qb = each(lambda a_: a_[:, 0:R2].astype(BF16), am)
        xs = each(lambda u, a_, v_: u[0:R2, :] + _mm(a_[:, R2:2 * R2], v_), uu, am, vst)
        for i in range(n_fac):
            xh = each(lambda x: x.astype(BF16), xs)
            if i < RW_SPLIT_STAGES:
                xl = each(lambda x, h: (x - h.astype(F32)).astype(BF16), xs, xh)
                corr = each(lambda q, l_: jnp.dot(q, l_, preferred_element_type=F32), qb, xl)
            else:
                corr = [0.0] * len(xs)
            if i < n_fac - 1:
                zz = each(lambda q, h: jnp.dot(q, jnp.concatenate([h, q], axis=1),
                                               preferred_element_type=F32), qb, xh)
                xs = each(lambda x, z, c_: x + z[:, 0:PAIR_W] + c_, xs, zz, corr)
                qb = each(lambda z: z[:, PAIR_W:PAIR_W + R2].astype(BF16), zz)
            else:
                xs = each(lambda x, q, h, c_: x + jnp.dot(q, h, preferred_element_type=F32) + c_,
                          xs, qb, xh, corr)
        sv = each(lambda x, v_: jnp.concatenate([x, v_], axis=0), xs, vst)
        yst = each(lambda u, n, s_: u[R2:2 * R2, :] + _mm(n, s_), uu, ncat, sv)
        for (d, r0, ls), y_ in zip(dst, yst):
            yo_sc[d, pl.ds(r0, C), ls] = y_[0:C, :] + y_[C:2 * C, :]
        return tuple(each(lambda s_, p, v_, r_: s_ * p + _mm_tn(v_, r_) * p, states, pc, sv, rhs_t))

    if lat:
        states0 = tuple(_pair_blockdiag(st_ref[sq, d, 2 * pr], st_ref[sq, d, 2 * pr + 1])
                        for sq, d, pr in chains)
    else:
        states0 = tuple(jnp.zeros((PAIR_W, PAIR_W), F32) for _ in chains)
    states = lax.fori_loop(0, n_chunk, body, states0)
    if not lat:
        for (sq, d, pr), s_ in zip(chains, states):
            for hh in range(2):
                fin_ref[sq, d, 2 * pr + hh] = _pair_diag(s_, hh)
    inv_n = 1.0 / HEAD_DIM
    for sq in range(nb):
        rows = slice(sq * T, (sq + 1) * T)
        y = yo_sc[0, rows, :] + yo_sc[1, rows, :]
        mu = _mm_split(y, bd) * inv_n
        yc = y - mu
        var = _mm_split(yc * yc, bd) * inv_n
        yn = yc * lax.rsqrt(var + RW_GN_EPS) * lnw_ref[...] + lnb_ref[...]
        bonus = _mm_split(rk_sc[rows, :], bd) * v_sc[rows, :]
        y_ref[rows, :] = (yn + bonus) * g_sc[rows, :]


def _rwkv_call(p, T, n_seq, blk0, nb, lat, l, params, sm, bd, state):
    tw = pltpu.VMEM((nb * T, GROUP_W), F32)
    tw2 = pltpu.VMEM((2, nb * T, GROUP_W), F32)
    op2 = pltpu.VMEM((2, nb * T, GROUP_W), BF16)
    return _mixer_call(functools.partial(_rwkv_kernel, T, nb, lat), "rwkv7_T%d" % T, p, RW_PAD, T,
                       n_seq, blk0, nb, l, params, (sm, bd), state,
                       (2, N_HEADS, HEAD_DIM, HEAD_DIM),
                       [op2, op2, op2, op2,
                        pltpu.VMEM((2, N_PAIRS, nb * T // RW_CHUNK, PAIR_W), F32),
                        tw, tw2, tw, tw,
                        pltpu.VMEM((N_PAIRS, T, PAIR_W), F32),
                        pltpu.VMEM((T // SUBLANES, PAIR_W), F32)],
                       single_buffer_p=(nb * T >= RW_SINGLE_BUF_ROWS))


def _pad_last(a, n):
    return jnp.pad(a, ((0, 0),) * (a.ndim - 1) + ((0, n - a.shape[-1]),))


def _row(a):
    return a.reshape(a.shape[0], 1, -1)


def _heads_blockdiag(w):
    eye = jnp.eye(N_HEADS, dtype=w.dtype)
    full = w[..., :, :, None, :] * eye[:, None, :, None]
    return full.reshape(w.shape[:-3] + (GROUP_W, GROUP_W))


def kernel(x_prompt, x_sample, state_rwkv, state_hgrn, state_ssd, state_lru, c, c_ctx, mod_w, mod_b, norm_g, w_in, w_out, rw_mu, rw_w0, rw_w2, rw_a0, rw_a2, rw_g2, rw_kk, rw_ka, rw_rk, rw_ln_w, rw_ln_b, hg_lb, hg_norm, ss_conv_w, ss_conv_b, ss_dt_bias, ss_A_log, ss_D, ss_norm, lr_conv_w, lr_conv_b, lr_wa, lr_ba, lr_wx, lr_bx, lr_lam, mlp_w1, mlp_w2):
    bp, t_ctx, _ = x_prompt.shape
    bl, t_lat, _ = x_sample.shape
    n_ctx_tok = bp * t_ctx
    assert n_ctx_tok % t_lat == 0 and t_lat % GRID_W == 0 and 1 + bl <= 8
    lat_blk0 = n_ctx_tok // t_lat
    tiles = _Tiles(n_ctx_tok, bl * t_lat, t_lat)

    x_ctx = x_prompt.reshape(n_ctx_tok, D_MODEL)
    x_lat = x_sample.reshape(bl * t_lat, D_MODEL)
    cond8 = jnp.zeros((8, D_MODEL), F32).at[0].set(c_ctx).at[1:1 + bl].set(c)
    mod, ws_in = _mod_call(cond8, mod_w, mod_b, w_in)
    mod = mod.reshape(DEPTH, 8, MOD_CH, D_MODEL)

    lane = jnp.arange(GROUP_W)
    bd_ones = (lane[:, None] // HEAD_DIM == lane[None, :] // HEAD_DIM).astype(BF16)

    col = jnp.arange(RW_PAD)
    quarter = col // RW_SHIFT_SPLIT
    real = col < RW_COLS
    sm_lat = jnp.stack([(quarter == i) & real for i in range(4)]).astype(F32)
    sm_ctx = jnp.stack([(quarter < 2) & real, (quarter >= 2) & real,
                        jnp.zeros_like(real), jnp.zeros_like(real)]).astype(F32)
    lora_blocks = (rw_w2[:, 0], rw_w2[:, 1], rw_a2[:, 0], rw_a2[:, 1], rw_g2)
    wl = jnp.concatenate(
        [jnp.pad(blk, ((0, 0), (0, 0), (j * GROUP_W, (len(lora_blocks) - 1 - j) * GROUP_W)))
         for j, blk in enumerate(lora_blocks)], axis=1)
    wl = jnp.pad(wl, ((0, 0), (0, PAIR_W - wl.shape[1]), (0, 0))).astype(BF16)
    rw_params = (_pad_last(_row(rw_mu), RW_PAD), rw_w0, rw_a0, wl, _row(rw_kk), _row(rw_ka),
                 _row(rw_rk), _row(rw_ln_w), _row(rw_ln_b))

    lb_soft = jax.nn.softmax(hg_lb.astype(F32), axis=0)
    lower = jnp.cumsum(lb_soft, axis=0) - lb_soft[0]
    lb_floor = jnp.maximum(lower, LB_FLOOR)
    hg_params = (jnp.log(lb_floor), jnp.log1p(-lower), lb_floor - lower, _row(hg_norm))

    ss_params = (ss_conv_w, _row(ss_conv_b), _pad_last(_row(ss_dt_bias), PAIR_W),
                 _pad_last(_row(-jnp.exp(ss_A_log.astype(F32))), PAIR_W),
                 _row(jnp.repeat(ss_D, HEAD_DIM, axis=-1)), _row(ss_norm))

    wcat = jnp.concatenate([_heads_blockdiag(lr_wa[:, 0]), _heads_blockdiag(lr_wx[:, 0]),
                            _heads_blockdiag(lr_wa[:, 1]), _heads_blockdiag(lr_wx[:, 1])],
                           axis=-1).astype(BF16)
    bcat = _row(jnp.stack([lr_ba[:, 0], lr_bx[:, 0], lr_ba[:, 1], lr_bx[:, 1]], axis=1))
    lr_params = (lr_conv_w, _row(lr_conv_b), wcat, bcat, lr_lam)

    new_rw = jax.ShapeDtypeStruct((bp, DEPTH, 2, N_HEADS, HEAD_DIM, HEAD_DIM), F32)
    new_hg = jax.ShapeDtypeStruct((bp, DEPTH, 2, N_HEADS, HEAD_DIM, HEAD_DIM), F32)
    new_ss = jax.ShapeDtypeStruct((bp, DEPTH, 2, N_HEADS, HEAD_DIM, SSD_STATE), F32)
    new_lr = jax.ShapeDtypeStruct((bp, DEPTH, 2, GROUP_W), F32)

    for l in range(DEPTH):
        (p_rw, p_hg, p_ss, p_lr), mlp_b = _inproj_call(tiles, l, x_ctx, x_lat, mod, norm_g, ws_in,
                                                       (w_out, mlp_w1, mlp_w2))

        y_rw_c, new_rw = _rwkv_call(p_rw, t_ctx, bp, 0, RW_CTX_SEQS, False, l, rw_params, sm_ctx,
                                    bd_ones, (new_rw, False))
        y_rw_l, = _rwkv_call(p_rw, t_lat, bl, lat_blk0, RW_LAT_SEQS, True, l, rw_params, sm_lat,
                             bd_ones, (state_rwkv, True))
        y_hg_c, new_hg = _hgrn_call(p_hg, t_ctx, bp, 0, l, hg_params, bd_ones, (new_hg, False))
        y_hg_l, = _hgrn_call(p_hg, t_lat, bl, lat_blk0, l, hg_params, bd_ones, (state_hgrn, True))
        y_ss_c, new_ss = _ssd_call(p_ss, t_ctx, bp, 0, l, ss_params, (new_ss, False))
        y_ss_l, = _ssd_call(p_ss, t_lat, bl, lat_blk0, l, ss_params, (state_ssd, True))
        y_lr_c, new_lr = _lru_call(p_lr, t_ctx, bp, 0, l, lr_params, (new_lr, False))
        y_lr_l, = _lru_call(p_lr, t_lat, bl, lat_blk0, l, lr_params, (state_lru, True))

        x_ctx, x_lat, ws_in = _outmlp_call(tiles, l, x_ctx, x_lat, (y_rw_c, y_hg_c, y_ss_c, y_lr_c),
                                           (y_rw_l, y_hg_l, y_ss_l, y_lr_l), mod, norm_g, *mlp_b,
                                           w_in if l + 1 < DEPTH else None)

    return (x_ctx.reshape(bp, t_ctx, D_MODEL), x_lat.reshape(bl, t_lat, D_MODEL),
            new_rw, new_hg, new_ss, new_lr)
```
